```python
import math
import jax
import jax.numpy as jnp
from jax import lax
import numpy as np

D_MODEL = 2048
BATCH = 4
SEQ = 4096
DEPTH = 1

GRID_W = 64
CTX_LEN = 256
EPS = 1e-6

N_Q_HEADS = 8
N_KV_HEADS = 2
Q_PER_KV = N_Q_HEADS // N_KV_HEADS
HEAD_DIM = 128
ATTN_WIDTH = N_Q_HEADS * HEAD_DIM
KV_WIDTH = N_KV_HEADS * HEAD_DIM
WINDOW = 128
ATTN_BLOCK = 128
ROPE_THETA = 10000.0
ROPE_AXIS_DIM = HEAD_DIM // 2

SSD_HEADS = 16
SSD_HEAD_DIM = 64
SSD_WIDTH = SSD_HEADS * SSD_HEAD_DIM
SSD_GROUPS = 4
SSD_HEADS_PER_GROUP = SSD_HEADS // SSD_GROUPS
SSD_STATE = 128
SSD_CONV = 5
SSD_CHUNK = 128
N_DIRS = 2
XBC_WIDTH = SSD_WIDTH + 2 * SSD_GROUPS * SSD_STATE

MIX_WIDTH = ATTN_WIDTH + SSD_WIDTH
IN_SPLITS = (ATTN_WIDTH, KV_WIDTH, KV_WIDTH, SSD_WIDTH, XBC_WIDTH, N_DIRS * SSD_HEADS)
IN_WIDTH = ATTN_WIDTH + 2 * KV_WIDTH + SSD_WIDTH + XBC_WIDTH + N_DIRS * SSD_HEADS

N_EXPERTS = 64
N_EXPERT_GROUPS = 8
EXPERTS_PER_GROUP = N_EXPERTS // N_EXPERT_GROUPS
TOPK_GROUPS = 4
TOP_K = 8
EXPERT_DIM = 512
SHARED_DIM = 512
ROUTED_SCALE = 2.5
MOE_BLOCK = 128

kernel_name = "hybrid_attn_ssd_moe_dit_layer"


def rmsnorm(x, g):
    xf = x.astype(jnp.float32)
    y = xf * lax.rsqrt(jnp.mean(xf * xf, axis=-1, keepdims=True) + EPS)
    return (y * g.astype(jnp.float32)).astype(x.dtype)


def modulate(h, shift, scale):
    return h * (1 + scale) + shift


def split_cols(p):
    idx = np.cumsum(IN_SPLITS)[:-1].tolist()
    return jnp.split(p, idx, axis=-1)


def rope_tables(pos, dim):
    inv = ROPE_THETA ** (-jnp.arange(0, dim, 2, dtype=jnp.float32) / dim)
    ang = pos.astype(jnp.float32)[:, None] * inv[None, :]
    return jnp.cos(ang), jnp.sin(ang)


def rotate_half_axis(x, cos, sin):
    half = x.shape[-1] // 2
    shape = (1, cos.shape[0]) + (1,) * (x.ndim - 3) + (half,)
    c = cos.reshape(shape).astype(x.dtype)
    s = sin.reshape(shape).astype(x.dtype)
    x1, x2 = x[..., :half], x[..., half:]
    return jnp.concatenate([x1 * c - x2 * s, x1 * s + x2 * c], axis=-1)


def axial_rope(x, rows, cols):
    cr, sr = rope_tables(rows, ROPE_AXIS_DIM)
    cc, sc = rope_tables(cols, ROPE_AXIS_DIM)
    return jnp.concatenate([rotate_half_axis(x[..., :ROPE_AXIS_DIM], cr, sr),
                            rotate_half_axis(x[..., ROPE_AXIS_DIM:], cc, sc)], axis=-1)


def windowed_attention(q, k, v, k_ctx, v_ctx, sink):
    b, n = q.shape[:2]
    nb = n // ATTN_BLOCK
    scale = HEAD_DIM ** -0.5
    qb = q.reshape(b, nb, ATTN_BLOCK, N_KV_HEADS, Q_PER_KV, HEAD_DIM)

    def band(t):
        tp = jnp.pad(t, ((0, 0), (ATTN_BLOCK, ATTN_BLOCK), (0, 0), (0, 0)))
        tp = tp.reshape(b, nb + 2, ATTN_BLOCK, N_KV_HEADS, HEAD_DIM)
        return jnp.concatenate([tp[:, :-2], tp[:, 1:-1], tp[:, 2:]], axis=2)

    kw, vw = band(k), band(v)
    s_loc = jnp.einsum('bnqhgd,bnkhd->bnhgqk', qb, kw).astype(jnp.float32) * scale
    blk = jnp.arange(nb)[:, None, None]
    qpos = blk * ATTN_BLOCK + jnp.arange(ATTN_BLOCK)[None, :, None]
    kpos = (blk - 1) * ATTN_BLOCK + jnp.arange(3 * ATTN_BLOCK)[None, None, :]
    valid = (jnp.abs(qpos - kpos) <= WINDOW) & (kpos >= 0) & (kpos < n)
    s_loc = jnp.where(valid[None, :, None, None], s_loc, -jnp.inf)
    s_ctx = jnp.einsum('bnqhgd,bchd->bnhgqc', qb, k_ctx).astype(jnp.float32) * scale
    s_sink = jnp.broadcast_to(sink.astype(jnp.float32)[None, None, :, :, None, None],
                              s_loc.shape[:-1] + (1,))
    p = jax.nn.softmax(jnp.concatenate([s_loc, s_ctx, s_sink], axis=-1), axis=-1)
    n_loc = 3 * ATTN_BLOCK
    p_loc = p[..., :n_loc].astype(v.dtype)
    p_ctx = p[..., n_loc:n_loc + k_ctx.shape[1]].astype(v.dtype)
    o = (jnp.einsum('bnhgqk,bnkhd->bnqhgd', p_loc, vw)
         + jnp.einsum('bnhgqc,bchd->bnqhgd', p_ctx, v_ctx))
    return o.reshape(b, n, ATTN_WIDTH)


def context_attention(q, k, v, sink):
    b, n = q.shape[:2]
    s = jnp.einsum('bqhgd,bkhd->bhgqk', q, k).astype(jnp.float32) * (HEAD_DIM ** -0.5)
    s_sink = jnp.broadcast_to(sink.astype(jnp.float32)[None, :, :, None, None], s.shape[:-1] + (1,))
    p = jax.nn.softmax(jnp.concatenate([s, s_sink], axis=-1), axis=-1)[..., :-1]
    o = jnp.einsum('bhgqk,bkhd->bqhgd', p.astype(v.dtype), v)
    return o.reshape(b, n, ATTN_WIDTH)


def dwconv_centred(u, w, bias):
    out = lax.conv_general_dilated(u, w[:, None, :], window_strides=(1,),
                                   padding=[(SSD_CONV // 2, SSD_CONV // 2)],
                                   dimension_numbers=('NWC', 'WIO', 'NWC'),
                                   feature_group_count=u.shape[-1])
    return out + bias


def conv_split(xbc, conv_w, conv_b):
    b, n, _ = xbc.shape
    u = jax.nn.silu(dwconv_centred(xbc, conv_w, conv_b))
    xs, bm, cm = jnp.split(u, [SSD_WIDTH, SSD_WIDTH + SSD_GROUPS * SSD_STATE], axis=-1)
    return (xs.reshape(b, n, SSD_GROUPS, SSD_HEADS_PER_GROUP, SSD_HEAD_DIM),
            bm.reshape(b, n, SSD_GROUPS, SSD_STATE),
            cm.reshape(b, n, SSD_GROUPS, SSD_STATE))


def ssd_chunked(x, dt, a, bm, cm, h0):
    b, n = x.shape[:2]
    nc, q = n // SSD_CHUNK, SSD_CHUNK
    x = x.reshape(b, nc, q, SSD_GROUPS, SSD_HEADS_PER_GROUP, SSD_HEAD_DIM)
    dt = dt.reshape(b, nc, q, SSD_GROUPS, SSD_HEADS_PER_GROUP)
    bm = bm.reshape(b, nc, q, SSD_GROUPS, SSD_STATE)
    cm = cm.reshape(b, nc, q, SSD_GROUPS, SSD_STATE)
    a_cum = jnp.cumsum(dt * a, axis=2)
    tril = jnp.tril(jnp.ones((q, q), dtype=bool))[None, None, :, :, None, None]
    seg = a_cum[:, :, :, None] - a_cum[:, :, None, :]
    decay_in = jnp.exp(jnp.where(tril, seg, -jnp.inf))
    xd = x * dt[..., None]
    cb = jnp.einsum('bclgn,bcsgn->bclsg', cm, bm)
    y_diag = jnp.einsum('bclsgr,bcsgrp->bclgrp', cb[..., None] * decay_in, xd)
    decay_out = jnp.exp(a_cum[:, :, -1:] - a_cum)
    states = jnp.einsum('bcsgn,bcsgrp->bcgrpn', bm, xd * decay_out[..., None])
    chunk_decay = jnp.exp(a_cum[:, :, -1])

    def step(h, inp):
        dec, st = inp
        return h * dec[..., None, None] + st, h

    h_last, h_prev = lax.scan(step, h0.astype(states.dtype),
                              (jnp.moveaxis(chunk_decay, 1, 0), jnp.moveaxis(states, 1, 0)))
    h_prev = jnp.moveaxis(h_prev, 0, 1)
    y_off = jnp.einsum('bclgn,bcgrpn->bclgrp', cm, h_prev) * jnp.exp(a_cum)[..., None]
    y = (y_diag + y_off).reshape(b, n, SSD_GROUPS, SSD_HEADS_PER_GROUP, SSD_HEAD_DIM)
    return y, h_last


def seq_flip(t, rev):
    return jnp.flip(t, axis=1) if rev else t


def bidirectional_ssd(xbc_c, dt_c, xbc_l, dt_l, conv_w, conv_b, dt_bias, a_log, d_skip):
    xs_c, b_c, c_c = conv_split(xbc_c, conv_w, conv_b)
    xs_l, b_l, c_l = conv_split(xbc_l, conv_w, conv_b)
    bsz, n_l = xbc_l.shape[:2]
    n_c = xbc_c.shape[1]
    gr = (SSD_GROUPS, SSD_HEADS_PER_GROUP)
    dt_c = dt_c.reshape((bsz, n_c, N_DIRS) + gr).astype(jnp.float32)
    dt_l = dt_l.reshape((bsz, n_l, N_DIRS) + gr).astype(jnp.float32)
    skip = d_skip.reshape(gr)[..., None]
    y_c = skip * xs_c
    y_l = skip * xs_l
    h0 = jnp.zeros((bsz,) + gr + (SSD_HEAD_DIM, SSD_STATE), jnp.float32)
    for d in range(N_DIRS):
        rev = d == 1
        a = -jnp.exp(a_log[d].astype(jnp.float32)).reshape(gr)
        bias = dt_bias[d].astype(jnp.float32).reshape(gr)
        dtc = jax.nn.softplus(dt_c[:, :, d] + bias)
        dtl = jax.nn.softplus(dt_l[:, :, d] + bias)
        yc, hc = ssd_chunked(seq_flip(xs_c, rev), seq_flip(dtc, rev), a,
                             seq_flip(b_c, rev), seq_flip(c_c, rev), h0)
        yl, _ = ssd_chunked(seq_flip(xs_l, rev), seq_flip(dtl, rev), a,
                            seq_flip(b_l, rev), seq_flip(c_l, rev), hc)
        y_c = y_c + seq_flip(yc, rev)
        y_l = y_l + seq_flip(yl, rev)
    return (y_l.reshape(bsz, n_l, SSD_WIDTH).astype(xbc_l.dtype),
            y_c.reshape(bsz, n_c, SSD_WIDTH).astype(xbc_c.dtype))


def token_mixer(h_lat, h_ctx, rows, cols, w_in, attn_sink, conv_w, conv_b, dt_bias, a_log,
                d_skip, ssd_norm_g, w_out, with_ctx_out):
    b, n, _ = h_lat.shape
    n_c = h_ctx.shape[1]
    q_l, k_l, v_l, z_l, xbc_l, dt_l = split_cols(h_lat @ w_in)
    q_c, k_c, v_c, z_c, xbc_c, dt_c = split_cols(h_ctx @ w_in)
    sink = attn_sink.reshape(N_KV_HEADS, Q_PER_KV)
    q_l = axial_rope(q_l.reshape(b, n, N_KV_HEADS, Q_PER_KV, HEAD_DIM), rows, cols)
    k_l = axial_rope(k_l.reshape(b, n, N_KV_HEADS, HEAD_DIM), rows, cols)
    v_l = v_l.reshape(b, n, N_KV_HEADS, HEAD_DIM)
    k_c = k_c.reshape(b, n_c, N_KV_HEADS, HEAD_DIM)
    v_c = v_c.reshape(b, n_c, N_KV_HEADS, HEAD_DIM)
    attn_l = windowed_attention(q_l, k_l, v_l, k_c, v_c, sink)
    ssd_l, ssd_c = bidirectional_ssd(xbc_c, dt_c, xbc_l, dt_l, conv_w, conv_b, dt_bias, a_log, d_skip)
    ssd_l = rmsnorm(ssd_l * jax.nn.silu(z_l), ssd_norm_g)
    out_l = jnp.concatenate([attn_l, ssd_l], axis=-1) @ w_out
    if not with_ctx_out:
        return out_l, None
    attn_c = context_attention(q_c.reshape(b, n_c, N_KV_HEADS, Q_PER_KV, HEAD_DIM), k_c, v_c, sink)
    ssd_c = rmsnorm(ssd_c * jax.nn.silu(z_c), ssd_norm_g)
    out_c = jnp.concatenate([attn_c, ssd_c], axis=-1) @ w_out
    return out_l, out_c


def moe_ffn(h, router_w, router_bias, w_gate, w_up, w_down, sw_gate, sw_up, sw_down):
    t, d = h.shape
    scores = jax.nn.sigmoid((h @ router_w).astype(jnp.float32))
    sel = scores + router_bias.astype(jnp.float32)
    grp_score = lax.top_k(sel.reshape(t, N_EXPERT_GROUPS, EXPERTS_PER_GROUP), 2)[0].sum(-1)
    _, grp_idx = lax.top_k(grp_score, TOPK_GROUPS)
    grp_keep = jnp.any(grp_idx[:, :, None] == jnp.arange(N_EXPERT_GROUPS)[None, None, :], axis=1)
    sel = jnp.where(jnp.repeat(grp_keep, EXPERTS_PER_GROUP, axis=1), sel, -jnp.inf)
    _, top_idx = lax.top_k(sel, TOP_K)
    top_s = jnp.take_along_axis(scores, top_idx, axis=1)
    top_w = ROUTED_SCALE * top_s / jnp.sum(top_s, axis=-1, keepdims=True)

    n_assign = t * TOP_K
    eid = top_idx.reshape(-1)
    tid = jnp.repeat(jnp.arange(t, dtype=jnp.int32), TOP_K)
    wts = top_w.reshape(-1)
    order = jnp.argsort(eid)
    eid_s, tid_s, w_s = eid[order], tid[order], wts[order]
    counts = jnp.bincount(eid, length=N_EXPERTS)
    padded = (counts + MOE_BLOCK - 1) // MOE_BLOCK * MOE_BLOCK
    raw_start = jnp.cumsum(counts) - counts
    pad_end = jnp.cumsum(padded)
    pad_start = pad_end - padded
    dest = pad_start[eid_s] + jnp.arange(n_assign, dtype=jnp.int32) - raw_start[eid_s]
    n_slots = -(-(n_assign + N_EXPERTS * (MOE_BLOCK - 1)) // MOE_BLOCK) * MOE_BLOCK
    n_blocks = n_slots // MOE_BLOCK
    slot_tok = jnp.full((n_slots,), t, jnp.int32).at[dest].set(tid_s)
    slot_w = jnp.zeros((n_slots,), jnp.float32).at[dest].set(w_s)
    blk_start = jnp.arange(n_blocks, dtype=jnp.int32) * MOE_BLOCK
    blk_exp = jnp.minimum(jnp.sum(blk_start[:, None] >= pad_end[None, :], axis=1), N_EXPERTS - 1)
    h_pad = jnp.concatenate([h, jnp.zeros((1, d), h.dtype)], axis=0)

    def block_step(y, inp):
        tok, w, e = inp
        xb = h_pad[tok]
        act = jax.nn.silu(xb @ w_gate[e]) * (xb @ w_up[e])
        out = (act @ w_down[e]) * w[:, None].astype(act.dtype)
        return y.at[tok].add(out.astype(y.dtype)), None

    y, _ = lax.scan(block_step, jnp.zeros((t + 1, d), h.dtype),
                    (slot_tok.reshape(n_blocks, MOE_BLOCK), slot_w.reshape(n_blocks, MOE_BLOCK), blk_exp))
    shared = (jax.nn.silu(h @ sw_gate) * (h @ sw_up)) @ sw_down
    return y[:t] + shared


def setup_inputs(seed: int = 0) -> dict:
    key = jax.random.key(seed)
    ks = jax.random.split(key, 26)
    f32 = jnp.float32

    def nrm(k, shape, scale):
        return jax.random.normal(k, shape, f32) * scale

    def gain(k, shape):
        return 1.0 + 0.01 * jax.random.normal(k, shape, f32)

    dt0 = jnp.exp(jax.random.uniform(ks[12], (DEPTH, N_DIRS, SSD_HEADS), f32,
                                     math.log(1e-3), math.log(1e-1)))
    return {
        "x": nrm(ks[0], (BATCH, SEQ, D_MODEL), 1.0),
        "c": nrm(ks[1], (BATCH, D_MODEL), 1.0),
        "ctx": nrm(ks[2], (BATCH, CTX_LEN, D_MODEL), 1.0),
        "c_ctx": nrm(ks[3], (D_MODEL,), 1.0),
        "w_ada": nrm(ks[4], (DEPTH, D_MODEL, 6 * D_MODEL), 0.5 * D_MODEL ** -0.5),
        "b_ada": nrm(ks[5], (DEPTH, 6 * D_MODEL), 0.01),
        "norm1_g": gain(ks[6], (DEPTH, D_MODEL)),
        "norm2_g": gain(ks[7], (DEPTH, D_MODEL)),
        "w_in": nrm(ks[8], (DEPTH, D_MODEL, IN_WIDTH), D_MODEL ** -0.5),
        "attn_sink": nrm(ks[9], (DEPTH, N_Q_HEADS), 0.5),
        "conv_w": nrm(ks[10], (DEPTH, SSD_CONV, XBC_WIDTH), SSD_CONV ** -0.5),
        "conv_b": nrm(ks[11], (DEPTH, XBC_WIDTH), 0.01),
        "dt_bias": dt0 + jnp.log(-jnp.expm1(-dt0)),
        "a_log": jnp.log(jax.random.uniform(ks[13], (DEPTH, N_DIRS, SSD_HEADS), f32, 1.0, 16.0)),
        "d_skip": gain(ks[14], (DEPTH, SSD_HEADS)),
        "ssd_norm_g": gain(ks[15], (DEPTH, SSD_WIDTH)),
        "w_out": nrm(ks[16], (DEPTH, MIX_WIDTH, D_MODEL), MIX_WIDTH ** -0.5),
        "router_w": nrm(ks[17], (DEPTH, D_MODEL, N_EXPERTS), D_MODEL ** -0.5),
        "router_bias": nrm(ks[18], (DEPTH, N_EXPERTS), 0.01),
        "expert_w_gate": nrm(ks[19], (DEPTH, N_EXPERTS, D_MODEL, EXPERT_DIM), D_MODEL ** -0.5),
        "expert_w_up": nrm(ks[20], (DEPTH, N_EXPERTS, D_MODEL, EXPERT_DIM), D_MODEL ** -0.5),
        "expert_w_down": nrm(ks[21], (DEPTH, N_EXPERTS, EXPERT_DIM, D_MODEL), EXPERT_DIM ** -0.5),
        "shared_w_gate": nrm(ks[22], (DEPTH, D_MODEL, SHARED_DIM), D_MODEL ** -0.5),
        "shared_w_up": nrm(ks[23], (DEPTH, D_MODEL, SHARED_DIM), D_MODEL ** -0.5),
        "shared_w_down": nrm(ks[24], (DEPTH, SHARED_DIM, D_MODEL), SHARED_DIM ** -0.5),
        "final_norm_g": gain(ks[25], (D_MODEL,)),
    }


def reference(x, c, ctx, c_ctx, w_ada, b_ada, norm1_g, norm2_g, w_in, attn_sink, conv_w, conv_b,
              dt_bias, a_log, d_skip, ssd_norm_g, w_out, router_w, router_bias, expert_w_gate,
              expert_w_up, expert_w_down, shared_w_gate, shared_w_up, shared_w_down, final_norm_g):
    b, n, _ = x.shape
    n_rows = n // GRID_W
    rows = jnp.broadcast_to(jnp.arange(n_rows, dtype=jnp.int32)[:, None], (n_rows, GRID_W)).reshape(-1)
    cols = jnp.broadcast_to(jnp.arange(GRID_W, dtype=jnp.int32)[None, :], (n_rows, GRID_W)).reshape(-1)
    xc = ctx
    for l in range(DEPTH):
        update_ctx = l < DEPTH - 1
        mod = jax.nn.silu(c) @ w_ada[l] + b_ada[l]
        mod_c = jax.nn.silu(c_ctx) @ w_ada[l] + b_ada[l]
        sh1, sc1, g1, sh2, sc2, g2 = jnp.split(mod[:, None, :], 6, axis=-1)
        sh1c, sc1c, g1c, sh2c, sc2c, g2c = jnp.split(mod_c, 6, axis=-1)
        h = modulate(rmsnorm(x, norm1_g[l]), sh1, sc1)
        hc = modulate(rmsnorm(xc, norm1_g[l]), sh1c, sc1c)
        m_l, m_c = token_mixer(h, hc, rows, cols, w_in[l], attn_sink[l], conv_w[l], conv_b[l],
                               dt_bias[l], a_log[l], d_skip[l], ssd_norm_g[l], w_out[l], update_ctx)
        x = x + g1 * m_l
        h2 = modulate(rmsnorm(x, norm2_g[l]), sh2, sc2)
        ffn = moe_ffn(h2.reshape(-1, D_MODEL), router_w[l], router_bias[l], expert_w_gate[l],
                      expert_w_up[l], expert_w_down[l], shared_w_gate[l], shared_w_up[l], shared_w_down[l])
        x = x + g2 * ffn.reshape(b, n, D_MODEL)
        if update_ctx:
            xc = xc + g1c * m_c
            hc2 = modulate(rmsnorm(xc, norm2_g[l]), sh2c, sc2c)
            ffn_c = moe_ffn(hc2.reshape(-1, D_MODEL), router_w[l], router_bias[l], expert_w_gate[l],
                            expert_w_up[l], expert_w_down[l], shared_w_gate[l], shared_w_up[l],
                            shared_w_down[l])
            xc = xc + g2c * ffn_c.reshape(xc.shape)
    return rmsnorm(x, final_norm_g)
```

```python
import functools
import math

import jax
import jax.numpy as jnp
import numpy as np
from jax import lax
from jax.experimental import pallas as pl
from jax.experimental.pallas import tpu as pltpu

F32 = jnp.float32
BF16 = jnp.bfloat16
I32 = jnp.int32

EPS = 1e-6
GRID_W = 64
N_Q_HEADS = 8
N_KV_HEADS = 2
Q_PER_KV = N_Q_HEADS // N_KV_HEADS
HEAD_DIM = 128
ATTN_WIDTH = N_Q_HEADS * HEAD_DIM
KV_WIDTH = N_KV_HEADS * HEAD_DIM
WINDOW = 128
ATTN_BLOCK = 128
ROPE_THETA = 10000.0
ROPE_AXIS_DIM = HEAD_DIM // 2
SSD_HEADS = 16
SSD_HEAD_DIM = 64
SSD_WIDTH = SSD_HEADS * SSD_HEAD_DIM
SSD_GROUPS = 4
SSD_HEADS_PER_GROUP = SSD_HEADS // SSD_GROUPS
SSD_STATE = 128
SSD_CONV = 5
N_DIRS = 2
XBC_WIDTH = SSD_WIDTH + 2 * SSD_GROUPS * SSD_STATE
N_EXPERTS = 64
N_EXPERT_GROUPS = 8
EXPERTS_PER_GROUP = N_EXPERTS // N_EXPERT_GROUPS
TOPK_GROUPS = 4
TOP_K = 8
EXPERT_DIM = 512
ROUTED_SCALE = 2.5

LANES = 128
SUBLANES = 8
NEG_BIG = -1e30

COL_Q = 0
COL_Z = ATTN_WIDTH
COL_XBC = COL_Z + SSD_WIDTH
COL_K = COL_XBC + XBC_WIDTH
COL_V = COL_K + KV_WIDTH
MAIN_WIDTH = COL_V + KV_WIDTH

IN_TILE_N = 512
SSD_CHUNK = 256
MOE_TILE = 256
ROUTE_TILE = 512
TOK_TILE = 256
COMB_TILE = 128

_NT = (((1,), (1,)), ((), ()))
_TN = (((0,), (0,)), ((), ()))


def _cparams(sem, vmem_mb):
    return pltpu.CompilerParams(dimension_semantics=sem, vmem_limit_bytes=vmem_mb * 1024 * 1024)


def _dot(a, b, dims=None):
    if dims is None:
        return jnp.dot(a, b, preferred_element_type=F32)
    return lax.dot_general(a, b, dims, preferred_element_type=F32)


def _split(a):
    hi = a.astype(BF16)
    lo = (a - hi.astype(F32)).astype(BF16)
    return hi, lo


def _dot_lhs_f32(a, b_exact, dims=None):
    hi, lo = _split(a)
    return _dot(hi, b_exact, dims) + _dot(lo, b_exact, dims)


def _dot_rhs_f32(a_exact, b, dims=None):
    hi, lo = _split(b)
    return _dot(a_exact, hi, dims) + _dot(a_exact, lo, dims)


def _dot_f32(a, b, dims=None):
    ah, al = _split(a)
    bh, bl = _split(b)
    return _dot(ah, bh, dims) + (_dot(al, bh, dims) + _dot(ah, bl, dims))


def _sigmoid(x):
    return 1.0 / (1.0 + jnp.exp(-x))


def _silu(x):
    return x * _sigmoid(x)


def _rms(x):
    return x * lax.rsqrt(jnp.mean(x * x, axis=-1, keepdims=True) + EPS)


def _chunks(d):
    return d // LANES


def _load_rows(ref, rows, nch, base=None):
    src = ref if base is None else ref.at[base]
    return jnp.concatenate([src[pl.ds(j, rows, stride=nch), :] for j in range(nch)], axis=1)


def _store_rows(ref, val, nch):
    rows = val.shape[0]
    for j in range(nch):
        ref[pl.ds(j, rows, stride=nch), :] = val[:, j * LANES:(j + 1) * LANES]


def _ada_kernel(c_ref, w_ref, b_ref, o_ref):
    o_ref[...] = _dot_f32(_silu(c_ref[...]), w_ref[...]) + b_ref[...]


def _ada(cvec, w, b):
    m, d = cvec.shape
    n = w.shape[1]
    tn = 1024
    return pl.pallas_call(
        _ada_kernel,
        grid=(n // tn,),
        in_specs=[pl.BlockSpec((m, d), lambda j: (0, 0)),
                  pl.BlockSpec((d, tn), lambda j: (0, j)),
                  pl.BlockSpec((1, tn), lambda j: (0, j))],
        out_specs=pl.BlockSpec((m, tn), lambda j: (0, j)),
        out_shape=jax.ShapeDtypeStruct((m, n), F32),
        compiler_params=_cparams(("arbitrary",), 40),
        name="ada",
    )(cvec, w, b)


def _rope_mix(acc, cos, sin):
    tn = acc.shape[1]
    lane = lax.broadcasted_iota(I32, acc.shape, 1)
    first = (lane & (ROPE_AXIS_DIM - 1)) < (ROPE_AXIS_DIM // 2)
    half = ROPE_AXIS_DIM // 2
    partner = jnp.where(first, pltpu.roll(acc, tn - half, 1), pltpu.roll(acc, half, 1))
    return acc * cos + partner * sin


def _inproj_kernel(*refs, rope, n_rope_tiles, kv_tile):
    if rope:
        x_ref, g_ref, sc_ref, sh_ref, w_ref, wdt_ref, cos_ref, sin_ref, o_ref, dt_ref, h_scr = refs
    else:
        x_ref, g_ref, sc_ref, sh_ref, w_ref, wdt_ref, o_ref, dt_ref, h_scr = refs
    j = pl.program_id(1)

    @pl.when(j == 0)
    def _():
        h = (_rms(x_ref[...]) * g_ref[...]) * (1.0 + sc_ref[0]) + sh_ref[0]
        hb = h.astype(BF16)
        h_scr[...] = hb
        dt_ref[...] = _dot(hb, wdt_ref[...])

    acc = _dot(h_scr[...], w_ref[...])
    if not rope:
        o_ref[...] = acc.astype(o_ref.dtype)
        return
    reps = acc.shape[1] // HEAD_DIM

    @pl.when(j < n_rope_tiles)
    def _():
        cos = jnp.concatenate([cos_ref[...]] * reps, axis=1)
        sin = jnp.concatenate([sin_ref[...]] * reps, axis=1)
        o_ref[...] = _rope_mix(acc, cos, sin).astype(o_ref.dtype)

    @pl.when(j == kv_tile)
    def _():
        nk = KV_WIDTH // HEAD_DIM
        ones = jnp.ones_like(cos_ref[...])
        cos = jnp.concatenate([cos_ref[...]] * nk + [ones] * (reps - nk), axis=1)
        sin = jnp.concatenate([sin_ref[...]] * nk + [ones * 0.0] * (reps - nk), axis=1)
        o_ref[...] = _rope_mix(acc, cos, sin).astype(o_ref.dtype)

    @pl.when((j >= n_rope_tiles) & (j != kv_tile))
    def _():
        o_ref[...] = acc.astype(o_ref.dtype)


def _inproj(x, g, sc, sh, w, wdt, cos, sin, *, seq, tm, rope):
    t, d = x.shape
    wn = w.shape[1]
    tn = IN_TILE_N
    per = seq // tm
    in_specs = [pl.BlockSpec((tm, d), lambda i, j: (i, 0)),
                pl.BlockSpec((1, d), lambda i, j: (0, 0)),
                pl.BlockSpec((1, 1, d), lambda i, j: (i // per, 0, 0)),
                pl.BlockSpec((1, 1, d), lambda i, j: (i // per, 0, 0)),
                pl.BlockSpec((d, tn), lambda i, j: (0, j)),
                pl.BlockSpec((d, LANES), lambda i, j: (0, 0))]
    args = [x, g, sc, sh, w, wdt]
    if rope:
        in_specs += [pl.BlockSpec((tm, HEAD_DIM), lambda i, j: (i % per, 0)),
                     pl.BlockSpec((tm, HEAD_DIM), lambda i, j: (i % per, 0))]
        args += [cos, sin]
    kern = functools.partial(_inproj_kernel, rope=rope, n_rope_tiles=ATTN_WIDTH // tn,
                             kv_tile=COL_K // tn)
    return pl.pallas_call(
        kern,
        grid=(t // tm, wn // tn),
        in_specs=in_specs,
        out_specs=[pl.BlockSpec((tm, tn), lambda i, j: (i, j)),
                   pl.BlockSpec((tm, LANES), lambda i, j: (i, 0))],
        out_shape=[jax.ShapeDtypeStruct((t, wn), BF16), jax.ShapeDtypeStruct((t, LANES), F32)],
        scratch_shapes=[pltpu.VMEM((tm, d), BF16)],
        compiler_params=_cparams(("arbitrary", "arbitrary"), 40),
        name="inproj_rope" if rope else "inproj",
    )(*args)


def _conv_kernel(x_ref, w_ref, b_ref, o_ref, pad_scr, *, seq, rows):
    halo = SUBLANES
    cw = x_ref.shape[1]
    pad_scr[0:halo, :] = jnp.zeros((halo, cw), F32)
    pad_scr[seq + halo:seq + 2 * halo, :] = jnp.zeros((halo, cw), F32)
    pad_scr[halo:seq + halo, :] = x_ref[...].astype(F32)
    w = w_ref[...]
    bias = b_ref[...]
    for r0 in range(0, seq, rows):
        acc = jnp.broadcast_to(bias, (rows, cw))
        for tap in range(SSD_CONV):
            start = r0 + halo - SSD_CONV // 2 + tap
            acc = acc + w[tap:tap + 1, :] * pad_scr[start:start + rows, :]
        o_ref[r0:r0 + rows, :] = _silu(acc).astype(o_ref.dtype)


def _conv(main, col0, w8, b, *, seq):
    t = main.shape[0]
    cw = 256
    rows = min(seq, 256)
    cb0 = col0 // cw
    kern = functools.partial(_conv_kernel, seq=seq, rows=rows)
    return pl.pallas_call(
        kern,
        grid=(t // seq, XBC_WIDTH // cw),
        in_specs=[pl.BlockSpec((seq, cw), lambda b_, c: (b_, cb0 + c)),
                  pl.BlockSpec((SUBLANES, cw), lambda b_, c: (0, c)),
                  pl.BlockSpec((1, cw), lambda b_, c: (0, c))],
        out_specs=pl.BlockSpec((seq, cw), lambda b_, c: (b_, c)),
        out_shape=jax.ShapeDtypeStruct((t, XBC_WIDTH), BF16),
        scratch_shapes=[pltpu.VMEM((seq + 2 * SUBLANES, cw), F32)],
        compiler_params=_cparams(("arbitrary", "arbitrary"), 40),
        name="conv",
    )(main, w8, b)


def _attn_kernel(sink_ref, q_ref, kp_ref, kc_ref, kn_ref, vp_ref, vc_ref, vn_ref,
                 kx_ref, vx_ref, o_ref, *, nb):
    qi = pl.program_id(1)
    hk = pl.program_id(2)
    blk = ATTN_BLOCK
    g = Q_PER_KV
    q = jnp.concatenate([q_ref[:, i * HEAD_DIM:(i + 1) * HEAD_DIM] for i in range(g)], axis=0)
    k_loc = jnp.concatenate([kp_ref[...], kc_ref[...], kn_ref[...]], axis=0)
    v_loc = jnp.concatenate([vp_ref[...], vc_ref[...], vn_ref[...]], axis=0)
    scale = HEAD_DIM ** -0.5
    s_loc = _dot(q, k_loc, _NT) * scale
    s_ctx = _dot(q, kx_ref[...], _NT) * scale
    r = lax.broadcasted_iota(I32, s_loc.shape, 0) & (blk - 1)
    c = lax.broadcasted_iota(I32, s_loc.shape, 1)
    lo = jnp.maximum(r, jnp.where(qi == 0, blk, 0))
    hi = jnp.minimum(r + 2 * WINDOW, jnp.where(qi == nb - 1, 2 * blk - 1, 3 * blk - 1))
    s_loc = jnp.where((c >= lo) & (c <= hi), s_loc, NEG_BIG)
    row = lax.broadcasted_iota(I32, (g * blk, 1), 0)
    sink = jnp.zeros((g * blk, 1), F32)
    for i in range(g):
        sink = jnp.where((row >= i * blk) & (row < (i + 1) * blk), sink_ref[hk * g + i], sink)
    m = jnp.maximum(jnp.maximum(jnp.max(s_loc, axis=1, keepdims=True),
                                jnp.max(s_ctx, axis=1, keepdims=True)), sink)
    p_loc = jnp.exp(s_loc - m)
    p_ctx = jnp.exp(s_ctx - m)
    den = (jnp.sum(p_loc, axis=1, keepdims=True) + jnp.sum(p_ctx, axis=1, keepdims=True)
           + jnp.exp(sink - m))
    o = _dot(p_loc.astype(BF16), v_loc) + _dot(p_ctx.astype(BF16), vx_ref[...])
    o = o / den
    for i in range(g):
        o_ref[:, i * HEAD_DIM:(i + 1) * HEAD_DIM] = o[i * blk:(i + 1) * blk, :].astype(o_ref.dtype)


def _attn(main, main_c, sink, *, batch, seq, ctx_len):
    t = main.shape[0]
    blk = ATTN_BLOCK
    nb = seq // blk
    gw = Q_PER_KV * HEAD_DIM
    kcol = COL_K // HEAD_DIM
    vcol = COL_V // HEAD_DIM
    kcol_c = XBC_WIDTH // HEAD_DIM
    vcol_c = kcol_c + N_KV_HEADS

    def nbr(col, off):
        return pl.BlockSpec(
            (blk, HEAD_DIM),
            lambda b, i, h: (b * nb + jnp.clip(i + off, 0, nb - 1), col + h))

    in_specs = [pl.BlockSpec(memory_space=pltpu.SMEM),
                pl.BlockSpec((blk, gw), lambda b, i, h: (b * nb + i, h)),
                nbr(kcol, -1), nbr(kcol, 0), nbr(kcol, 1),
                nbr(vcol, -1), nbr(vcol, 0), nbr(vcol, 1),
                pl.BlockSpec((ctx_len, HEAD_DIM), lambda b, i, h: (b, kcol_c + h)),
                pl.BlockSpec((ctx_len, HEAD_DIM), lambda b, i, h: (b, vcol_c + h))]
    return pl.pallas_call(
        functools.partial(_attn_kernel, nb=nb),
        grid=(batch, nb, N_KV_HEADS),
        in_specs=in_specs,
        out_specs=pl.BlockSpec((blk, gw), lambda b, i, h: (b * nb + i, h)),
        out_shape=jax.ShapeDtypeStruct((t, ATTN_WIDTH), BF16),
        compiler_params=_cparams(("arbitrary", "arbitrary", "arbitrary"), 40),
        name="attn",
    )(sink, main, main, main, main, main, main, main, main_c, main_c)


def _softplus(x):
    return jnp.maximum(x, 0.0) + jnp.log1p(jnp.exp(-jnp.abs(x)))


def _ssd_chunk(u_ref, dtp_ref, bias_ref, a_ref, e_ref, tri_ref, h_scr, y_ref, d):
    q = u_ref.shape[0]
    rev = d == 1
    p = SSD_HEAD_DIM
    gw = SSD_HEADS_PER_GROUP * p
    dt = _softplus(dtp_ref[...] + bias_ref[...])
    dta = dt * a_ref[...]
    tri = tri_ref[d]
    a_cum = _dot_rhs_f32(tri, dta)
    a_row = _dot_lhs_f32(dta.T, tri, _NT)
    edge = a_cum[0:1, :] if rev else a_cum[q - 1:q, :]
    ea = jnp.exp(a_cum)
    w_state = dt * jnp.exp(edge - a_cum)
    e_mat = e_ref[d]
    dt_e = _dot_lhs_f32(dt, e_mat)
    ws_e = _dot_lhs_f32(w_state, e_mat)
    x = u_ref[:, 0:SSD_WIDTH].astype(F32)
    xdd = (x * ws_e).astype(BF16)
    li = lax.broadcasted_iota(I32, (q, q), 0)
    si = lax.broadcasted_iota(I32, (q, q), 1)
    keep = (si >= li) if rev else (si <= li)
    ea_e = _dot_lhs_f32(ea, e_mat)
    edge_e = ea_e[0:1, :] if rev else ea_e[q - 1:q, :]
    if y_ref is not None:
        xd = (x * dt_e).astype(BF16)
    for g in range(SSD_GROUPS):
        b_g = u_ref[:, SSD_WIDTH + g * SSD_STATE:SSD_WIDTH + (g + 1) * SSD_STATE]
        c0 = SSD_WIDTH + SSD_GROUPS * SSD_STATE + g * SSD_STATE
        c_g = u_ref[:, c0:c0 + SSD_STATE]
        h_t = h_scr[d, g]
        if y_ref is not None:
            cb = _dot(c_g, b_g, _NT)
            ys = []
            for r in range(SSD_HEADS_PER_GROUP):
                hd = g * SSD_HEADS_PER_GROUP + r
                ln = d * SSD_HEADS + hd
                seg = a_cum[:, ln:ln + 1] - a_row[ln:ln + 1, :]
                decay = jnp.exp(jnp.where(keep, seg, NEG_BIG))
                ys.append(_dot((cb * decay).astype(BF16), xd[:, hd * p:(hd + 1) * p]))
            y_off = _dot(c_g, h_t.astype(BF16)) * ea_e[:, g * gw:(g + 1) * gw]
            y_ref[:, g * gw:(g + 1) * gw] = jnp.concatenate(ys, axis=1) + y_off
        st = _dot(b_g, xdd[:, g * gw:(g + 1) * gw], _TN)
        h_scr[d, g] = h_t * edge_e[:, g * gw:(g + 1) * gw] + st


def _ssd_kernel(*refs, emit_y):
    if emit_y:
        (uf_ref, ub_ref, df_ref, db_ref, h0_ref, bias_ref, a_ref, e_ref, tri_ref,
         yf_ref, yb_ref, h_scr) = refs
    else:
        (uf_ref, ub_ref, df_ref, db_ref, bias_ref, a_ref, e_ref, tri_ref, hout_ref, h_scr) = refs
        yf_ref = yb_ref = None
    c = pl.program_id(1)

    @pl.when(c == 0)
    def _():
        if emit_y:
            h_scr[...] = h0_ref[0]
        else:
            h_scr[...] = jnp.zeros(h_scr.shape, F32)

    _ssd_chunk(uf_ref, df_ref, bias_ref, a_ref, e_ref, tri_ref, h_scr, yf_ref, 0)
    _ssd_chunk(ub_ref, db_ref, bias_ref, a_ref, e_ref, tri_ref, h_scr, yb_ref, 1)
    if not emit_y:
        @pl.when(c == pl.num_programs(1) - 1)
        def _():
            hout_ref[0] = h_scr[...]


def _ssd(u, dtp, h0, bias, a_neg, e_mat, tri, *, batch, seq, q):
    t = u.shape[0]
    nc = seq // q
    emit_y = h0 is not None
    hshape = (N_DIRS, SSD_GROUPS, SSD_STATE, SSD_HEADS_PER_GROUP * SSD_HEAD_DIM)
    fwd = lambda b, c: (b * nc + c, 0)
    bwd = lambda b, c: (b * nc + nc - 1 - c, 0)
    const2 = lambda b, c: (0, 0)
    const3 = lambda b, c: (0, 0, 0)
    in_specs = [pl.BlockSpec((q, XBC_WIDTH), fwd), pl.BlockSpec((q, XBC_WIDTH), bwd),
                pl.BlockSpec((q, LANES), fwd), pl.BlockSpec((q, LANES), bwd)]
    args = [u, u, dtp, dtp]
    if emit_y:
        in_specs.append(pl.BlockSpec((1,) + hshape, lambda b, c: (b, 0, 0, 0, 0)))
        args.append(h0)
    in_specs += [pl.BlockSpec((1, LANES), const2), pl.BlockSpec((1, LANES), const2),
                 pl.BlockSpec((N_DIRS, LANES, SSD_WIDTH), const3),
                 pl.BlockSpec((N_DIRS, q, q), const3)]
    args += [bias, a_neg, e_mat, tri]
    if emit_y:
        out_specs = [pl.BlockSpec((q, SSD_WIDTH), fwd), pl.BlockSpec((q, SSD_WIDTH), bwd)]
        out_shape = [jax.ShapeDtypeStruct((t, SSD_WIDTH), F32)] * 2
    else:
        out_specs = pl.BlockSpec((1,) + hshape, lambda b, c: (b, 0, 0, 0, 0))
        out_shape = jax.ShapeDtypeStruct((batch,) + hshape, F32)
    return pl.pallas_call(
        functools.partial(_ssd_kernel, emit_y=emit_y),
        grid=(batch, nc),
        in_specs=in_specs,
        out_specs=out_specs,
        out_shape=out_shape,
        scratch_shapes=[pltpu.VMEM(hshape, F32)],
        compiler_params=_cparams(("arbitrary", "arbitrary"), 48),
        name="ssd_lat" if emit_y else "ssd_ctx",
    )(*args)


def _outproj_kernel(attn_ref, yf_ref, yb_ref, xs_ref, z_ref, x_ref, w_ref, skip_ref, ng_ref,
                    g1_ref, n2_ref, sc2_ref, sh2_ref, rw_ref, x1_ref, h2_ref, lg_ref):
    y = yf_ref[...] + yb_ref[...] + skip_ref[...] * xs_ref[...].astype(F32)
    y = y * _silu(z_ref[...].astype(F32))
    y = (_rms(y) * ng_ref[...]).astype(BF16)
    m = _dot(attn_ref[...], w_ref[0:ATTN_WIDTH, :]) + _dot(y, w_ref[ATTN_WIDTH:, :])
    x1 = x_ref[...] + g1_ref[0] * m
    x1_ref[...] = x1
    h2 = (_rms(x1) * n2_ref[...]) * (1.0 + sc2_ref[0]) + sh2_ref[0]
    _store_rows(h2_ref, h2, _chunks(h2.shape[1]))
    lg_ref[...] = _dot_f32(rw_ref[...], h2, _NT)


def _outproj(attn, yf, yb, u, main, x, w_out, skip_e, ssd_g, g1, n2, sc2, sh2, rw_t, *, seq):
    t, d = x.shape
    tm = TOK_TILE
    per = seq // tm
    row = lambda i: (i, 0)
    c2 = lambda i: (0, 0)
    bat = lambda i: (i // per, 0, 0)
    in_specs = [pl.BlockSpec((tm, ATTN_WIDTH), row),
                pl.BlockSpec((tm, SSD_WIDTH), row),
                pl.BlockSpec((tm, SSD_WIDTH), row),
                pl.BlockSpec((tm, SSD_WIDTH), row),
                pl.BlockSpec((tm, SSD_WIDTH), lambda i: (i, COL_Z // SSD_WIDTH)),
                pl.BlockSpec((tm, d), row),
                pl.BlockSpec((ATTN_WIDTH + SSD_WIDTH, d), c2),
                pl.BlockSpec((1, SSD_WIDTH), c2),
                pl.BlockSpec((1, SSD_WIDTH), c2),
                pl.BlockSpec((1, 1, d), bat),
                pl.BlockSpec((1, d), c2),
                pl.BlockSpec((1, 1, d), bat),
                pl.BlockSpec((1, 1, d), bat),
                pl.BlockSpec((N_EXPERTS, d), c2)]
    return pl.pallas_call(
        _outproj_kernel,
        grid=(t // tm,),
        in_specs=in_specs,
        out_specs=[pl.BlockSpec((tm, d), row), pl.BlockSpec((tm * _chunks(d), LANES), row),
                   pl.BlockSpec((N_EXPERTS, tm), lambda i: (0, i))],
        out_shape=[jax.ShapeDtypeStruct((t, d), F32),
                   jax.ShapeDtypeStruct((t * _chunks(d), LANES), F32),
                   jax.ShapeDtypeStruct((N_EXPERTS, t), F32)],
        compiler_params=_cparams(("arbitrary",), 52),
        name="outproj",
    )(attn, yf, yb, u, main, x, w_out, skip_e, ssd_g, g1, n2, sc2, sh2, rw_t)


def _first_index(hit, iota, size):
    return jnp.min(jnp.where(hit, iota, size), axis=0, keepdims=True)


def _route_kernel(lg_ref, bias_ref, su_ref, idx_ref, w_ref, rank_ref, cnt_ref, carry_scr):
    i = pl.program_id(0)
    tm = lg_ref.shape[1]

    @pl.when(i == 0)
    def _():
        carry_scr[...] = jnp.zeros(carry_scr.shape, F32)

    scores = _sigmoid(lg_ref[...])
    sel = scores + bias_ref[...][:, 0:1]
    e8 = EXPERTS_PER_GROUP
    i8 = lax.broadcasted_iota(I32, (e8, tm), 0).astype(F32)
    gs = []
    for g in range(N_EXPERT_GROUPS):
        blk = sel[g * e8:(g + 1) * e8, :]
        m1 = jnp.max(blk, axis=0, keepdims=True)
        f1 = _first_index(blk == m1, i8, e8)
        m2 = jnp.max(jnp.where(i8 == f1, NEG_BIG, blk), axis=0, keepdims=True)
        gs.append(m1 + m2)
    gsc = jnp.concatenate(gs, axis=0)
    ig = lax.broadcasted_iota(I32, gsc.shape, 0).astype(F32)
    keep = jnp.zeros(gsc.shape, F32)
    for _ in range(TOPK_GROUPS):
        m = jnp.max(gsc, axis=0, keepdims=True)
        f = _first_index(gsc == m, ig, N_EXPERT_GROUPS)
        keep = jnp.where(ig == f, 1.0, keep)
        gsc = jnp.where(ig == f, NEG_BIG, gsc)
    keep_e = jnp.concatenate(
        [jnp.broadcast_to(keep[g:g + 1, :], (e8, tm)) for g in range(N_EXPERT_GROUPS)], axis=0)
    cand = jnp.where(keep_e > 0.5, sel, NEG_BIG)
    ie = lax.broadcasted_iota(I32, cand.shape, 0).astype(F32)
    chosen = jnp.zeros(cand.shape, F32)
    idxs, svals, hits = [], [], []
    for _ in range(TOP_K):
        m = jnp.max(cand, axis=0, keepdims=True)
        f = _first_index(cand == m, ie, N_EXPERTS)
        hit = ie == f
        idxs.append(f)
        svals.append(jnp.sum(jnp.where(hit, scores, 0.0), axis=0, keepdims=True))
        hits.append(hit)
        chosen = jnp.where(hit, 1.0, chosen)
        cand = jnp.where(hit, NEG_BIG, cand)
    s_all = jnp.concatenate(svals, axis=0)
    idx_ref[...] = jnp.concatenate(idxs, axis=0).astype(I32)
    w_ref[...] = ROUTED_SCALE * s_all / jnp.sum(s_all, axis=0, keepdims=True)
    before = _dot(chosen.astype(BF16), su_ref[...]) + carry_scr[...][:, 0:1]
    ranks = [jnp.sum(jnp.where(h, before, 0.0), axis=0, keepdims=True) for h in hits]
    rank_ref[...] = jnp.concatenate(ranks, axis=0).astype(I32)
    carry_scr[...] = carry_scr[...] + jnp.sum(chosen, axis=1, keepdims=True)
    cnt_ref[...] = carry_scr[...].astype(I32)


def _route(logits_t, bias_col):
    e, t = logits_t.shape
    tm = ROUTE_TILE
    su = jnp.triu(jnp.ones((tm, tm), F32), 1).astype(BF16)
    col = lambda i: (0, i)
    return pl.pallas_call(
        _route_kernel,
        grid=(t // tm,),
        in_specs=[pl.BlockSpec((e, tm), col),
                  pl.BlockSpec((e, LANES), lambda i: (0, 0)),
                  pl.BlockSpec((tm, tm), lambda i: (0, 0))],
        out_specs=[pl.BlockSpec((TOP_K, tm), col), pl.BlockSpec((TOP_K, tm), col),
                   pl.BlockSpec((TOP_K, tm), col), pl.BlockSpec((e, LANES), lambda i: (0, 0))],
        out_shape=[jax.ShapeDtypeStruct((TOP_K, t), I32), jax.ShapeDtypeStruct((TOP_K, t), F32),
                   jax.ShapeDtypeStruct((TOP_K, t), I32), jax.ShapeDtypeStruct((e, LANES), I32)],
        scratch_shapes=[pltpu.VMEM((e, LANES), F32)],
        compiler_params=_cparams(("arbitrary",), 40),
        name="route",
    )(logits_t, bias_col, su)


def _row_copy(src, dst, sem):
    return pltpu.make_async_copy(src, dst, sem)


def _dispatch_kernel(pend_ref, dest_ref, h_ref, xs_ref, zero_scr, sem, *, nch):
    i = pl.program_id(0)
    tm = h_ref.shape[0] // nch
    tile_rows = zero_scr.shape[0]

    @pl.when(i == 0)
    def _():
        zero_scr[...] = jnp.zeros(zero_scr.shape, zero_scr.dtype)

        def pad_copy(e):
            start_row = pl.multiple_of(pend_ref[e + 1] * nch - tile_rows, tile_rows)
            return _row_copy(zero_scr, xs_ref.at[pl.ds(start_row, tile_rows), :], sem.at[0])

        def start(e, carry):
            @pl.when(pend_ref[e + 1] > pend_ref[e])
            def _():
                pad_copy(e).start()
            return carry

        def wait(e, carry):
            @pl.when(pend_ref[e + 1] > pend_ref[e])
            def _():
                pad_copy(e).wait()
            return carry

        lax.fori_loop(0, N_EXPERTS, start, 0)
        lax.fori_loop(0, N_EXPERTS, wait, 0)

    def tok_copy(t, k):
        src = pl.multiple_of(t * nch, nch)
        dst = pl.multiple_of(dest_ref[0, k, t] * nch, nch)
        return _row_copy(h_ref.at[pl.ds(src, nch), :], xs_ref.at[pl.ds(dst, nch), :], sem.at[1])

    def start(t, carry):
        for k in range(TOP_K):
            tok_copy(t, k).start()
        return carry

    def wait(t, carry):
        for k in range(TOP_K):
            tok_copy(t, k).wait()
        return carry

    lax.fori_loop(0, tm, start, 0)
    lax.fori_loop(0, tm, wait, 0)


def _dispatch(pad_end_ext, dest3, h2, n_slots, nch):
    t = h2.shape[0] // nch
    tm = TOK_TILE
    grid_spec = pltpu.PrefetchScalarGridSpec(
        num_scalar_prefetch=1,
        grid=(t // tm,),
        in_specs=[pl.BlockSpec((1, TOP_K, tm), lambda i, pe: (i, 0, 0), memory_space=pltpu.SMEM),
                  pl.BlockSpec((tm * nch, LANES), lambda i, pe: (i, 0))],
        out_specs=pl.BlockSpec(memory_space=pl.ANY),
        scratch_shapes=[pltpu.VMEM((MOE_TILE * nch, LANES), F32), pltpu.SemaphoreType.DMA((2,))],
    )
    return pl.pallas_call(
        functools.partial(_dispatch_kernel, nch=nch),
        grid_spec=grid_spec,
        out_shape=jax.ShapeDtypeStruct((n_slots * nch, LANES), F32),
        compiler_params=_cparams(("arbitrary",), 40),
        name="dispatch",
    )(pad_end_ext, dest3, h2)


def _experts_kernel(texp_ref, nuse_ref, x_ref, wg_ref, wu_ref, wd_ref, y_ref,
                    wg_scr, wu_scr, wd_scr):
    i = pl.program_id(0)
    prev = texp_ref[jnp.maximum(i - 1, 0)]
    fresh = (i == 0) | (texp_ref[i] != prev)

    @pl.when(i < nuse_ref[0])
    def _():
        @pl.when(fresh)
        def _():
            wg_scr[...] = wg_ref[0].astype(BF16)
            wu_scr[...] = wu_ref[0].astype(BF16)
            wd_scr[...] = wd_ref[0].astype(BF16)

        nch = _chunks(wg_scr.shape[0])
        xb = _load_rows(x_ref, x_ref.shape[0] // nch, nch).astype(BF16)
        act = _silu(_dot(xb, wg_scr[...])) * _dot(xb, wu_scr[...])
        _store_rows(y_ref, _dot(act.astype(BF16), wd_scr[...]), nch)


def _experts(tile_exp, n_used, xs, wg, wu, wd):
    d, f = wg.shape[1], wg.shape[2]
    nch = _chunks(d)
    n_slots = xs.shape[0] // nch
    te = MOE_TILE
    nt = n_slots // te

    def live(i, texp, nuse):
        return jnp.minimum(i, nuse[0] - 1)

    grid_spec = pltpu.PrefetchScalarGridSpec(
        num_scalar_prefetch=2,
        grid=(nt,),
        in_specs=[pl.BlockSpec((te * nch, LANES), lambda i, texp, nuse: (live(i, texp, nuse), 0)),
                  pl.BlockSpec((1, d, f), lambda i, texp, nuse: (texp[live(i, texp, nuse)], 0, 0)),
                  pl.BlockSpec((1, d, f), lambda i, texp, nuse: (texp[live(i, texp, nuse)], 0, 0)),
                  pl.BlockSpec((1, f, d), lambda i, texp, nuse: (texp[live(i, texp, nuse)], 0, 0))],
        out_specs=pl.BlockSpec((te * nch, LANES), lambda i, texp, nuse: (live(i, texp, nuse), 0)),
        scratch_shapes=[pltpu.VMEM((d, f), BF16), pltpu.VMEM((d, f), BF16), pltpu.VMEM((f, d), BF16)],
    )
    return pl.pallas_call(
        _experts_kernel,
        grid_spec=grid_spec,
        out_shape=jax.ShapeDtypeStruct((n_slots * nch, LANES), F32),
        compiler_params=_cparams(("arbitrary",), 56),
        name="experts",
    )(tile_exp, n_used, xs, wg, wu, wd)


def _combine_kernel(dest_ref, ys_ref, wt_ref, h_ref, x1_ref, sg_ref, su_ref, sd_ref, g2_ref,
                    fg_ref, eye_ref, o_ref, buf, sem):
    tm = x1_ref.shape[0]
    nch = _chunks(x1_ref.shape[1])

    def row_copy(t, k):
        src = pl.multiple_of(dest_ref[0, k, t] * nch, nch)
        dst = pl.multiple_of(t * nch, nch)
        return _row_copy(ys_ref.at[pl.ds(src, nch), :], buf.at[k, pl.ds(dst, nch), :], sem.at[0])

    def start(t, carry):
        for k in range(TOP_K):
            row_copy(t, k).start()
        return carry

    def wait(t, carry):
        for k in range(TOP_K):
            row_copy(t, k).wait()
        return carry

    lax.fori_loop(0, tm, start, 0)
    hb = _load_rows(h_ref, tm, nch).astype(BF16)
    act = _silu(_dot(hb, sg_ref[...])) * _dot(hb, su_ref[...])
    ffn = _dot(act.astype(BF16), sd_ref[...])
    w_col = _dot_rhs_f32(eye_ref[...], wt_ref[...], _NT)
    lax.fori_loop(0, tm, wait, 0)
    for k in range(TOP_K):
        ffn = ffn + w_col[:, k:k + 1] * _load_rows(buf, tm, nch, base=k)
    x2 = x1_ref[...] + g2_ref[0] * ffn
    o_ref[...] = _rms(x2) * fg_ref[...]


def _combine(dest3, ys, w_t, h2, x1, sg, su, sd, g2, fg, *, seq):
    t, d = x1.shape
    nch = _chunks(d)
    tm = COMB_TILE
    per = seq // tm
    f = sg.shape[1]
    eye = jnp.eye(tm, dtype=BF16)
    row = lambda i: (i, 0)
    c2 = lambda i: (0, 0)
    return pl.pallas_call(
        _combine_kernel,
        grid=(t // tm,),
        in_specs=[pl.BlockSpec((1, TOP_K, tm), lambda i: (i, 0, 0), memory_space=pltpu.SMEM),
                  pl.BlockSpec(memory_space=pl.ANY),
                  pl.BlockSpec((TOP_K, tm), lambda i: (0, i)),
                  pl.BlockSpec((tm * nch, LANES), row),
                  pl.BlockSpec((tm, d), row),
                  pl.BlockSpec((d, f), c2), pl.BlockSpec((d, f), c2), pl.BlockSpec((f, d), c2),
                  pl.BlockSpec((1, 1, d), lambda i: (i // per, 0, 0)),
                  pl.BlockSpec((1, d), c2),
                  pl.BlockSpec((tm, tm), c2)],
        out_specs=pl.BlockSpec((tm, d), row),
        out_shape=jax.ShapeDtypeStruct((t, d), F32),
        scratch_shapes=[pltpu.VMEM((TOP_K, tm * nch, LANES), F32), pltpu.SemaphoreType.DMA((1,))],
        compiler_params=_cparams(("arbitrary",), 48),
        name="combine",
    )(dest3, ys, w_t, h2, x1, sg, su, sd, g2, fg, eye)


def _rope_tables(seq):
    n_rows = seq // GRID_W
    rows = jnp.repeat(jnp.arange(n_rows, dtype=F32), GRID_W)
    cols = jnp.tile(jnp.arange(GRID_W, dtype=F32), n_rows)
    inv = ROPE_THETA ** (-jnp.arange(0, ROPE_AXIS_DIM, 2, dtype=F32) / ROPE_AXIS_DIM)
    ar = rows[:, None] * inv[None, :]
    ac = cols[:, None] * inv[None, :]
    cos = jnp.concatenate([jnp.cos(ar), jnp.cos(ar), jnp.cos(ac), jnp.cos(ac)], axis=1)
    sin = jnp.concatenate([-jnp.sin(ar), jnp.sin(ar), -jnp.sin(ac), jnp.sin(ac)], axis=1)
    return cos, sin


def _layer(x, c, ctx, c_ctx, w_ada, b_ada, norm1_g, norm2_g, w_in, attn_sink, conv_w, conv_b,
           dt_bias, a_log, d_skip, ssd_norm_g, w_out, router_w, router_bias, wg, wu, wd,
           sw_gate, sw_up, sw_down, final_norm_g):
    batch, seq, d = x.shape
    ctx_len = ctx.shape[1]
    t = batch * seq

    cvec = jnp.zeros((SUBLANES, d), F32).at[:batch].set(c).at[batch].set(c_ctx)
    mod = _ada(cvec, w_ada, b_ada[None, :])
    sh1, sc1, g1, sh2, sc2, g2 = [mod[:batch, i * d:(i + 1) * d][:, None, :] for i in range(6)]
    sh1c, sc1c = [jnp.broadcast_to(mod[batch, i * d:(i + 1) * d][None, None, :], (batch, 1, d))
                  for i in range(2)]

    o_q, o_k, o_v, o_z, o_x, o_dt = np.cumsum((0, ATTN_WIDTH, KV_WIDTH, KV_WIDTH, SSD_WIDTH,
                                               XBC_WIDTH)).tolist()
    w_main = jnp.concatenate([w_in[:, o_q:o_k], w_in[:, o_z:o_x], w_in[:, o_x:o_dt],
                              w_in[:, o_k:o_v], w_in[:, o_v:o_z]], axis=1).astype(BF16)
    w_dt = jnp.pad(w_in[:, o_dt:], ((0, 0), (0, LANES - N_DIRS * SSD_HEADS))).astype(BF16)
    cos, sin = _rope_tables(seq)
    n1 = norm1_g[None, :]
    main, dtp = _inproj(x.reshape(t, d), n1, sc1, sh1, w_main, w_dt, cos, sin,
                        seq=seq, tm=512, rope=True)
    main_c, dtp_c = _inproj(ctx.reshape(batch * ctx_len, d), n1, sc1c, sh1c,
                            w_main[:, COL_XBC:], w_dt, None, None,
                            seq=ctx_len, tm=ctx_len, rope=False)

    attn = _attn(main, main_c, attn_sink, batch=batch, seq=seq, ctx_len=ctx_len)

    cw8 = jnp.pad(conv_w, ((0, SUBLANES - SSD_CONV), (0, 0)))
    u = _conv(main, COL_XBC, cw8, conv_b[None, :], seq=seq)
    u_c = _conv(main_c, 0, cw8, conv_b[None, :], seq=ctx_len)
    nd = N_DIRS * SSD_HEADS
    bias = jnp.pad(dt_bias.reshape(1, nd), ((0, 0), (0, LANES - nd)))
    a_neg = jnp.pad(-jnp.exp(a_log.reshape(1, nd)), ((0, 0), (0, LANES - nd)))
    lane_head = np.arange(SSD_WIDTH) // SSD_HEAD_DIM
    e_np = np.zeros((N_DIRS, LANES, SSD_WIDTH), np.float32)
    for dd in range(N_DIRS):
        e_np[dd, dd * SSD_HEADS + lane_head, np.arange(SSD_WIDTH)] = 1.0
    e_mat = jnp.asarray(e_np, BF16)

    def tri(q):
        lo = np.tril(np.ones((q, q), np.float32))
        return jnp.asarray(np.stack([lo, lo.T]), BF16)

    qc = min(SSD_CHUNK, ctx_len)
    h_ctx = _ssd(u_c, dtp_c, None, bias, a_neg, e_mat, tri(qc), batch=batch, seq=ctx_len, q=qc)
    ql = min(SSD_CHUNK, seq)
    yf, yb = _ssd(u, dtp, h_ctx, bias, a_neg, e_mat, tri(ql), batch=batch, seq=seq, q=ql)

    skip_e = jnp.repeat(d_skip, SSD_HEAD_DIM)[None, :]
    x1, h2, logits_t = _outproj(attn, yf, yb, u, main, x.reshape(t, d), w_out.astype(BF16),
                                skip_e, ssd_norm_g[None, :], g1, norm2_g[None, :], sc2, sh2,
                                router_w.T, seq=seq)

    bias_col = jnp.broadcast_to(router_bias[:, None], (N_EXPERTS, LANES))
    idx_t, w_t, rank_t, cnt = _route(logits_t, bias_col)
    counts = cnt[:, 0]
    te = MOE_TILE
    padded = (counts + te - 1) // te * te
    pad_end = jnp.cumsum(padded)
    pad_start = pad_end - padded
    n_slots = -(-(t * TOP_K + N_EXPERTS * (te - 1)) // te) * te
    n_tiles = n_slots // te
    dest = pad_start[idx_t] + rank_t
    dest3 = dest.reshape(TOP_K, t // TOK_TILE, TOK_TILE).transpose(1, 0, 2)
    dest3c = dest.reshape(TOP_K, t // COMB_TILE, COMB_TILE).transpose(1, 0, 2)
    tile_start = jnp.arange(n_tiles, dtype=I32) * te
    tile_exp = jnp.minimum(jnp.sum(tile_start[:, None] >= pad_end[None, :], axis=1),
                           N_EXPERTS - 1).astype(I32)
    n_used = (pad_end[-1:] // te).astype(I32)
    pad_end_ext = jnp.concatenate([jnp.zeros((1,), I32), pad_end.astype(I32)])

    xs = _dispatch(pad_end_ext, dest3, h2, n_slots, _chunks(d))
    ys = _experts(tile_exp, n_used, xs, wg, wu, wd)
    out = _combine(dest3c, ys, w_t, h2, x1, sw_gate.astype(BF16), sw_up.astype(BF16),
                   sw_down.astype(BF16), g2, final_norm_g[None, :], seq=seq)
    return out.reshape(batch, seq, d)


def kernel(x, c, ctx, c_ctx, w_ada, b_ada, norm1_g, norm2_g, w_in, attn_sink, conv_w, conv_b, dt_bias, a_log, d_skip, ssd_norm_g, w_out, router_w, router_bias, expert_w_gate, expert_w_up, expert_w_down, shared_w_gate, shared_w_up, shared_w_down, final_norm_g):
    assert w_ada.shape[0] == 1, "single-layer stack only"
    return _layer(x, c, ctx, c_ctx, w_ada[0], b_ada[0], norm1_g[0], norm2_g[0], w_in[0],
                  attn_sink[0], conv_w[0], conv_b[0], dt_bias[0], a_log[0], d_skip[0],
                  ssd_norm_g[0], w_out[0], router_w[0], router_bias[0], expert_w_gate[0],
                  expert_w_up[0], expert_w_down[0], shared_w_gate[0], shared_w_up[0],
                  shared_w_down[0], final_norm_g)
```

```python
import functools
import math

import jax
import jax.numpy as jnp
import numpy as np
from jax import lax
from jax.experimental import pallas as pl
from jax.experimental.pallas import tpu as pltpu

F32 = jnp.float32
BF16 = jnp.bfloat16
I32 = jnp.int32

EPS = 1e-6
GRID_W = 64
N_Q_HEADS = 8
N_KV_HEADS = 2
Q_PER_KV = N_Q_HEADS // N_KV_HEADS
HEAD_DIM = 128
ATTN_WIDTH = N_Q_HEADS * HEAD_DIM
KV_WIDTH = N_KV_HEADS * HEAD_DIM
WINDOW = 128
ATTN_BLOCK = 128
ROPE_THETA = 10000.0
ROPE_AXIS_DIM = HEAD_DIM // 2
SSD_HEADS = 16
SSD_HEAD_DIM = 64
SSD_WIDTH = SSD_HEADS * SSD_HEAD_DIM
SSD_GROUPS = 4
SSD_HEADS_PER_GROUP = SSD_HEADS // SSD_GROUPS
SSD_STATE = 128
SSD_CONV = 5
N_DIRS = 2
XBC_WIDTH = SSD_WIDTH + 2 * SSD_GROUPS * SSD_STATE
N_EXPERTS = 64
N_EXPERT_GROUPS = 8
EXPERTS_PER_GROUP = N_EXPERTS // N_EXPERT_GROUPS
TOPK_GROUPS = 4
TOP_K = 8
EXPERT_DIM = 512
ROUTED_SCALE = 2.5

LANES = 128
SUBLANES = 8
NEG_BIG = -1e30

COL_Q = 0
COL_Z = ATTN_WIDTH
COL_XBC = COL_Z + SSD_WIDTH
COL_K = COL_XBC + XBC_WIDTH
COL_V = COL_K + KV_WIDTH
MAIN_WIDTH = COL_V + KV_WIDTH

IN_TILE_N = 512
SSD_CHUNK = 256
MOE_TILE = 256
ROUTE_TILE = 512
TOK_TILE = 256
COMB_TILE = 128

_NT = (((1,), (1,)), ((), ()))
_TN = (((0,), (0,)), ((), ()))


def _cparams(sem, vmem_mb):
    return pltpu.CompilerParams(dimension_semantics=sem, vmem_limit_bytes=vmem_mb * 1024 * 1024)


def _dot(a, b, dims=None):
    if dims is None:
        return jnp.dot(a, b, preferred_element_type=F32)
    return lax.dot_general(a, b, dims, preferred_element_type=F32)


def _split(a):
    hi = a.astype(BF16)
    lo = (a - hi.astype(F32)).astype(BF16)
    return hi, lo


def _dot_lhs_f32(a, b_exact, dims=None):
    hi, lo = _split(a)
    return _dot(hi, b_exact, dims) + _dot(lo, b_exact, dims)


def _dot_rhs_f32(a_exact, b, dims=None):
    hi, lo = _split(b)
    return _dot(a_exact, hi, dims) + _dot(a_exact, lo, dims)


def _dot_f32(a, b, dims=None):
    ah, al = _split(a)
    bh, bl = _split(b)
    return _dot(ah, bh, dims) + (_dot(al, bh, dims) + _dot(ah, bl, dims))


def _sigmoid(x):
    return 1.0 / (1.0 + jnp.exp(-x))


def _silu(x):
    return x * _sigmoid(x)


def _rms(x):
    return x * lax.rsqrt(jnp.mean(x * x, axis=-1, keepdims=True) + EPS)


U32 = jnp.uint32
_HI_MASK = 0xFFFF0000


def _chunks(d):
    return d // (2 * LANES)


def _load_rows(ref, rows, nch):
    words = [ref[pl.ds(j, rows, stride=nch), :] for j in range(nch)]
    lo = [lax.bitcast_convert_type(w << 16, F32) for w in words]
    hi = [lax.bitcast_convert_type(w & jnp.uint32(_HI_MASK), F32) for w in words]
    return jnp.concatenate(lo + hi, axis=1)


def _store_rows(ref, val, nch):
    rows, d = val.shape
    bits = lax.bitcast_convert_type(val.astype(BF16).astype(F32), U32)
    for j in range(nch):
        lo = bits[:, j * LANES:(j + 1) * LANES] >> 16
        hi = bits[:, d // 2 + j * LANES:d // 2 + (j + 1) * LANES] & jnp.uint32(_HI_MASK)
        ref[pl.ds(j, rows, stride=nch), :] = lo | hi


def _ada_kernel(c_ref, w_ref, b_ref, o_ref):
    o_ref[...] = _dot_f32(_silu(c_ref[...]), w_ref[...]) + b_ref[...]


def _ada(cvec, w, b):
    m, d = cvec.shape
    n = w.shape[1]
    tn = 1024
    return pl.pallas_call(
        _ada_kernel,
        grid=(n // tn,),
        in_specs=[pl.BlockSpec((m, d), lambda j: (0, 0)),
                  pl.BlockSpec((d, tn), lambda j: (0, j)),
                  pl.BlockSpec((1, tn), lambda j: (0, j))],
        out_specs=pl.BlockSpec((m, tn), lambda j: (0, j)),
        out_shape=jax.ShapeDtypeStruct((m, n), F32),
        compiler_params=_cparams(("arbitrary",), 40),
        name="ada",
    )(cvec, w, b)


def _rope_mix(acc, cos, sin):
    tn = acc.shape[1]
    lane = lax.broadcasted_iota(I32, acc.shape, 1)
    first = (lane & (ROPE_AXIS_DIM - 1)) < (ROPE_AXIS_DIM // 2)
    half = ROPE_AXIS_DIM // 2
    partner = jnp.where(first, pltpu.roll(acc, tn - half, 1), pltpu.roll(acc, half, 1))
    return acc * cos + partner * sin


def _inproj_kernel(*refs, rope, n_rope_tiles, kv_tile):
    if rope:
        x_ref, g_ref, sc_ref, sh_ref, w_ref, wdt_ref, cos_ref, sin_ref, o_ref, dt_ref, h_scr = refs
    else:
        x_ref, g_ref, sc_ref, sh_ref, w_ref, wdt_ref, o_ref, dt_ref, h_scr = refs
    j = pl.program_id(1)

    @pl.when(j == 0)
    def _():
        h = (_rms(x_ref[...]) * g_ref[...]) * (1.0 + sc_ref[0]) + sh_ref[0]
        hb = h.astype(BF16)
        h_scr[...] = hb
        dt_ref[...] = _dot(hb, wdt_ref[...])

    acc = _dot(h_scr[...], w_ref[0])
    if not rope:
        o_ref[...] = acc.astype(o_ref.dtype)
        return
    reps = acc.shape[1] // HEAD_DIM

    @pl.when(j < n_rope_tiles)
    def _():
        cos = jnp.concatenate([cos_ref[...]] * reps, axis=1)
        sin = jnp.concatenate([sin_ref[...]] * reps, axis=1)
        o_ref[...] = _rope_mix(acc, cos, sin).astype(o_ref.dtype)

    @pl.when(j == kv_tile)
    def _():
        nk = KV_WIDTH // HEAD_DIM
        ones = jnp.ones_like(cos_ref[...])
        cos = jnp.concatenate([cos_ref[...]] * nk + [ones] * (reps - nk), axis=1)
        sin = jnp.concatenate([sin_ref[...]] * nk + [ones * 0.0] * (reps - nk), axis=1)
        o_ref[...] = _rope_mix(acc, cos, sin).astype(o_ref.dtype)

    @pl.when((j >= n_rope_tiles) & (j != kv_tile))
    def _():
        o_ref[...] = acc.astype(o_ref.dtype)


def _inproj(x, g, sc, sh, w, wdt, cos, sin, *, seq, tm, rope):
    t, d = x.shape
    wn = w.shape[1]
    tn = IN_TILE_N
    per = seq // tm
    w = w.reshape(d, wn // tn, tn).transpose(1, 0, 2)
    in_specs = [pl.BlockSpec((tm, d), lambda i, j: (i, 0)),
                pl.BlockSpec((1, d), lambda i, j: (0, 0)),
                pl.BlockSpec((1, 1, d), lambda i, j: (i // per, 0, 0)),
                pl.BlockSpec((1, 1, d), lambda i, j: (i // per, 0, 0)),
                pl.BlockSpec((1, d, tn), lambda i, j: (j, 0, 0)),
                pl.BlockSpec((d, LANES), lambda i, j: (0, 0))]
    args = [x, g, sc, sh, w, wdt]
    if rope:
        in_specs += [pl.BlockSpec((tm, HEAD_DIM), lambda i, j: (i % per, 0)),
                     pl.BlockSpec((tm, HEAD_DIM), lambda i, j: (i % per, 0))]
        args += [cos, sin]
    kern = functools.partial(_inproj_kernel, rope=rope, n_rope_tiles=ATTN_WIDTH // tn,
                             kv_tile=COL_K // tn)
    return pl.pallas_call(
        kern,
        grid=(t // tm, wn // tn),
        in_specs=in_specs,
        out_specs=[pl.BlockSpec((tm, tn), lambda i, j: (i, j)),
                   pl.BlockSpec((tm, LANES), lambda i, j: (i, 0))],
        out_shape=[jax.ShapeDtypeStruct((t, wn), BF16), jax.ShapeDtypeStruct((t, LANES), F32)],
        scratch_shapes=[pltpu.VMEM((tm, d), BF16)],
        compiler_params=_cparams(("arbitrary", "arbitrary"), 48),
        name="inproj_rope" if rope else "inproj",
    )(*args)


def _conv_kernel(x_ref, w_ref, b_ref, o_ref, pad_scr, *, seq, rows):
    halo = SUBLANES
    cw = x_ref.shape[1]
    pad_scr[0:halo, :] = jnp.zeros((halo, cw), F32)
    pad_scr[seq + halo:seq + 2 * halo, :] = jnp.zeros((halo, cw), F32)
    pad_scr[halo:seq + halo, :] = x_ref[...].astype(F32)
    w = w_ref[...]
    bias = b_ref[...]
    for r0 in range(0, seq, rows):
        acc = jnp.broadcast_to(bias, (rows, cw))
        for tap in range(SSD_CONV):
            start = r0 + halo - SSD_CONV // 2 + tap
            acc = acc + w[tap:tap + 1, :] * pad_scr[start:start + rows, :]
        o_ref[r0:r0 + rows, :] = _silu(acc).astype(o_ref.dtype)


def _conv(main, col0, w8, b, *, seq):
    t = main.shape[0]
    cw = 256
    rows = min(seq, 256)
    cb0 = col0 // cw
    kern = functools.partial(_conv_kernel, seq=seq, rows=rows)
    return pl.pallas_call(
        kern,
        grid=(t // seq, XBC_WIDTH // cw),
        in_specs=[pl.BlockSpec((seq, cw), lambda b_, c: (b_, cb0 + c)),
                  pl.BlockSpec((SUBLANES, cw), lambda b_, c: (0, c)),
                  pl.BlockSpec((1, cw), lambda b_, c: (0, c))],
        out_specs=pl.BlockSpec((seq, cw), lambda b_, c: (b_, c)),
        out_shape=jax.ShapeDtypeStruct((t, XBC_WIDTH), BF16),
        scratch_shapes=[pltpu.VMEM((seq + 2 * SUBLANES, cw), F32)],
        compiler_params=_cparams(("arbitrary", "arbitrary"), 40),
        name="conv",
    )(main, w8, b)


def _attn_kernel(sink_ref, q_ref, kp_ref, kc_ref, kn_ref, vp_ref, vc_ref, vn_ref,
                 kx_ref, vx_ref, o_ref, *, nb):
    qi = pl.program_id(1)
    hk = pl.program_id(2)
    blk = ATTN_BLOCK
    g = Q_PER_KV
    q = jnp.concatenate([q_ref[:, i * HEAD_DIM:(i + 1) * HEAD_DIM] for i in range(g)], axis=0)
    k_loc = jnp.concatenate([kp_ref[...], kc_ref[...], kn_ref[...]], axis=0)
    v_loc = jnp.concatenate([vp_ref[...], vc_ref[...], vn_ref[...]], axis=0)
    scale = HEAD_DIM ** -0.5
    s_loc = _dot(q, k_loc, _NT) * scale
    s_ctx = _dot(q, kx_ref[...], _NT) * scale
    r = lax.broadcasted_iota(I32, s_loc.shape, 0) & (blk - 1)
    c = lax.broadcasted_iota(I32, s_loc.shape, 1)
    lo = jnp.maximum(r, jnp.where(qi == 0, blk, 0))
    hi = jnp.minimum(r + 2 * WINDOW, jnp.where(qi == nb - 1, 2 * blk - 1, 3 * blk - 1))
    s_loc = jnp.where((c >= lo) & (c <= hi), s_loc, NEG_BIG)
    row = lax.broadcasted_iota(I32, (g * blk, 1), 0)
    sink = jnp.zeros((g * blk, 1), F32)
    for i in range(g):
        sink = jnp.where((row >= i * blk) & (row < (i + 1) * blk), sink_ref[hk * g + i], sink)
    m = jnp.maximum(jnp.maximum(jnp.max(s_loc, axis=1, keepdims=True),
                                jnp.max(s_ctx, axis=1, keepdims=True)), sink)
    p_loc = jnp.exp(s_loc - m)
    p_ctx = jnp.exp(s_ctx - m)
    den = (jnp.sum(p_loc, axis=1, keepdims=True) + jnp.sum(p_ctx, axis=1, keepdims=True)
           + jnp.exp(sink - m))
    o = _dot(p_loc.astype(BF16), v_loc) + _dot(p_ctx.astype(BF16), vx_ref[...])
    o = o / den
    for i in range(g):
        o_ref[:, i * HEAD_DIM:(i + 1) * HEAD_DIM] = o[i * blk:(i + 1) * blk, :].astype(o_ref.dtype)


def _attn(main, main_c, sink, *, batch, seq, ctx_len):
    t = main.shape[0]
    blk = ATTN_BLOCK
    nb = seq // blk
    gw = Q_PER_KV * HEAD_DIM
    kcol = COL_K // HEAD_DIM
    vcol = COL_V // HEAD_DIM
    kcol_c = XBC_WIDTH // HEAD_DIM
    vcol_c = kcol_c + N_KV_HEADS

    def nbr(col, off):
        return pl.BlockSpec(
            (blk, HEAD_DIM),
            lambda b, i, h: (b * nb + jnp.clip(i + off, 0, nb - 1), col + h))

    in_specs = [pl.BlockSpec(memory_space=pltpu.SMEM),
                pl.BlockSpec((blk, gw), lambda b, i, h: (b * nb + i, h)),
                nbr(kcol, -1), nbr(kcol, 0), nbr(kcol, 1),
                nbr(vcol, -1), nbr(vcol, 0), nbr(vcol, 1),
                pl.BlockSpec((ctx_len, HEAD_DIM), lambda b, i, h: (b, kcol_c + h)),
                pl.BlockSpec((ctx_len, HEAD_DIM), lambda b, i, h: (b, vcol_c + h))]
    return pl.pallas_call(
        functools.partial(_attn_kernel, nb=nb),
        grid=(batch, nb, N_KV_HEADS),
        in_specs=in_specs,
        out_specs=pl.BlockSpec((blk, gw), lambda b, i, h: (b * nb + i, h)),
        out_shape=jax.ShapeDtypeStruct((t, ATTN_WIDTH), BF16),
        compiler_params=_cparams(("arbitrary", "arbitrary", "arbitrary"), 40),
        name="attn",
    )(sink, main, main, main, main, main, main, main, main_c, main_c)


def _softplus(x):
    return jnp.maximum(x, 0.0) + jnp.log1p(jnp.exp(-jnp.abs(x)))


def _ssd_chunk(u_ref, dtp_ref, bias_ref, a_ref, e_ref, tri_ref, h_scr, y_ref, d):
    q = u_ref.shape[0]
    rev = d == 1
    p = SSD_HEAD_DIM
    gw = SSD_HEADS_PER_GROUP * p
    dt = _softplus(dtp_ref[...] + bias_ref[...])
    dta = dt * a_ref[...]
    tri = tri_ref[d]
    a_cum = _dot_rhs_f32(tri, dta)
    a_row = _dot_lhs_f32(dta.T, tri, _NT)
    edge = a_cum[0:1, :] if rev else a_cum[q - 1:q, :]
    ea = jnp.exp(a_cum)
    w_state = dt * jnp.exp(edge - a_cum)
    e_mat = e_ref[d]
    dt_e = _dot_lhs_f32(dt, e_mat)
    ws_e = _dot_lhs_f32(w_state, e_mat)
    x = u_ref[:, 0:SSD_WIDTH].astype(F32)
    xdd = (x * ws_e).astype(BF16)
    li = lax.broadcasted_iota(I32, (q, q), 0)
    si = lax.broadcasted_iota(I32, (q, q), 1)
    keep = (si >= li) if rev else (si <= li)
    ea_e = _dot_lhs_f32(ea, e_mat)
    edge_e = ea_e[0:1, :] if rev else ea_e[q - 1:q, :]
    if y_ref is not None:
        xd = (x * dt_e).astype(BF16)
    for g in range(SSD_GROUPS):
        b_g = u_ref[:, SSD_WIDTH + g * SSD_STATE:SSD_WIDTH + (g + 1) * SSD_STATE]
        c0 = SSD_WIDTH + SSD_GROUPS * SSD_STATE + g * SSD_STATE
        c_g = u_ref[:, c0:c0 + SSD_STATE]
        h_t = h_scr[d, g]
        if y_ref is not None:
            cb = _dot(c_g, b_g, _NT)
            ys = []
            for r in range(SSD_HEADS_PER_GROUP):
                hd = g * SSD_HEADS_PER_GROUP + r
                ln = d * SSD_HEADS + hd
                seg = a_cum[:, ln:ln + 1] - a_row[ln:ln + 1, :]
                decay = jnp.exp(jnp.where(keep, seg, NEG_BIG))
                ys.append(_dot((cb * decay).astype(BF16), xd[:, hd * p:(hd + 1) * p]))
            y_off = _dot(c_g, h_t.astype(BF16)) * ea_e[:, g * gw:(g + 1) * gw]
            y_ref[:, g * gw:(g + 1) * gw] = jnp.concatenate(ys, axis=1) + y_off
        st = _dot(b_g, xdd[:, g * gw:(g + 1) * gw], _TN)
        h_scr[d, g] = h_t * edge_e[:, g * gw:(g + 1) * gw] + st


def _ssd_kernel(*refs, emit_y):
    if emit_y:
        (uf_ref, ub_ref, df_ref, db_ref, h0_ref, bias_ref, a_ref, e_ref, tri_ref,
         yf_ref, yb_ref, h_scr) = refs
    else:
        (uf_ref, ub_ref, df_ref, db_ref, bias_ref, a_ref, e_ref, tri_ref, hout_ref, h_scr) = refs
        yf_ref = yb_ref = None
    c = pl.program_id(1)

    @pl.when(c == 0)
    def _():
        if emit_y:
            h_scr[...] = h0_ref[0]
        else:
            h_scr[...] = jnp.zeros(h_scr.shape, F32)

    _ssd_chunk(uf_ref, df_ref, bias_ref, a_ref, e_ref, tri_ref, h_scr, yf_ref, 0)
    _ssd_chunk(ub_ref, db_ref, bias_ref, a_ref, e_ref, tri_ref, h_scr, yb_ref, 1)
    if not emit_y:
        @pl.when(c == pl.num_programs(1) - 1)
        def _():
            hout_ref[0] = h_scr[...]


def _ssd(u, dtp, h0, bias, a_neg, e_mat, tri, *, batch, seq, q):
    t = u.shape[0]
    nc = seq // q
    emit_y = h0 is not None
    hshape = (N_DIRS, SSD_GROUPS, SSD_STATE, SSD_HEADS_PER_GROUP * SSD_HEAD_DIM)
    fwd = lambda b, c: (b * nc + c, 0)
    bwd = lambda b, c: (b * nc + nc - 1 - c, 0)
    const2 = lambda b, c: (0, 0)
    const3 = lambda b, c: (0, 0, 0)
    in_specs = [pl.BlockSpec((q, XBC_WIDTH), fwd), pl.BlockSpec((q, XBC_WIDTH), bwd),
                pl.BlockSpec((q, LANES), fwd), pl.BlockSpec((q, LANES), bwd)]
    args = [u, u, dtp, dtp]
    if emit_y:
        in_specs.append(pl.BlockSpec((1,) + hshape, lambda b, c: (b, 0, 0, 0, 0)))
        args.append(h0)
    in_specs += [pl.BlockSpec((1, LANES), const2), pl.BlockSpec((1, LANES), const2),
                 pl.BlockSpec((N_DIRS, LANES, SSD_WIDTH), const3),
                 pl.BlockSpec((N_DIRS, q, q), const3)]
    args += [bias, a_neg, e_mat, tri]
    if emit_y:
        out_specs = [pl.BlockSpec((q, SSD_WIDTH), fwd), pl.BlockSpec((q, SSD_WIDTH), bwd)]
        out_shape = [jax.ShapeDtypeStruct((t, SSD_WIDTH), F32)] * 2
    else:
        out_specs = pl.BlockSpec((1,) + hshape, lambda b, c: (b, 0, 0, 0, 0))
        out_shape = jax.ShapeDtypeStruct((batch,) + hshape, F32)
    return pl.pallas_call(
        functools.partial(_ssd_kernel, emit_y=emit_y),
        grid=(batch, nc),
        in_specs=in_specs,
        out_specs=out_specs,
        out_shape=out_shape,
        scratch_shapes=[pltpu.VMEM(hshape, F32)],
        compiler_params=_cparams(("arbitrary", "arbitrary"), 48),
        name="ssd_lat" if emit_y else "ssd_ctx",
    )(*args)


def _outproj_kernel(attn_ref, yf_ref, yb_ref, xs_ref, z_ref, x_ref, w_ref, skip_ref, ng_ref,
                    g1_ref, n2_ref, sc2_ref, sh2_ref, rw_ref, x1_ref, h2_ref, lg_ref):
    y = yf_ref[...] + yb_ref[...] + skip_ref[...] * xs_ref[...].astype(F32)
    y = y * _silu(z_ref[...].astype(F32))
    y = (_rms(y) * ng_ref[...]).astype(BF16)
    m = _dot(attn_ref[...], w_ref[0:ATTN_WIDTH, :]) + _dot(y, w_ref[ATTN_WIDTH:, :])
    x1 = x_ref[...] + g1_ref[0] * m
    x1_ref[...] = x1
    h2 = (_rms(x1) * n2_ref[...]) * (1.0 + sc2_ref[0]) + sh2_ref[0]
    _store_rows(h2_ref, h2, _chunks(h2.shape[1]))
    lg_ref[...] = _dot_f32(rw_ref[...], h2, _NT)


def _outproj(attn, yf, yb, u, main, x, w_out, skip_e, ssd_g, g1, n2, sc2, sh2, rw_t, *, seq):
    t, d = x.shape
    tm = TOK_TILE
    per = seq // tm
    row = lambda i: (i, 0)
    c2 = lambda i: (0, 0)
    bat = lambda i: (i // per, 0, 0)
    in_specs = [pl.BlockSpec((tm, ATTN_WIDTH), row),
                pl.BlockSpec((tm, SSD_WIDTH), row),
                pl.BlockSpec((tm, SSD_WIDTH), row),
                pl.BlockSpec((tm, SSD_WIDTH), row),
                pl.BlockSpec((tm, SSD_WIDTH), lambda i: (i, COL_Z // SSD_WIDTH)),
                pl.BlockSpec((tm, d), row),
                pl.BlockSpec((ATTN_WIDTH + SSD_WIDTH, d), c2),
                pl.BlockSpec((1, SSD_WIDTH), c2),
                pl.BlockSpec((1, SSD_WIDTH), c2),
                pl.BlockSpec((1, 1, d), bat),
                pl.BlockSpec((1, d), c2),
                pl.BlockSpec((1, 1, d), bat),
                pl.BlockSpec((1, 1, d), bat),
                pl.BlockSpec((N_EXPERTS, d), c2)]
    return pl.pallas_call(
        _outproj_kernel,
        grid=(t // tm,),
        in_specs=in_specs,
        out_specs=[pl.BlockSpec((tm, d), row), pl.BlockSpec((tm * _chunks(d), LANES), row),
                   pl.BlockSpec((N_EXPERTS, tm), lambda i: (0, i))],
        out_shape=[jax.ShapeDtypeStruct((t, d), F32),
                   jax.ShapeDtypeStruct((t * _chunks(d), LANES), U32),
                   jax.ShapeDtypeStruct((N_EXPERTS, t), F32)],
        compiler_params=_cparams(("arbitrary",), 52),
        name="outproj",
    )(attn, yf, yb, u, main, x, w_out, skip_e, ssd_g, g1, n2, sc2, sh2, rw_t)


def _first_index(hit, iota, size):
    return jnp.min(jnp.where(hit, iota, size), axis=0, keepdims=True)


def _route_kernel(lg_ref, bias_ref, su_ref, idx_ref, w_ref, rank_ref, cnt_ref, carry_scr):
    i = pl.program_id(0)
    tm = lg_ref.shape[1]

    @pl.when(i == 0)
    def _():
        carry_scr[...] = jnp.zeros(carry_scr.shape, F32)

    scores = _sigmoid(lg_ref[...])
    sel = scores + bias_ref[...][:, 0:1]
    e8 = EXPERTS_PER_GROUP
    i8 = lax.broadcasted_iota(I32, (e8, tm), 0).astype(F32)
    gs = []
    for g in range(N_EXPERT_GROUPS):
        blk = sel[g * e8:(g + 1) * e8, :]
        m1 = jnp.max(blk, axis=0, keepdims=True)
        f1 = _first_index(blk == m1, i8, e8)
        m2 = jnp.max(jnp.where(i8 == f1, NEG_BIG, blk), axis=0, keepdims=True)
        gs.append(m1 + m2)
    gsc = jnp.concatenate(gs, axis=0)
    ig = lax.broadcasted_iota(I32, gsc.shape, 0).astype(F32)
    keep = jnp.zeros(gsc.shape, F32)
    for _ in range(TOPK_GROUPS):
        m = jnp.max(gsc, axis=0, keepdims=True)
        f = _first_index(gsc == m, ig, N_EXPERT_GROUPS)
        keep = jnp.where(ig == f, 1.0, keep)
        gsc = jnp.where(ig == f, NEG_BIG, gsc)
    keep_e = jnp.concatenate(
        [jnp.broadcast_to(keep[g:g + 1, :], (e8, tm)) for g in range(N_EXPERT_GROUPS)], axis=0)
    cand = jnp.where(keep_e > 0.5, sel, NEG_BIG)
    ie = lax.broadcasted_iota(I32, cand.shape, 0).astype(F32)
    chosen = jnp.zeros(cand.shape, F32)
    idxs, svals, hits = [], [], []
    for _ in range(TOP_K):
        m = jnp.max(cand, axis=0, keepdims=True)
        f = _first_index(cand == m, ie, N_EXPERTS)
        hit = ie == f
        idxs.append(f)
        svals.append(jnp.sum(jnp.where(hit, scores, 0.0), axis=0, keepdims=True))
        hits.append(hit)
        chosen = jnp.where(hit, 1.0, chosen)
        cand = jnp.where(hit, NEG_BIG, cand)
    s_all = jnp.concatenate(svals, axis=0)
    idx_ref[...] = jnp.concatenate(idxs, axis=0).astype(I32)
    w_ref[...] = ROUTED_SCALE * s_all / jnp.sum(s_all, axis=0, keepdims=True)
    before = _dot(chosen.astype(BF16), su_ref[...]) + carry_scr[...][:, 0:1]
    ranks = [jnp.sum(jnp.where(h, before, 0.0), axis=0, keepdims=True) for h in hits]
    rank_ref[...] = jnp.concatenate(ranks, axis=0).astype(I32)
    carry_scr[...] = carry_scr[...] + jnp.sum(chosen, axis=1, keepdims=True)
    cnt_ref[...] = carry_scr[...].astype(I32)


def _route(logits_t, bias_col):
    e, t = logits_t.shape
    tm = ROUTE_TILE
    su = jnp.triu(jnp.ones((tm, tm), F32), 1).astype(BF16)
    col = lambda i: (0, i)
    return pl.pallas_call(
        _route_kernel,
        grid=(t // tm,),
        in_specs=[pl.BlockSpec((e, tm), col),
                  pl.BlockSpec((e, LANES), lambda i: (0, 0)),
                  pl.BlockSpec((tm, tm), lambda i: (0, 0))],
        out_specs=[pl.BlockSpec((TOP_K, tm), col), pl.BlockSpec((TOP_K, tm), col),
                   pl.BlockSpec((TOP_K, tm), col), pl.BlockSpec((e, LANES), lambda i: (0, 0))],
        out_shape=[jax.ShapeDtypeStruct((TOP_K, t), I32), jax.ShapeDtypeStruct((TOP_K, t), F32),
                   jax.ShapeDtypeStruct((TOP_K, t), I32), jax.ShapeDtypeStruct((e, LANES), I32)],
        scratch_shapes=[pltpu.VMEM((e, LANES), F32)],
        compiler_params=_cparams(("arbitrary",), 40),
        name="route",
    )(logits_t, bias_col, su)


def _slots_kernel(idx_ref, rank_ref, ps_ref, d_ref, dc_ref):
    tm = idx_ref.shape[1]
    ie = lax.broadcasted_iota(I32, (N_EXPERTS, tm), 0)
    ps = ps_ref[...][:, 0:1]
    rows = []
    for k in range(TOP_K):
        hit = ie == idx_ref[k:k + 1, :]
        rows.append(jnp.sum(jnp.where(hit, ps, 0.0), axis=0, keepdims=True))
    dest = jnp.concatenate(rows, axis=0).astype(I32) + rank_ref[...]
    d_ref[0] = dest
    tc = dc_ref.shape[2]
    for s in range(tm // tc):
        dc_ref[s] = dest[:, s * tc:(s + 1) * tc]


def _slots(idx_t, rank_t, ps_col):
    t = idx_t.shape[1]
    tm = TOK_TILE
    tc = COMB_TILE
    col = lambda i: (0, i)
    return pl.pallas_call(
        _slots_kernel,
        grid=(t // tm,),
        in_specs=[pl.BlockSpec((TOP_K, tm), col), pl.BlockSpec((TOP_K, tm), col),
                  pl.BlockSpec((N_EXPERTS, LANES), lambda i: (0, 0))],
        out_specs=[pl.BlockSpec((1, TOP_K, tm), lambda i: (i, 0, 0)),
                   pl.BlockSpec((tm // tc, TOP_K, tc), lambda i: (i, 0, 0))],
        out_shape=[jax.ShapeDtypeStruct((t // tm, TOP_K, tm), I32),
                   jax.ShapeDtypeStruct((t // tc, TOP_K, tc), I32)],
        compiler_params=_cparams(("arbitrary",), 40),
        name="slots",
    )(idx_t, rank_t, ps_col)


def _row_copy(src, dst, sem):
    return pltpu.make_async_copy(src, dst, sem)


def _dispatch_kernel(pend_ref, dest_ref, h_ref, xs_ref, zero_scr, sem, *, nch):
    i = pl.program_id(0)
    tm = h_ref.shape[0] // nch
    tile_rows = zero_scr.shape[0]

    @pl.when(i == 0)
    def _():
        zero_scr[...] = jnp.zeros(zero_scr.shape, zero_scr.dtype)

        def pad_copy(e):
            start_row = pl.multiple_of(pend_ref[e + 1] * nch - tile_rows, tile_rows)
            return _row_copy(zero_scr, xs_ref.at[pl.ds(start_row, tile_rows), :], sem.at[0])

        def start(e, carry):
            @pl.when(pend_ref[e + 1] > pend_ref[e])
            def _():
                pad_copy(e).start()
            return carry

        def wait(e, carry):
            @pl.when(pend_ref[e + 1] > pend_ref[e])
            def _():
                pad_copy(e).wait()
            return carry

        lax.fori_loop(0, N_EXPERTS, start, 0)
        lax.fori_loop(0, N_EXPERTS, wait, 0)

    def tok_copy(t, k):
        src = pl.multiple_of(t * nch, nch)
        dst = pl.multiple_of(dest_ref[0, k, t] * nch, nch)
        return _row_copy(h_ref.at[pl.ds(src, nch), :], xs_ref.at[pl.ds(dst, nch), :], sem.at[1])

    def start(t, carry):
        for k in range(TOP_K):
            tok_copy(t, k).start(priority=k % 2)
        return carry

    def wait(t, carry):
        for k in range(TOP_K):
            tok_copy(t, k).wait()
        return carry

    lax.fori_loop(0, tm, start, 0)
    lax.fori_loop(0, tm, wait, 0)


def _dispatch(pad_end_ext, dest3, h2, n_slots, nch):
    t = h2.shape[0] // nch
    tm = TOK_TILE
    grid_spec = pltpu.PrefetchScalarGridSpec(
        num_scalar_prefetch=1,
        grid=(t // tm,),
        in_specs=[pl.BlockSpec((1, TOP_K, tm), lambda i, pe: (i, 0, 0), memory_space=pltpu.SMEM),
                  pl.BlockSpec((tm * nch, LANES), lambda i, pe: (i, 0))],
        out_specs=pl.BlockSpec(memory_space=pl.ANY),
        scratch_shapes=[pltpu.VMEM((MOE_TILE * nch, LANES), U32), pltpu.SemaphoreType.DMA((2,))],
    )
    return pl.pallas_call(
        functools.partial(_dispatch_kernel, nch=nch),
        grid_spec=grid_spec,
        out_shape=jax.ShapeDtypeStruct((n_slots * nch, LANES), U32),
        compiler_params=_cparams(("arbitrary",), 40),
        name="dispatch",
    )(pad_end_ext, dest3, h2)


def _experts_kernel(texp_ref, nuse_ref, x_ref, wg_ref, wu_ref, wd_ref, y_ref,
                    wg_scr, wu_scr, wd_scr):
    i = pl.program_id(0)
    prev = texp_ref[jnp.maximum(i - 1, 0)]
    fresh = (i == 0) | (texp_ref[i] != prev)

    @pl.when(i < nuse_ref[0])
    def _():
        @pl.when(fresh)
        def _():
            wg_scr[...] = wg_ref[0].astype(BF16)
            wu_scr[...] = wu_ref[0].astype(BF16)
            wd_scr[...] = wd_ref[0].astype(BF16)

        nch = _chunks(wg_scr.shape[0])
        xb = _load_rows(x_ref, x_ref.shape[0] // nch, nch).astype(BF16)
        act = _silu(_dot(xb, wg_scr[...])) * _dot(xb, wu_scr[...])
        _store_rows(y_ref, _dot(act.astype(BF16), wd_scr[...]), nch)


def _experts(tile_exp, n_used, xs, wg, wu, wd):
    d, f = wg.shape[1], wg.shape[2]
    nch = _chunks(d)
    n_slots = xs.shape[0] // nch
    te = MOE_TILE
    nt = n_slots // te

    def live(i, texp, nuse):
        return jnp.minimum(i, nuse[0] - 1)

    grid_spec = pltpu.PrefetchScalarGridSpec(
        num_scalar_prefetch=2,
        grid=(nt,),
        in_specs=[pl.BlockSpec((te * nch, LANES), lambda i, texp, nuse: (live(i, texp, nuse), 0)),
                  pl.BlockSpec((1, d, f), lambda i, texp, nuse: (texp[live(i, texp, nuse)], 0, 0)),
                  pl.BlockSpec((1, d, f), lambda i, texp, nuse: (texp[live(i, texp, nuse)], 0, 0)),
                  pl.BlockSpec((1, f, d), lambda i, texp, nuse: (texp[live(i, texp, nuse)], 0, 0))],
        out_specs=pl.BlockSpec((te * nch, LANES), lambda i, texp, nuse: (live(i, texp, nuse), 0)),
        scratch_shapes=[pltpu.VMEM((d, f), BF16), pltpu.VMEM((d, f), BF16), pltpu.VMEM((f, d), BF16)],
    )
    return pl.pallas_call(
        _experts_kernel,
        grid_spec=grid_spec,
        out_shape=jax.ShapeDtypeStruct((n_slots * nch, LANES), U32),
        compiler_params=_cparams(("arbitrary",), 56),
        name="experts",
    )(tile_exp, n_used, xs, wg, wu, wd)


def _combine_kernel(dest_ref, dnext_ref, ys_ref, wt_ref, h_ref, x1_ref, sg_ref, su_ref, sd_ref,
                    g2_ref, fg_ref, eye_ref, o_ref, buf, sem):
    i = pl.program_id(0)
    n = pl.num_programs(0)
    tm = x1_ref.shape[0]
    nch = _chunks(x1_ref.shape[1])
    slot = i % 2

    def row_copy(d_ref, s, t, k):
        src = pl.multiple_of(d_ref[0, k, t] * nch, nch)
        dst = pl.multiple_of(t * nch, nch)
        return _row_copy(ys_ref.at[pl.ds(src, nch), :], buf.at[s, k, pl.ds(dst, nch), :], sem.at[s])

    def start_tile(d_ref, s):
        def body(t, carry):
            for k in range(TOP_K):
                row_copy(d_ref, s, t, k).start(priority=k % 2)
            return carry
        lax.fori_loop(0, tm, body, 0)

    @pl.when(i == 0)
    def _():
        start_tile(dest_ref, 0)

    @pl.when(i + 1 < n)
    def _():
        start_tile(dnext_ref, 1 - slot)

    hb = _load_rows(h_ref, tm, nch).astype(BF16)
    act = _silu(_dot(hb, sg_ref[...])) * _dot(hb, su_ref[...])
    ffn = _dot(act.astype(BF16), sd_ref[...])
    w_col = _dot_rhs_f32(eye_ref[...], wt_ref[...], _NT)

    def wait(t, carry):
        for k in range(TOP_K):
            row_copy(dest_ref, slot, t, k).wait()
        return carry

    lax.fori_loop(0, tm, wait, 0)
    for k in range(TOP_K):
        ffn = ffn + w_col[:, k:k + 1] * _load_rows(buf.at[slot, k], tm, nch)
    x2 = x1_ref[...] + g2_ref[0] * ffn
    o_ref[...] = _rms(x2) * fg_ref[...]


def _combine(dest3, ys, w_t, h2, x1, sg, su, sd, g2, fg, *, seq):
    t, d = x1.shape
    nch = _chunks(d)
    tm = COMB_TILE
    per = seq // tm
    f = sg.shape[1]
    eye = jnp.eye(tm, dtype=BF16)
    row = lambda i: (i, 0)
    c2 = lambda i: (0, 0)
    last = t // tm - 1
    return pl.pallas_call(
        _combine_kernel,
        grid=(t // tm,),
        in_specs=[pl.BlockSpec((1, TOP_K, tm), lambda i: (i, 0, 0), memory_space=pltpu.SMEM),
                  pl.BlockSpec((1, TOP_K, tm), lambda i: (jnp.minimum(i + 1, last), 0, 0),
                               memory_space=pltpu.SMEM),
                  pl.BlockSpec(memory_space=pl.ANY),
                  pl.BlockSpec((TOP_K, tm), lambda i: (0, i)),
                  pl.BlockSpec((tm * nch, LANES), row),
                  pl.BlockSpec((tm, d), row),
                  pl.BlockSpec((d, f), c2), pl.BlockSpec((d, f), c2), pl.BlockSpec((f, d), c2),
                  pl.BlockSpec((1, 1, d), lambda i: (i // per, 0, 0)),
                  pl.BlockSpec((1, d), c2),
                  pl.BlockSpec((tm, tm), c2)],
        out_specs=pl.BlockSpec((tm, d), row),
        out_shape=jax.ShapeDtypeStruct((t, d), F32),
        scratch_shapes=[pltpu.VMEM((2, TOP_K, tm * nch, LANES), U32),
                        pltpu.SemaphoreType.DMA((2,))],
        compiler_params=_cparams(("arbitrary",), 48),
        name="combine",
    )(dest3, dest3, ys, w_t, h2, x1, sg, su, sd, g2, fg, eye)


def _rope_tables(seq):
    n_rows = seq // GRID_W
    rows = jnp.repeat(jnp.arange(n_rows, dtype=F32), GRID_W)
    cols = jnp.tile(jnp.arange(GRID_W, dtype=F32), n_rows)
    inv = ROPE_THETA ** (-jnp.arange(0, ROPE_AXIS_DIM, 2, dtype=F32) / ROPE_AXIS_DIM)
    ar = rows[:, None] * inv[None, :]
    ac = cols[:, None] * inv[None, :]
    cos = jnp.concatenate([jnp.cos(ar), jnp.cos(ar), jnp.cos(ac), jnp.cos(ac)], axis=1)
    sin = jnp.concatenate([-jnp.sin(ar), jnp.sin(ar), -jnp.sin(ac), jnp.sin(ac)], axis=1)
    return cos, sin


def _layer(x, c, ctx, c_ctx, w_ada, b_ada, norm1_g, norm2_g, w_in, attn_sink, conv_w, conv_b,
           dt_bias, a_log, d_skip, ssd_norm_g, w_out, router_w, router_bias, wg, wu, wd,
           sw_gate, sw_up, sw_down, final_norm_g):
    batch, seq, d = x.shape
    ctx_len = ctx.shape[1]
    t = batch * seq

    cvec = jnp.zeros((SUBLANES, d), F32).at[:batch].set(c).at[batch].set(c_ctx)
    mod = _ada(cvec, w_ada, b_ada[None, :])
    sh1, sc1, g1, sh2, sc2, g2 = [mod[:batch, i * d:(i + 1) * d][:, None, :] for i in range(6)]
    sh1c, sc1c = [jnp.broadcast_to(mod[batch, i * d:(i + 1) * d][None, None, :], (batch, 1, d))
                  for i in range(2)]

    o_q, o_k, o_v, o_z, o_x, o_dt = np.cumsum((0, ATTN_WIDTH, KV_WIDTH, KV_WIDTH, SSD_WIDTH,
                                               XBC_WIDTH)).tolist()
    w_main = jnp.concatenate([w_in[:, o_q:o_k], w_in[:, o_z:o_x], w_in[:, o_x:o_dt],
                              w_in[:, o_k:o_v], w_in[:, o_v:o_z]], axis=1).astype(BF16)
    w_dt = jnp.pad(w_in[:, o_dt:], ((0, 0), (0, LANES - N_DIRS * SSD_HEADS))).astype(BF16)
    cos, sin = _rope_tables(seq)
    n1 = norm1_g[None, :]
    main, dtp = _inproj(x.reshape(t, d), n1, sc1, sh1, w_main, w_dt, cos, sin,
                        seq=seq, tm=min(seq, 1024), rope=True)
    main_c, dtp_c = _inproj(ctx.reshape(batch * ctx_len, d), n1, sc1c, sh1c,
                            w_main[:, COL_XBC:], w_dt, None, None,
                            seq=ctx_len, tm=ctx_len, rope=False)

    attn = _attn(main, main_c, attn_sink, batch=batch, seq=seq, ctx_len=ctx_len)

    cw8 = jnp.pad(conv_w, ((0, SUBLANES - SSD_CONV), (0, 0)))
    u = _conv(main, COL_XBC, cw8, conv_b[None, :], seq=seq)
    u_c = _conv(main_c, 0, cw8, conv_b[None, :], seq=ctx_len)
    nd = N_DIRS * SSD_HEADS
    bias = jnp.pad(dt_bias.reshape(1, nd), ((0, 0), (0, LANES - nd)))
    a_neg = jnp.pad(-jnp.exp(a_log.reshape(1, nd)), ((0, 0), (0, LANES - nd)))
    lane_head = np.arange(SSD_WIDTH) // SSD_HEAD_DIM
    e_np = np.zeros((N_DIRS, LANES, SSD_WIDTH), np.float32)
    for dd in range(N_DIRS):
        e_np[dd, dd * SSD_HEADS + lane_head, np.arange(SSD_WIDTH)] = 1.0
    e_mat = jnp.asarray(e_np, BF16)

    def tri(q):
        lo = np.tril(np.ones((q, q), np.float32))
        return jnp.asarray(np.stack([lo, lo.T]), BF16)

    qc = min(SSD_CHUNK, ctx_len)
    h_ctx = _ssd(u_c, dtp_c, None, bias, a_neg, e_mat, tri(qc), batch=batch, seq=ctx_len, q=qc)
    ql = min(SSD_CHUNK, seq)
    yf, yb = _ssd(u, dtp, h_ctx, bias, a_neg, e_mat, tri(ql), batch=batch, seq=seq, q=ql)

    skip_e = jnp.repeat(d_skip, SSD_HEAD_DIM)[None, :]
    x1, h2, logits_t = _outproj(attn, yf, yb, u, main, x.reshape(t, d), w_out.astype(BF16),
                                skip_e, ssd_norm_g[None, :], g1, norm2_g[None, :], sc2, sh2,
                                router_w.T, seq=seq)

    bias_col = jnp.broadcast_to(router_bias[:, None], (N_EXPERTS, LANES))
    idx_t, w_t, rank_t, cnt = _route(logits_t, bias_col)
    counts = cnt[:, 0]
    te = MOE_TILE
    padded = (counts + te - 1) // te * te
    pad_end = jnp.cumsum(padded)
    pad_start = pad_end - padded
    n_slots = -(-(t * TOP_K + N_EXPERTS * (te - 1)) // te) * te
    n_tiles = n_slots // te
    ps_col = jnp.broadcast_to(pad_start.astype(F32)[:, None], (N_EXPERTS, LANES))
    dest3, dest3c = _slots(idx_t, rank_t, ps_col)
    tile_start = jnp.arange(n_tiles, dtype=I32) * te
    tile_exp = jnp.minimum(jnp.sum(tile_start[:, None] >= pad_end[None, :], axis=1),
                           N_EXPERTS - 1).astype(I32)
    n_used = (pad_end[-1:] // te).astype(I32)
    pad_end_ext = jnp.concatenate([jnp.zeros((1,), I32), pad_end.astype(I32)])

    xs = _dispatch(pad_end_ext, dest3, h2, n_slots, _chunks(d))
    ys = _experts(tile_exp, n_used, xs, wg, wu, wd)
    out = _combine(dest3c, ys, w_t, h2, x1, sw_gate.astype(BF16), sw_up.astype(BF16),
                   sw_down.astype(BF16), g2, final_norm_g[None, :], seq=seq)
    return out.reshape(batch, seq, d)


def kernel(x, c, ctx, c_ctx, w_ada, b_ada, norm1_g, norm2_g, w_in, attn_sink, conv_w, conv_b, dt_bias, a_log, d_skip, ssd_norm_g, w_out, router_w, router_bias, expert_w_gate, expert_w_up, expert_w_down, shared_w_gate, shared_w_up, shared_w_down, final_norm_g):
    assert w_ada.shape[0] == 1, "single-layer stack only"
    return _layer(x, c, ctx, c_ctx, w_ada[0], b_ada[0], norm1_g[0], norm2_g[0], w_in[0],
                  attn_sink[0], conv_w[0], conv_b[0], dt_bias[0], a_log[0], d_skip[0],
                  ssd_norm_g[0], w_out[0], router_w[0], router_bias[0], expert_w_gate[0],
                  expert_w_up[0], expert_w_down[0], shared_w_gate[0], shared_w_up[0],
                  shared_w_down[0], final_norm_g)
```

```python
import functools
import math

import jax
import jax.numpy as jnp
import numpy as np
from jax import lax
from jax.experimental import pallas as pl
from jax.experimental.pallas import tpu as pltpu

F32 = jnp.float32
BF16 = jnp.bfloat16
I32 = jnp.int32

EPS = 1e-6
GRID_W = 64
N_Q_HEADS = 8
N_KV_HEADS = 2
Q_PER_KV = N_Q_HEADS // N_KV_HEADS
HEAD_DIM = 128
ATTN_WIDTH = N_Q_HEADS * HEAD_DIM
KV_WIDTH = N_KV_HEADS * HEAD_DIM
WINDOW = 128
ATTN_BLOCK = 128
ROPE_THETA = 10000.0
ROPE_AXIS_DIM = HEAD_DIM // 2
SSD_HEADS = 16
SSD_HEAD_DIM = 64
SSD_WIDTH = SSD_HEADS * SSD_HEAD_DIM
SSD_GROUPS = 4
SSD_HEADS_PER_GROUP = SSD_HEADS // SSD_GROUPS
SSD_STATE = 128
SSD_CONV = 5
N_DIRS = 2
XBC_WIDTH = SSD_WIDTH + 2 * SSD_GROUPS * SSD_STATE
N_EXPERTS = 64
N_EXPERT_GROUPS = 8
EXPERTS_PER_GROUP = N_EXPERTS // N_EXPERT_GROUPS
TOPK_GROUPS = 4
TOP_K = 8
EXPERT_DIM = 512
ROUTED_SCALE = 2.5

LANES = 128
SUBLANES = 8
NEG_BIG = -1e30

COL_Q = 0
COL_Z = ATTN_WIDTH
COL_XBC = COL_Z + SSD_WIDTH
COL_K = COL_XBC + XBC_WIDTH
COL_V = COL_K + KV_WIDTH
MAIN_WIDTH = COL_V + KV_WIDTH

IN_TILE_N = 512
SSD_CHUNK = 256
MOE_TILE = 256
ROUTE_TILE = 512
TOK_TILE = 256
COMB_TILE = 128

_NT = (((1,), (1,)), ((), ()))
_TN = (((0,), (0,)), ((), ()))


def _cparams(sem, vmem_mb):
    return pltpu.CompilerParams(dimension_semantics=sem, vmem_limit_bytes=vmem_mb * 1024 * 1024)


def _dot(a, b, dims=None):
    if dims is None:
        return jnp.dot(a, b, preferred_element_type=F32)
    return lax.dot_general(a, b, dims, preferred_element_type=F32)


def _split(a):
    hi = a.astype(BF16)
    lo = (a - hi.astype(F32)).astype(BF16)
    return hi, lo


def _dot_lhs_f32(a, b_exact, dims=None):
    hi, lo = _split(a)
    return _dot(hi, b_exact, dims) + _dot(lo, b_exact, dims)


def _dot_rhs_f32(a_exact, b, dims=None):
    hi, lo = _split(b)
    return _dot(a_exact, hi, dims) + _dot(a_exact, lo, dims)


def _dot_f32(a, b, dims=None):
    ah, al = _split(a)
    bh, bl = _split(b)
    return _dot(ah, bh, dims) + (_dot(al, bh, dims) + _dot(ah, bl, dims))


def _sigmoid(x):
    return 1.0 / (1.0 + jnp.exp(-x))


def _silu(x):
    return x * _sigmoid(x)


def _rms(x):
    return x * lax.rsqrt(jnp.mean(x * x, axis=-1, keepdims=True) + EPS)


U32 = jnp.uint32
_HI_MASK = 0xFFFF0000


def _chunks(d):
    return d // (2 * LANES)


def _load_rows(ref, rows, nch):
    words = [ref[pl.ds(j, rows, stride=nch), :] for j in range(nch)]
    lo = [lax.bitcast_convert_type(w << 16, F32) for w in words]
    hi = [lax.bitcast_convert_type(w & jnp.uint32(_HI_MASK), F32) for w in words]
    return jnp.concatenate(lo + hi, axis=1)


def _store_rows(ref, val, nch):
    rows, d = val.shape
    bits = lax.bitcast_convert_type(val.astype(BF16).astype(F32), U32)
    for j in range(nch):
        lo = bits[:, j * LANES:(j + 1) * LANES] >> 16
        hi = bits[:, d // 2 + j * LANES:d // 2 + (j + 1) * LANES] & jnp.uint32(_HI_MASK)
        ref[pl.ds(j, rows, stride=nch), :] = lo | hi


def _ada_kernel(c_ref, w_ref, b_ref, o_ref):
    o_ref[...] = _dot_f32(_silu(c_ref[...]), w_ref[...]) + b_ref[...]


def _ada(cvec, w, b):
    m, d = cvec.shape
    n = w.shape[1]
    tn = 1024
    return pl.pallas_call(
        _ada_kernel,
        grid=(n // tn,),
        in_specs=[pl.BlockSpec((m, d), lambda j: (0, 0)),
                  pl.BlockSpec((d, tn), lambda j: (0, j)),
                  pl.BlockSpec((1, tn), lambda j: (0, j))],
        out_specs=pl.BlockSpec((m, tn), lambda j: (0, j)),
        out_shape=jax.ShapeDtypeStruct((m, n), F32),
        compiler_params=_cparams(("arbitrary",), 40),
        name="ada",
    )(cvec, w, b)


def _rope_mix(acc, cos, sin):
    heads = acc.shape[1] // HEAD_DIM
    partner = jnp.concatenate(
        [pltpu.roll(acc[:, h * HEAD_DIM:(h + 1) * HEAD_DIM], HEAD_DIM // 2, 1) for h in range(heads)],
        axis=1)
    return acc * cos + partner * sin


def _inproj_kernel(*refs, rope, n_rope_tiles, kv_tile):
    if rope:
        x_ref, g_ref, sc_ref, sh_ref, w_ref, wdt_ref, cos_ref, sin_ref, o_ref, dt_ref, h_scr = refs
    else:
        x_ref, g_ref, sc_ref, sh_ref, w_ref, wdt_ref, o_ref, dt_ref, h_scr = refs
    j = pl.program_id(1)

    @pl.when(j == 0)
    def _():
        h = (_rms(x_ref[...]) * g_ref[...]) * (1.0 + sc_ref[0]) + sh_ref[0]
        hb = h.astype(BF16)
        h_scr[...] = hb
        dt_ref[...] = _dot(hb, wdt_ref[...])

    acc = _dot(h_scr[...], w_ref[0])
    if not rope:
        o_ref[...] = acc.astype(o_ref.dtype)
        return
    reps = acc.shape[1] // HEAD_DIM

    @pl.when(j < n_rope_tiles)
    def _():
        cos = jnp.concatenate([cos_ref[...]] * reps, axis=1)
        sin = jnp.concatenate([sin_ref[...]] * reps, axis=1)
        o_ref[...] = _rope_mix(acc, cos, sin).astype(o_ref.dtype)

    @pl.when(j == kv_tile)
    def _():
        nk = KV_WIDTH // HEAD_DIM
        ones = jnp.ones_like(cos_ref[...])
        cos = jnp.concatenate([cos_ref[...]] * nk + [ones] * (reps - nk), axis=1)
        sin = jnp.concatenate([sin_ref[...]] * nk + [ones * 0.0] * (reps - nk), axis=1)
        o_ref[...] = _rope_mix(acc, cos, sin).astype(o_ref.dtype)

    @pl.when((j >= n_rope_tiles) & (j != kv_tile))
    def _():
        o_ref[...] = acc.astype(o_ref.dtype)


def _inproj(x, g, sc, sh, w, wdt, cos, sin, *, seq, tm, rope):
    t, d = x.shape
    wn = w.shape[1]
    tn = IN_TILE_N
    per = seq // tm
    w = w.reshape(d, wn // tn, tn).transpose(1, 0, 2)
    in_specs = [pl.BlockSpec((tm, d), lambda i, j: (i, 0)),
                pl.BlockSpec((1, d), lambda i, j: (0, 0)),
                pl.BlockSpec((1, 1, d), lambda i, j: (i // per, 0, 0)),
                pl.BlockSpec((1, 1, d), lambda i, j: (i // per, 0, 0)),
                pl.BlockSpec((1, d, tn), lambda i, j: (j, 0, 0)),
                pl.BlockSpec((d, LANES), lambda i, j: (0, 0))]
    args = [x, g, sc, sh, w, wdt]
    if rope:
        in_specs += [pl.BlockSpec((tm, HEAD_DIM), lambda i, j: (i % per, 0)),
                     pl.BlockSpec((tm, HEAD_DIM), lambda i, j: (i % per, 0))]
        args += [cos, sin]
    kern = functools.partial(_inproj_kernel, rope=rope, n_rope_tiles=ATTN_WIDTH // tn,
                             kv_tile=COL_K // tn)
    return pl.pallas_call(
        kern,
        grid=(t // tm, wn // tn),
        in_specs=in_specs,
        out_specs=[pl.BlockSpec((tm, tn), lambda i, j: (i, j)),
                   pl.BlockSpec((tm, LANES), lambda i, j: (i, 0))],
        out_shape=[jax.ShapeDtypeStruct((t, wn), BF16), jax.ShapeDtypeStruct((t, LANES), F32)],
        scratch_shapes=[pltpu.VMEM((tm, d), BF16)],
        compiler_params=_cparams(("arbitrary", "arbitrary"), 48),
        name="inproj_rope" if rope else "inproj",
    )(*args)


def _conv_kernel(x_ref, w_ref, b_ref, o_ref, pad_scr, *, seq, rows):
    halo = SUBLANES
    cw = x_ref.shape[1]
    pad_scr[0:halo, :] = jnp.zeros((halo, cw), F32)
    pad_scr[seq + halo:seq + 2 * halo, :] = jnp.zeros((halo, cw), F32)
    pad_scr[halo:seq + halo, :] = x_ref[...].astype(F32)
    w = w_ref[...]
    bias = b_ref[...]
    for r0 in range(0, seq, rows):
        acc = jnp.broadcast_to(bias, (rows, cw))
        for tap in range(SSD_CONV):
            start = r0 + halo - SSD_CONV // 2 + tap
            acc = acc + w[tap:tap + 1, :] * pad_scr[start:start + rows, :]
        o_ref[r0:r0 + rows, :] = _silu(acc).astype(o_ref.dtype)


def _conv(main, col0, w8, b, *, seq):
    t = main.shape[0]
    cw = 256
    rows = min(seq, 256)
    cb0 = col0 // cw
    kern = functools.partial(_conv_kernel, seq=seq, rows=rows)
    return pl.pallas_call(
        kern,
        grid=(t // seq, XBC_WIDTH // cw),
        in_specs=[pl.BlockSpec((seq, cw), lambda b_, c: (b_, cb0 + c)),
                  pl.BlockSpec((SUBLANES, cw), lambda b_, c: (0, c)),
                  pl.BlockSpec((1, cw), lambda b_, c: (0, c))],
        out_specs=pl.BlockSpec((seq, cw), lambda b_, c: (b_, c)),
        out_shape=jax.ShapeDtypeStruct((t, XBC_WIDTH), BF16),
        scratch_shapes=[pltpu.VMEM((seq + 2 * SUBLANES, cw), F32)],
        compiler_params=_cparams(("arbitrary", "arbitrary"), 40),
        name="conv",
    )(main, w8, b)


def _attn_kernel(sink_ref, q_ref, kp_ref, kc_ref, kn_ref, vp_ref, vc_ref, vn_ref,
                 kx_ref, vx_ref, o_ref, *, nb):
    qi = pl.program_id(1)
    hk = pl.program_id(2)
    blk = ATTN_BLOCK
    g = Q_PER_KV
    q = jnp.concatenate([q_ref[:, i * HEAD_DIM:(i + 1) * HEAD_DIM] for i in range(g)], axis=0)
    k_loc = jnp.concatenate([kp_ref[...], kc_ref[...], kn_ref[...]], axis=0)
    v_loc = jnp.concatenate([vp_ref[...], vc_ref[...], vn_ref[...]], axis=0)
    scale = HEAD_DIM ** -0.5
    s_loc = _dot(q, k_loc, _NT) * scale
    s_ctx = _dot(q, kx_ref[...], _NT) * scale
    r = lax.broadcasted_iota(I32, s_loc.shape, 0) & (blk - 1)
    c = lax.broadcasted_iota(I32, s_loc.shape, 1)
    lo = jnp.maximum(r, jnp.where(qi == 0, blk, 0))
    hi = jnp.minimum(r + 2 * WINDOW, jnp.where(qi == nb - 1, 2 * blk - 1, 3 * blk - 1))
    s_loc = jnp.where((c >= lo) & (c <= hi), s_loc, NEG_BIG)
    row = lax.broadcasted_iota(I32, (g * blk, 1), 0)
    sink = jnp.zeros((g * blk, 1), F32)
    for i in range(g):
        sink = jnp.where((row >= i * blk) & (row < (i + 1) * blk), sink_ref[hk * g + i], sink)
    m = jnp.maximum(jnp.maximum(jnp.max(s_loc, axis=1, keepdims=True),
                                jnp.max(s_ctx, axis=1, keepdims=True)), sink)
    p_loc = jnp.exp(s_loc - m)
    p_ctx = jnp.exp(s_ctx - m)
    den = (jnp.sum(p_loc, axis=1, keepdims=True) + jnp.sum(p_ctx, axis=1, keepdims=True)
           + jnp.exp(sink - m))
    o = _dot(p_loc.astype(BF16), v_loc) + _dot(p_ctx.astype(BF16), vx_ref[...])
    o = o / den
    for i in range(g):
        o_ref[:, i * HEAD_DIM:(i + 1) * HEAD_DIM] = o[i * blk:(i + 1) * blk, :].astype(o_ref.dtype)


def _attn(main, main_c, sink, *, batch, seq, ctx_len):
    t = main.shape[0]
    blk = ATTN_BLOCK
    nb = seq // blk
    gw = Q_PER_KV * HEAD_DIM
    kcol = COL_K // HEAD_DIM
    vcol = COL_V // HEAD_DIM
    kcol_c = XBC_WIDTH // HEAD_DIM
    vcol_c = kcol_c + N_KV_HEADS

    def nbr(col, off):
        return pl.BlockSpec(
            (blk, HEAD_DIM),
            lambda b, i, h: (b * nb + jnp.clip(i + off, 0, nb - 1), col + h))

    in_specs = [pl.BlockSpec(memory_space=pltpu.SMEM),
                pl.BlockSpec((blk, gw), lambda b, i, h: (b * nb + i, h)),
                nbr(kcol, -1), nbr(kcol, 0), nbr(kcol, 1),
                nbr(vcol, -1), nbr(vcol, 0), nbr(vcol, 1),
                pl.BlockSpec((ctx_len, HEAD_DIM), lambda b, i, h: (b, kcol_c + h)),
                pl.BlockSpec((ctx_len, HEAD_DIM), lambda b, i, h: (b, vcol_c + h))]
    return pl.pallas_call(
        functools.partial(_attn_kernel, nb=nb),
        grid=(batch, nb, N_KV_HEADS),
        in_specs=in_specs,
        out_specs=pl.BlockSpec((blk, gw), lambda b, i, h: (b * nb + i, h)),
        out_shape=jax.ShapeDtypeStruct((t, ATTN_WIDTH), BF16),
        compiler_params=_cparams(("arbitrary", "arbitrary", "arbitrary"), 40),
        name="attn",
    )(sink, main, main, main, main, main, main, main, main_c, main_c)


def _softplus(x):
    return jnp.maximum(x, 0.0) + jnp.log1p(jnp.exp(-jnp.abs(x)))


def _ssd_chunk(u_ref, dtp_ref, bias_ref, a_ref, e_ref, tri_ref, h_scr, y_ref, d):
    q = u_ref.shape[0]
    rev = d == 1
    p = SSD_HEAD_DIM
    gw = SSD_HEADS_PER_GROUP * p
    dt = _softplus(dtp_ref[...] + bias_ref[...])
    dta = dt * a_ref[...]
    tri = tri_ref[d]
    a_cum = _dot_rhs_f32(tri, dta)
    a_row = _dot_lhs_f32(dta.T, tri, _NT)
    edge = a_cum[0:1, :] if rev else a_cum[q - 1:q, :]
    ea = jnp.exp(a_cum)
    w_state = dt * jnp.exp(edge - a_cum)
    e_mat = e_ref[d]
    dt_e = _dot_lhs_f32(dt, e_mat)
    ws_e = _dot_lhs_f32(w_state, e_mat)
    x = u_ref[:, 0:SSD_WIDTH].astype(F32)
    xdd = (x * ws_e).astype(BF16)
    li = lax.broadcasted_iota(I32, (q, q), 0)
    si = lax.broadcasted_iota(I32, (q, q), 1)
    keep = (si >= li) if rev else (si <= li)
    ea_e = _dot_lhs_f32(ea, e_mat)
    edge_e = ea_e[0:1, :] if rev else ea_e[q - 1:q, :]
    if y_ref is not None:
        xd = (x * dt_e).astype(BF16)
    for g in range(SSD_GROUPS):
        b_g = u_ref[:, SSD_WIDTH + g * SSD_STATE:SSD_WIDTH + (g + 1) * SSD_STATE]
        c0 = SSD_WIDTH + SSD_GROUPS * SSD_STATE + g * SSD_STATE
        c_g = u_ref[:, c0:c0 + SSD_STATE]
        h_t = h_scr[d, g]
        if y_ref is not None:
            cb = _dot(c_g, b_g, _NT)
            ys = []
            for r in range(SSD_HEADS_PER_GROUP):
                hd = g * SSD_HEADS_PER_GROUP + r
                ln = d * SSD_HEADS + hd
                seg = a_cum[:, ln:ln + 1] - a_row[ln:ln + 1, :]
                decay = jnp.exp(jnp.where(keep, seg, NEG_BIG))
                ys.append(_dot((cb * decay).astype(BF16), xd[:, hd * p:(hd + 1) * p]))
            y_off = _dot(c_g, h_t.astype(BF16)) * ea_e[:, g * gw:(g + 1) * gw]
            y_ref[:, g * gw:(g + 1) * gw] = jnp.concatenate(ys, axis=1) + y_off
        st = _dot(b_g, xdd[:, g * gw:(g + 1) * gw], _TN)
        h_scr[d, g] = h_t * edge_e[:, g * gw:(g + 1) * gw] + st


def _ssd_kernel(*refs, emit_y):
    if emit_y:
        (uf_ref, ub_ref, df_ref, db_ref, h0_ref, bias_ref, a_ref, e_ref, tri_ref,
         yf_ref, yb_ref, h_scr) = refs
    else:
        (uf_ref, ub_ref, df_ref, db_ref, bias_ref, a_ref, e_ref, tri_ref, hout_ref, h_scr) = refs
        yf_ref = yb_ref = None
    c = pl.program_id(1)

    @pl.when(c == 0)
    def _():
        if emit_y:
            h_scr[...] = h0_ref[0]
        else:
            h_scr[...] = jnp.zeros(h_scr.shape, F32)

    _ssd_chunk(uf_ref, df_ref, bias_ref, a_ref, e_ref, tri_ref, h_scr, yf_ref, 0)
    _ssd_chunk(ub_ref, db_ref, bias_ref, a_ref, e_ref, tri_ref, h_scr, yb_ref, 1)
    if not emit_y:
        @pl.when(c == pl.num_programs(1) - 1)
        def _():
            hout_ref[0] = h_scr[...]


def _ssd(u, dtp, h0, bias, a_neg, e_mat, tri, *, batch, seq, q):
    t = u.shape[0]
    nc = seq // q
    emit_y = h0 is not None
    hshape = (N_DIRS, SSD_GROUPS, SSD_STATE, SSD_HEADS_PER_GROUP * SSD_HEAD_DIM)
    fwd = lambda b, c: (b * nc + c, 0)
    bwd = lambda b, c: (b * nc + nc - 1 - c, 0)
    const2 = lambda b, c: (0, 0)
    const3 = lambda b, c: (0, 0, 0)
    in_specs = [pl.BlockSpec((q, XBC_WIDTH), fwd), pl.BlockSpec((q, XBC_WIDTH), bwd),
                pl.BlockSpec((q, LANES), fwd), pl.BlockSpec((q, LANES), bwd)]
    args = [u, u, dtp, dtp]
    if emit_y:
        in_specs.append(pl.BlockSpec((1,) + hshape, lambda b, c: (b, 0, 0, 0, 0)))
        args.append(h0)
    in_specs += [pl.BlockSpec((1, LANES), const2), pl.BlockSpec((1, LANES), const2),
                 pl.BlockSpec((N_DIRS, LANES, SSD_WIDTH), const3),
                 pl.BlockSpec((N_DIRS, q, q), const3)]
    args += [bias, a_neg, e_mat, tri]
    if emit_y:
        out_specs = [pl.BlockSpec((q, SSD_WIDTH), fwd), pl.BlockSpec((q, SSD_WIDTH), bwd)]
        out_shape = [jax.ShapeDtypeStruct((t, SSD_WIDTH), F32)] * 2
    else:
        out_specs = pl.BlockSpec((1,) + hshape, lambda b, c: (b, 0, 0, 0, 0))
        out_shape = jax.ShapeDtypeStruct((batch,) + hshape, F32)
    return pl.pallas_call(
        functools.partial(_ssd_kernel, emit_y=emit_y),
        grid=(batch, nc),
        in_specs=in_specs,
        out_specs=out_specs,
        out_shape=out_shape,
        scratch_shapes=[pltpu.VMEM(hshape, F32)],
        compiler_params=_cparams(("arbitrary", "arbitrary"), 48),
        name="ssd_lat" if emit_y else "ssd_ctx",
    )(*args)


def _outproj_kernel(attn_ref, yf_ref, yb_ref, xs_ref, z_ref, x_ref, w_ref, skip_ref, ng_ref,
                    g1_ref, n2_ref, sc2_ref, sh2_ref, rw_ref, x1_ref, h2_ref, lg_ref):
    y = yf_ref[...] + yb_ref[...] + skip_ref[...] * xs_ref[...].astype(F32)
    y = y * _silu(z_ref[...].astype(F32))
    y = (_rms(y) * ng_ref[...]).astype(BF16)
    m = _dot(attn_ref[...], w_ref[0:ATTN_WIDTH, :]) + _dot(y, w_ref[ATTN_WIDTH:, :])
    x1 = x_ref[...] + g1_ref[0] * m
    x1_ref[...] = x1
    h2 = (_rms(x1) * n2_ref[...]) * (1.0 + sc2_ref[0]) + sh2_ref[0]
    _store_rows(h2_ref, h2, _chunks(h2.shape[1]))
    lg_ref[...] = _dot_f32(rw_ref[...], h2, _NT)


def _outproj(attn, yf, yb, u, main, x, w_out, skip_e, ssd_g, g1, n2, sc2, sh2, rw_t, *, seq):
    t, d = x.shape
    tm = TOK_TILE
    per = seq // tm
    row = lambda i: (i, 0)
    c2 = lambda i: (0, 0)
    bat = lambda i: (i // per, 0, 0)
    in_specs = [pl.BlockSpec((tm, ATTN_WIDTH), row),
                pl.BlockSpec((tm, SSD_WIDTH), row),
                pl.BlockSpec((tm, SSD_WIDTH), row),
                pl.BlockSpec((tm, SSD_WIDTH), row),
                pl.BlockSpec((tm, SSD_WIDTH), lambda i: (i, COL_Z // SSD_WIDTH)),
                pl.BlockSpec((tm, d), row),
                pl.BlockSpec((ATTN_WIDTH + SSD_WIDTH, d), c2),
                pl.BlockSpec((1, SSD_WIDTH), c2),
                pl.BlockSpec((1, SSD_WIDTH), c2),
                pl.BlockSpec((1, 1, d), bat),
                pl.BlockSpec((1, d), c2),
                pl.BlockSpec((1, 1, d), bat),
                pl.BlockSpec((1, 1, d), bat),
                pl.BlockSpec((N_EXPERTS, d), c2)]
    return pl.pallas_call(
        _outproj_kernel,
        grid=(t // tm,),
        in_specs=in_specs,
        out_specs=[pl.BlockSpec((tm, d), row), pl.BlockSpec((tm * _chunks(d), LANES), row),
                   pl.BlockSpec((N_EXPERTS, tm), lambda i: (0, i))],
        out_shape=[jax.ShapeDtypeStruct((t, d), F32),
                   jax.ShapeDtypeStruct((t * _chunks(d), LANES), U32),
                   jax.ShapeDtypeStruct((N_EXPERTS, t), F32)],
        compiler_params=_cparams(("arbitrary",), 52),
        name="outproj",
    )(attn, yf, yb, u, main, x, w_out, skip_e, ssd_g, g1, n2, sc2, sh2, rw_t)


def _first_index(hit, iota, size):
    return jnp.min(jnp.where(hit, iota, size), axis=0, keepdims=True)


def _route_kernel(lg_ref, bias_ref, su_ref, idx_ref, w_ref, rank_ref, cnt_ref, carry_scr):
    i = pl.program_id(0)
    tm = lg_ref.shape[1]

    @pl.when(i == 0)
    def _():
        carry_scr[...] = jnp.zeros(carry_scr.shape, F32)

    scores = _sigmoid(lg_ref[...])
    sel = scores + bias_ref[...][:, 0:1]
    e8 = EXPERTS_PER_GROUP
    i8 = lax.broadcasted_iota(I32, (e8, tm), 0).astype(F32)
    gs = []
    for g in range(N_EXPERT_GROUPS):
        blk = sel[g * e8:(g + 1) * e8, :]
        m1 = jnp.max(blk, axis=0, keepdims=True)
        f1 = _first_index(blk == m1, i8, e8)
        m2 = jnp.max(jnp.where(i8 == f1, NEG_BIG, blk), axis=0, keepdims=True)
        gs.append(m1 + m2)
    gsc = jnp.concatenate(gs, axis=0)
    ig = lax.broadcasted_iota(I32, gsc.shape, 0).astype(F32)
    keep = jnp.zeros(gsc.shape, F32)
    for _ in range(TOPK_GROUPS):
        m = jnp.max(gsc, axis=0, keepdims=True)
        f = _first_index(gsc == m, ig, N_EXPERT_GROUPS)
        keep = jnp.where(ig == f, 1.0, keep)
        gsc = jnp.where(ig == f, NEG_BIG, gsc)
    keep_e = jnp.concatenate(
        [jnp.broadcast_to(keep[g:g + 1, :], (e8, tm)) for g in range(N_EXPERT_GROUPS)], axis=0)
    cand = jnp.where(keep_e > 0.5, sel, NEG_BIG)
    ie = lax.broadcasted_iota(I32, cand.shape, 0).astype(F32)
    chosen = jnp.zeros(cand.shape, F32)
    idxs, svals, hits = [], [], []
    for _ in range(TOP_K):
        m = jnp.max(cand, axis=0, keepdims=True)
        f = _first_index(cand == m, ie, N_EXPERTS)
        hit = ie == f
        idxs.append(f)
        svals.append(jnp.sum(jnp.where(hit, scores, 0.0), axis=0, keepdims=True))
        hits.append(hit)
        chosen = jnp.where(hit, 1.0, chosen)
        cand = jnp.where(hit, NEG_BIG, cand)
    s_all = jnp.concatenate(svals, axis=0)
    idx_ref[...] = jnp.concatenate(idxs, axis=0).astype(I32)
    w_ref[...] = ROUTED_SCALE * s_all / jnp.sum(s_all, axis=0, keepdims=True)
    before = _dot(chosen.astype(BF16), su_ref[...]) + carry_scr[...][:, 0:1]
    ranks = [jnp.sum(jnp.where(h, before, 0.0), axis=0, keepdims=True) for h in hits]
    rank_ref[...] = jnp.concatenate(ranks, axis=0).astype(I32)
    carry_scr[...] = carry_scr[...] + jnp.sum(chosen, axis=1, keepdims=True)
    cnt_ref[...] = carry_scr[...].astype(I32)


def _route(logits_t, bias_col):
    e, t = logits_t.shape
    tm = ROUTE_TILE
    su = jnp.triu(jnp.ones((tm, tm), F32), 1).astype(BF16)
    col = lambda i: (0, i)
    return pl.pallas_call(
        _route_kernel,
        grid=(t // tm,),
        in_specs=[pl.BlockSpec((e, tm), col),
                  pl.BlockSpec((e, LANES), lambda i: (0, 0)),
                  pl.BlockSpec((tm, tm), lambda i: (0, 0))],
        out_specs=[pl.BlockSpec((TOP_K, tm), col), pl.BlockSpec((TOP_K, tm), col),
                   pl.BlockSpec((TOP_K, tm), col), pl.BlockSpec((e, LANES), lambda i: (0, 0))],
        out_shape=[jax.ShapeDtypeStruct((TOP_K, t), I32), jax.ShapeDtypeStruct((TOP_K, t), F32),
                   jax.ShapeDtypeStruct((TOP_K, t), I32), jax.ShapeDtypeStruct((e, LANES), I32)],
        scratch_shapes=[pltpu.VMEM((e, LANES), F32)],
        compiler_params=_cparams(("arbitrary",), 40),
        name="route",
    )(logits_t, bias_col, su)


def _slots_kernel(idx_ref, rank_ref, ps_ref, d_ref, dc_ref):
    tm = idx_ref.shape[1]
    ie = lax.broadcasted_iota(I32, (N_EXPERTS, tm), 0)
    ps = ps_ref[...][:, 0:1]
    rows = []
    for k in range(TOP_K):
        hit = ie == idx_ref[k:k + 1, :]
        rows.append(jnp.sum(jnp.where(hit, ps, 0.0), axis=0, keepdims=True))
    dest = jnp.concatenate(rows, axis=0).astype(I32) + rank_ref[...]
    d_ref[0] = dest
    tc = dc_ref.shape[2]
    for s in range(tm // tc):
        dc_ref[s] = dest[:, s * tc:(s + 1) * tc]


def _slots(idx_t, rank_t, ps_col):
    t = idx_t.shape[1]
    tm = TOK_TILE
    tc = COMB_TILE
    col = lambda i: (0, i)
    return pl.pallas_call(
        _slots_kernel,
        grid=(t // tm,),
        in_specs=[pl.BlockSpec((TOP_K, tm), col), pl.BlockSpec((TOP_K, tm), col),
                  pl.BlockSpec((N_EXPERTS, LANES), lambda i: (0, 0))],
        out_specs=[pl.BlockSpec((1, TOP_K, tm), lambda i: (i, 0, 0)),
                   pl.BlockSpec((tm // tc, TOP_K, tc), lambda i: (i, 0, 0))],
        out_shape=[jax.ShapeDtypeStruct((t // tm, TOP_K, tm), I32),
                   jax.ShapeDtypeStruct((t // tc, TOP_K, tc), I32)],
        compiler_params=_cparams(("arbitrary",), 40),
        name="slots",
    )(idx_t, rank_t, ps_col)


def _row_copy(src, dst, sem):
    return pltpu.make_async_copy(src, dst, sem)


def _dispatch_kernel(pend_ref, dest_ref, h_ref, xs_ref, zero_scr, sem, *, nch):
    i = pl.program_id(0)
    tm = h_ref.shape[0] // nch
    tile_rows = zero_scr.shape[0]

    @pl.when(i == 0)
    def _():
        zero_scr[...] = jnp.zeros(zero_scr.shape, zero_scr.dtype)

        def pad_copy(e):
            start_row = pl.multiple_of(pend_ref[e + 1] * nch - tile_rows, tile_rows)
            return _row_copy(zero_scr, xs_ref.at[pl.ds(start_row, tile_rows), :], sem.at[0])

        def start(e, carry):
            @pl.when(pend_ref[e + 1] > pend_ref[e])
            def _():
                pad_copy(e).start()
            return carry

        def wait(e, carry):
            @pl.when(pend_ref[e + 1] > pend_ref[e])
            def _():
                pad_copy(e).wait()
            return carry

        lax.fori_loop(0, N_EXPERTS, start, 0)
        lax.fori_loop(0, N_EXPERTS, wait, 0)

    def tok_copy(t, k):
        src = pl.multiple_of(t * nch, nch)
        dst = pl.multiple_of(dest_ref[0, k, t] * nch, nch)
        return _row_copy(h_ref.at[pl.ds(src, nch), :], xs_ref.at[pl.ds(dst, nch), :], sem.at[1])

    def start(t, carry):
        for k in range(TOP_K):
            tok_copy(t, k).start(priority=k % 2)
        return carry

    def wait(t, carry):
        for k in range(TOP_K):
            tok_copy(t, k).wait()
        return carry

    lax.fori_loop(0, tm, start, 0)
    lax.fori_loop(0, tm, wait, 0)


def _dispatch(pad_end_ext, dest3, h2, n_slots, nch):
    t = h2.shape[0] // nch
    tm = TOK_TILE
    grid_spec = pltpu.PrefetchScalarGridSpec(
        num_scalar_prefetch=1,
        grid=(t // tm,),
        in_specs=[pl.BlockSpec((1, TOP_K, tm), lambda i, pe: (i, 0, 0), memory_space=pltpu.SMEM),
                  pl.BlockSpec((tm * nch, LANES), lambda i, pe: (i, 0))],
        out_specs=pl.BlockSpec(memory_space=pl.ANY),
        scratch_shapes=[pltpu.VMEM((MOE_TILE * nch, LANES), U32), pltpu.SemaphoreType.DMA((2,))],
    )
    return pl.pallas_call(
        functools.partial(_dispatch_kernel, nch=nch),
        grid_spec=grid_spec,
        out_shape=jax.ShapeDtypeStruct((n_slots * nch, LANES), U32),
        compiler_params=_cparams(("arbitrary",), 40),
        name="dispatch",
    )(pad_end_ext, dest3, h2)


def _experts_kernel(first_ref, ntile_ref, xs_ref, wg_ref, wu_ref, wd_ref, ys_ref,
                    wg_scr, wu_scr, wd_scr, xbuf, ybuf, sem):
    e = pl.program_id(0)
    n = ntile_ref[e]
    first = first_ref[e]
    rows = xbuf.shape[1]
    nch = _chunks(wg_scr.shape[0])
    te = rows // nch

    def x_copy(j, slot):
        r0 = pl.multiple_of((first + j) * rows, rows)
        return _row_copy(xs_ref.at[pl.ds(r0, rows), :], xbuf.at[slot], sem.at[0, slot])

    def y_copy(j, slot):
        r0 = pl.multiple_of((first + j) * rows, rows)
        return _row_copy(ybuf.at[slot], ys_ref.at[pl.ds(r0, rows), :], sem.at[1, slot])

    @pl.when(n > 0)
    def _():
        x_copy(0, 0).start()
        wg_scr[...] = wg_ref[0].astype(BF16)
        wu_scr[...] = wu_ref[0].astype(BF16)
        wd_scr[...] = wd_ref[0].astype(BF16)

        def tile(j, carry):
            slot = j % 2

            @pl.when(j + 1 < n)
            def _():
                x_copy(j + 1, 1 - slot).start()

            x_copy(j, slot).wait()

            @pl.when(j >= 2)
            def _():
                y_copy(j - 2, slot).wait()

            xb = _load_rows(xbuf.at[slot], te, nch).astype(BF16)
            act = _silu(_dot(xb, wg_scr[...])) * _dot(xb, wu_scr[...])
            _store_rows(ybuf.at[slot], _dot(act.astype(BF16), wd_scr[...]), nch)
            y_copy(j, slot).start()
            return carry

        lax.fori_loop(0, n, tile, 0)

        @pl.when(n >= 2)
        def _():
            y_copy(n - 2, n % 2).wait()

        y_copy(n - 1, (n - 1) % 2).wait()


def _experts(first_tile, n_tile, xs, wg, wu, wd):
    n_exp, d, f = wg.shape
    nch = _chunks(d)
    rows = MOE_TILE * nch
    grid_spec = pltpu.PrefetchScalarGridSpec(
        num_scalar_prefetch=2,
        grid=(n_exp,),
        in_specs=[pl.BlockSpec(memory_space=pl.ANY),
                  pl.BlockSpec((1, d, f), lambda e, ft, nt: (e, 0, 0)),
                  pl.BlockSpec((1, d, f), lambda e, ft, nt: (e, 0, 0)),
                  pl.BlockSpec((1, f, d), lambda e, ft, nt: (e, 0, 0))],
        out_specs=pl.BlockSpec(memory_space=pl.ANY),
        scratch_shapes=[pltpu.VMEM((d, f), BF16), pltpu.VMEM((d, f), BF16), pltpu.VMEM((f, d), BF16),
                        pltpu.VMEM((2, rows, LANES), U32), pltpu.VMEM((2, rows, LANES), U32),
                        pltpu.SemaphoreType.DMA((2, 2))],
    )
    return pl.pallas_call(
        _experts_kernel,
        grid_spec=grid_spec,
        out_shape=jax.ShapeDtypeStruct(xs.shape, U32),
        compiler_params=_cparams(("arbitrary",), 56),
        name="experts",
    )(first_tile, n_tile, xs, wg, wu, wd)


def _combine_kernel(dest_ref, dnext_ref, ys_ref, wt_ref, h_ref, x1_ref, sg_ref, su_ref, sd_ref,
                    g2_ref, fg_ref, eye_ref, o_ref, buf, sem):
    i = pl.program_id(0)
    n = pl.num_programs(0)
    tm = x1_ref.shape[0]
    nch = _chunks(x1_ref.shape[1])
    slot = i % 2

    def row_copy(d_ref, s, t, k):
        src = pl.multiple_of(d_ref[0, k, t] * nch, nch)
        dst = pl.multiple_of(t * nch, nch)
        return _row_copy(ys_ref.at[pl.ds(src, nch), :], buf.at[s, k, pl.ds(dst, nch), :], sem.at[s])

    def start_tile(d_ref, s):
        def body(t, carry):
            for k in range(TOP_K):
                row_copy(d_ref, s, t, k).start(priority=k % 2)
            return carry
        lax.fori_loop(0, tm, body, 0)

    @pl.when(i == 0)
    def _():
        start_tile(dest_ref, 0)

    @pl.when(i + 1 < n)
    def _():
        start_tile(dnext_ref, 1 - slot)

    hb = _load_rows(h_ref, tm, nch).astype(BF16)
    act = _silu(_dot(hb, sg_ref[...])) * _dot(hb, su_ref[...])
    ffn = _dot(act.astype(BF16), sd_ref[...])
    w_col = _dot_rhs_f32(eye_ref[...], wt_ref[...], _NT)

    def wait(t, carry):
        for k in range(TOP_K):
            row_copy(dest_ref, slot, t, k).wait()
        return carry

    lax.fori_loop(0, tm, wait, 0)
    for k in range(TOP_K):
        ffn = ffn + w_col[:, k:k + 1] * _load_rows(buf.at[slot, k], tm, nch)
    x2 = x1_ref[...] + g2_ref[0] * ffn
    o_ref[...] = _rms(x2) * fg_ref[...]


def _combine(dest3, ys, w_t, h2, x1, sg, su, sd, g2, fg, *, seq):
    t, d = x1.shape
    nch = _chunks(d)
    tm = COMB_TILE
    per = seq // tm
    f = sg.shape[1]
    eye = jnp.eye(tm, dtype=BF16)
    row = lambda i: (i, 0)
    c2 = lambda i: (0, 0)
    last = t // tm - 1
    return pl.pallas_call(
        _combine_kernel,
        grid=(t // tm,),
        in_specs=[pl.BlockSpec((1, TOP_K, tm), lambda i: (i, 0, 0), memory_space=pltpu.SMEM),
                  pl.BlockSpec((1, TOP_K, tm), lambda i: (jnp.minimum(i + 1, last), 0, 0),
                               memory_space=pltpu.SMEM),
                  pl.BlockSpec(memory_space=pl.ANY),
                  pl.BlockSpec((TOP_K, tm), lambda i: (0, i)),
                  pl.BlockSpec((tm * nch, LANES), row),
                  pl.BlockSpec((tm, d), row),
                  pl.BlockSpec((d, f), c2), pl.BlockSpec((d, f), c2), pl.BlockSpec((f, d), c2),
                  pl.BlockSpec((1, 1, d), lambda i: (i // per, 0, 0)),
                  pl.BlockSpec((1, d), c2),
                  pl.BlockSpec((tm, tm), c2)],
        out_specs=pl.BlockSpec((tm, d), row),
        out_shape=jax.ShapeDtypeStruct((t, d), F32),
        scratch_shapes=[pltpu.VMEM((2, TOP_K, tm * nch, LANES), U32),
                        pltpu.SemaphoreType.DMA((2,))],
        compiler_params=_cparams(("arbitrary",), 48),
        name="combine",
    )(dest3, dest3, ys, w_t, h2, x1, sg, su, sd, g2, fg, eye)


def _rope_tables(seq):
    n_rows = seq // GRID_W
    rows = jnp.repeat(jnp.arange(n_rows, dtype=F32), GRID_W)
    cols = jnp.tile(jnp.arange(GRID_W, dtype=F32), n_rows)
    inv = ROPE_THETA ** (-jnp.arange(0, ROPE_AXIS_DIM, 2, dtype=F32) / ROPE_AXIS_DIM)
    ar = rows[:, None] * inv[None, :]
    ac = cols[:, None] * inv[None, :]
    cos = jnp.concatenate([jnp.cos(ar), jnp.cos(ac), jnp.cos(ar), jnp.cos(ac)], axis=1)
    sin = jnp.concatenate([-jnp.sin(ar), -jnp.sin(ac), jnp.sin(ar), jnp.sin(ac)], axis=1)
    return cos, sin


def _rope_head_order(n_heads):
    q = ROPE_AXIS_DIM // 2
    one = np.concatenate([np.arange(0, q), np.arange(2 * q, 3 * q), np.arange(q, 2 * q),
                          np.arange(3 * q, 4 * q)])
    return np.concatenate([h * HEAD_DIM + one for h in range(n_heads)])


def _layer(x, c, ctx, c_ctx, w_ada, b_ada, norm1_g, norm2_g, w_in, attn_sink, conv_w, conv_b,
           dt_bias, a_log, d_skip, ssd_norm_g, w_out, router_w, router_bias, wg, wu, wd,
           sw_gate, sw_up, sw_down, final_norm_g):
    batch, seq, d = x.shape
    ctx_len = ctx.shape[1]
    t = batch * seq

    cvec = jnp.zeros((SUBLANES, d), F32).at[:batch].set(c).at[batch].set(c_ctx)
    mod = _ada(cvec, w_ada, b_ada[None, :])
    sh1, sc1, g1, sh2, sc2, g2 = [mod[:batch, i * d:(i + 1) * d][:, None, :] for i in range(6)]
    sh1c, sc1c = [jnp.broadcast_to(mod[batch, i * d:(i + 1) * d][None, None, :], (batch, 1, d))
                  for i in range(2)]

    o_q, o_k, o_v, o_z, o_x, o_dt = np.cumsum((0, ATTN_WIDTH, KV_WIDTH, KV_WIDTH, SSD_WIDTH,
                                               XBC_WIDTH)).tolist()
    w_q = w_in[:, o_q:o_k][:, _rope_head_order(N_Q_HEADS)]
    w_k = w_in[:, o_k:o_v][:, _rope_head_order(N_KV_HEADS)]
    w_main = jnp.concatenate([w_q, w_in[:, o_z:o_x], w_in[:, o_x:o_dt],
                              w_k, w_in[:, o_v:o_z]], axis=1).astype(BF16)
    w_dt = jnp.pad(w_in[:, o_dt:], ((0, 0), (0, LANES - N_DIRS * SSD_HEADS))).astype(BF16)
    cos, sin = _rope_tables(seq)
    n1 = norm1_g[None, :]
    main, dtp = _inproj(x.reshape(t, d), n1, sc1, sh1, w_main, w_dt, cos, sin,
                        seq=seq, tm=min(seq, 1024), rope=True)
    main_c, dtp_c = _inproj(ctx.reshape(batch * ctx_len, d), n1, sc1c, sh1c,
                            w_main[:, COL_XBC:], w_dt, None, None,
                            seq=ctx_len, tm=ctx_len, rope=False)

    attn = _attn(main, main_c, attn_sink, batch=batch, seq=seq, ctx_len=ctx_len)

    cw8 = jnp.pad(conv_w, ((0, SUBLANES - SSD_CONV), (0, 0)))
    u = _conv(main, COL_XBC, cw8, conv_b[None, :], seq=seq)
    u_c = _conv(main_c, 0, cw8, conv_b[None, :], seq=ctx_len)
    nd = N_DIRS * SSD_HEADS
    bias = jnp.pad(dt_bias.reshape(1, nd), ((0, 0), (0, LANES - nd)))
    a_neg = jnp.pad(-jnp.exp(a_log.reshape(1, nd)), ((0, 0), (0, LANES - nd)))
    lane_head = np.arange(SSD_WIDTH) // SSD_HEAD_DIM
    e_np = np.zeros((N_DIRS, LANES, SSD_WIDTH), np.float32)
    for dd in range(N_DIRS):
        e_np[dd, dd * SSD_HEADS + lane_head, np.arange(SSD_WIDTH)] = 1.0
    e_mat = jnp.asarray(e_np, BF16)

    def tri(q):
        lo = np.tril(np.ones((q, q), np.float32))
        return jnp.asarray(np.stack([lo, lo.T]), BF16)

    qc = min(SSD_CHUNK, ctx_len)
    h_ctx = _ssd(u_c, dtp_c, None, bias, a_neg, e_mat, tri(qc), batch=batch, seq=ctx_len, q=qc)
    ql = min(SSD_CHUNK, seq)
    yf, yb = _ssd(u, dtp, h_ctx, bias, a_neg, e_mat, tri(ql), batch=batch, seq=seq, q=ql)

    skip_e = jnp.repeat(d_skip, SSD_HEAD_DIM)[None, :]
    x1, h2, logits_t = _outproj(attn, yf, yb, u, main, x.reshape(t, d), w_out.astype(BF16),
                                skip_e, ssd_norm_g[None, :], g1, norm2_g[None, :], sc2, sh2,
                                router_w.T, seq=seq)

    bias_col = jnp.broadcast_to(router_bias[:, None], (N_EXPERTS, LANES))
    idx_t, w_t, rank_t, cnt = _route(logits_t, bias_col)
    counts = cnt[:, 0]
    te = MOE_TILE
    padded = (counts + te - 1) // te * te
    pad_end = jnp.cumsum(padded)
    pad_start = pad_end - padded
    n_slots = -(-(t * TOP_K + N_EXPERTS * (te - 1)) // te) * te
    ps_col = jnp.broadcast_to(pad_start.astype(F32)[:, None], (N_EXPERTS, LANES))
    dest3, dest3c = _slots(idx_t, rank_t, ps_col)
    pad_end_ext = jnp.concatenate([jnp.zeros((1,), I32), pad_end.astype(I32)])

    xs = _dispatch(pad_end_ext, dest3, h2, n_slots, _chunks(d))
    ys = _experts((pad_start // te).astype(I32), (padded // te).astype(I32), xs, wg, wu, wd)
    out = _combine(dest3c, ys, w_t, h2, x1, sw_gate.astype(BF16), sw_up.astype(BF16),
                   sw_down.astype(BF16), g2, final_norm_g[None, :], seq=seq)
    return out.reshape(batch, seq, d)


def kernel(x, c, ctx, c_ctx, w_ada, b_ada, norm1_g, norm2_g, w_in, attn_sink, conv_w, conv_b, dt_bias, a_log, d_skip, ssd_norm_g, w_out, router_w, router_bias, expert_w_gate, expert_w_up, expert_w_down, shared_w_gate, shared_w_up, shared_w_down, final_norm_g):
    assert w_ada.shape[0] == 1, "single-layer stack only"
    return _layer(x, c, ctx, c_ctx, w_ada[0], b_ada[0], norm1_g[0], norm2_g[0], w_in[0],
                  attn_sink[0], conv_w[0], conv_b[0], dt_bias[0], a_log[0], d_skip[0],
                  ssd_norm_g[0], w_out[0], router_w[0], router_bias[0], expert_w_gate[0],
                  expert_w_up[0], expert_w_down[0], shared_w_gate[0], shared_w_up[0],
                  shared_w_down[0], final_norm_g)
```

```python
import functools
import math

import jax
import jax.numpy as jnp
import numpy as np
from jax import lax
from jax.experimental import pallas as pl
from jax.experimental.pallas import tpu as pltpu

F32 = jnp.float32
BF16 = jnp.bfloat16
I32 = jnp.int32

EPS = 1e-6
GRID_W = 64
N_Q_HEADS = 8
N_KV_HEADS = 2
Q_PER_KV = N_Q_HEADS // N_KV_HEADS
HEAD_DIM = 128
ATTN_WIDTH = N_Q_HEADS * HEAD_DIM
KV_WIDTH = N_KV_HEADS * HEAD_DIM
WINDOW = 128
ATTN_BLOCK = 128
ROPE_THETA = 10000.0
ROPE_AXIS_DIM = HEAD_DIM // 2
SSD_HEADS = 16
SSD_HEAD_DIM = 64
SSD_WIDTH = SSD_HEADS * SSD_HEAD_DIM
SSD_GROUPS = 4
SSD_HEADS_PER_GROUP = SSD_HEADS // SSD_GROUPS
SSD_STATE = 128
SSD_CONV = 5
N_DIRS = 2
XBC_WIDTH = SSD_WIDTH + 2 * SSD_GROUPS * SSD_STATE
N_EXPERTS = 64
N_EXPERT_GROUPS = 8
EXPERTS_PER_GROUP = N_EXPERTS // N_EXPERT_GROUPS
TOPK_GROUPS = 4
TOP_K = 8
EXPERT_DIM = 512
ROUTED_SCALE = 2.5

LANES = 128
SUBLANES = 8
NEG_BIG = -1e30

COL_Q = 0
COL_Z = ATTN_WIDTH
COL_XBC = COL_Z + SSD_WIDTH
COL_K = COL_XBC + XBC_WIDTH
COL_V = COL_K + KV_WIDTH
MAIN_WIDTH = COL_V + KV_WIDTH

IN_TILE_N = 512
SSD_CHUNK = 256
MOE_TILE = 256
ROUTE_TILE = 512
TOK_TILE = 256
COMB_TILE = 128
TILE_DMA_PRIORITY = 1
WEIGHT_CHUNKS = 8

_NT = (((1,), (1,)), ((), ()))
_TN = (((0,), (0,)), ((), ()))


def _cparams(sem, vmem_mb):
    return pltpu.CompilerParams(dimension_semantics=sem, vmem_limit_bytes=vmem_mb * 1024 * 1024)


def _dot(a, b, dims=None):
    if dims is None:
        return jnp.dot(a, b, preferred_element_type=F32)
    return lax.dot_general(a, b, dims, preferred_element_type=F32)


def _split(a):
    hi = a.astype(BF16)
    lo = (a - hi.astype(F32)).astype(BF16)
    return hi, lo


def _dot_lhs_f32(a, b_exact, dims=None):
    hi, lo = _split(a)
    return _dot(hi, b_exact, dims) + _dot(lo, b_exact, dims)


def _dot_rhs_f32(a_exact, b, dims=None):
    hi, lo = _split(b)
    return _dot(a_exact, hi, dims) + _dot(a_exact, lo, dims)


def _dot_f32(a, b, dims=None):
    ah, al = _split(a)
    bh, bl = _split(b)
    return _dot(ah, bh, dims) + (_dot(al, bh, dims) + _dot(ah, bl, dims))


def _sigmoid(x):
    return 1.0 / (1.0 + jnp.exp(-x))


def _silu(x):
    return x * _sigmoid(x)


def _rms(x):
    return x * lax.rsqrt(jnp.mean(x * x, axis=-1, keepdims=True) + EPS)


U32 = jnp.uint32
_HI_MASK = 0xFFFF0000


def _chunks(d):
    return d // (2 * LANES)


def _load_rows(ref, rows, nch):
    words = [ref[pl.ds(j, rows, stride=nch), :] for j in range(nch)]
    lo = [lax.bitcast_convert_type(w << 16, F32) for w in words]
    hi = [lax.bitcast_convert_type(w & jnp.uint32(_HI_MASK), F32) for w in words]
    return jnp.concatenate(lo + hi, axis=1)


def _store_rows(ref, val, nch):
    rows, d = val.shape
    bits = lax.bitcast_convert_type(val.astype(BF16).astype(F32), U32)
    for j in range(nch):
        lo = bits[:, j * LANES:(j + 1) * LANES] >> 16
        hi = bits[:, d // 2 + j * LANES:d // 2 + (j + 1) * LANES] & jnp.uint32(_HI_MASK)
        ref[pl.ds(j, rows, stride=nch), :] = lo | hi


def _ada_kernel(c_ref, w_ref, b_ref, o_ref):
    o_ref[...] = _dot_f32(_silu(c_ref[...]), w_ref[...]) + b_ref[...]


def _ada(cvec, w, b):
    m, d = cvec.shape
    n = w.shape[1]
    tn = 1024
    return pl.pallas_call(
        _ada_kernel,
        grid=(n // tn,),
        in_specs=[pl.BlockSpec((m, d), lambda j: (0, 0)),
                  pl.BlockSpec((d, tn), lambda j: (0, j)),
                  pl.BlockSpec((1, tn), lambda j: (0, j))],
        out_specs=pl.BlockSpec((m, tn), lambda j: (0, j)),
        out_shape=jax.ShapeDtypeStruct((m, n), F32),
        compiler_params=_cparams(("arbitrary",), 40),
        name="ada",
    )(cvec, w, b)


def _rope_mix(acc, cos, sin):
    heads = acc.shape[1] // HEAD_DIM
    partner = jnp.concatenate(
        [pltpu.roll(acc[:, h * HEAD_DIM:(h + 1) * HEAD_DIM], HEAD_DIM // 2, 1) for h in range(heads)],
        axis=1)
    return acc * cos + partner * sin


def _inproj_kernel(*refs, rope, n_rope_tiles, kv_tile):
    if rope:
        x_ref, g_ref, sc_ref, sh_ref, w_ref, wdt_ref, cos_ref, sin_ref, o_ref, dt_ref, h_scr = refs
    else:
        x_ref, g_ref, sc_ref, sh_ref, w_ref, wdt_ref, o_ref, dt_ref, h_scr = refs
    j = pl.program_id(1)

    @pl.when(j == 0)
    def _():
        h = (_rms(x_ref[...]) * g_ref[...]) * (1.0 + sc_ref[0]) + sh_ref[0]
        hb = h.astype(BF16)
        h_scr[...] = hb
        dt_ref[...] = _dot(hb, wdt_ref[...])

    acc = _dot(h_scr[...], w_ref[0])
    if not rope:
        o_ref[...] = acc.astype(o_ref.dtype)
        return
    reps = acc.shape[1] // HEAD_DIM

    @pl.when(j < n_rope_tiles)
    def _():
        cos = jnp.concatenate([cos_ref[...]] * reps, axis=1)
        sin = jnp.concatenate([sin_ref[...]] * reps, axis=1)
        o_ref[...] = _rope_mix(acc, cos, sin).astype(o_ref.dtype)

    @pl.when(j == kv_tile)
    def _():
        nk = KV_WIDTH // HEAD_DIM
        ones = jnp.ones_like(cos_ref[...])
        cos = jnp.concatenate([cos_ref[...]] * nk + [ones] * (reps - nk), axis=1)
        sin = jnp.concatenate([sin_ref[...]] * nk + [ones * 0.0] * (reps - nk), axis=1)
        o_ref[...] = _rope_mix(acc, cos, sin).astype(o_ref.dtype)

    @pl.when((j >= n_rope_tiles) & (j != kv_tile))
    def _():
        o_ref[...] = acc.astype(o_ref.dtype)


def _inproj(x, g, sc, sh, w, wdt, cos, sin, *, seq, tm, rope):
    t, d = x.shape
    wn = w.shape[1]
    tn = IN_TILE_N
    per = seq // tm
    w = w.reshape(d, wn // tn, tn).transpose(1, 0, 2)
    in_specs = [pl.BlockSpec((tm, d), lambda i, j: (i, 0)),
                pl.BlockSpec((1, d), lambda i, j: (0, 0)),
                pl.BlockSpec((1, 1, d), lambda i, j: (i // per, 0, 0)),
                pl.BlockSpec((1, 1, d), lambda i, j: (i // per, 0, 0)),
                pl.BlockSpec((1, d, tn), lambda i, j: (j, 0, 0)),
                pl.BlockSpec((d, LANES), lambda i, j: (0, 0))]
    args = [x, g, sc, sh, w, wdt]
    if rope:
        in_specs += [pl.BlockSpec((tm, HEAD_DIM), lambda i, j: (i % per, 0)),
                     pl.BlockSpec((tm, HEAD_DIM), lambda i, j: (i % per, 0))]
        args += [cos, sin]
    kern = functools.partial(_inproj_kernel, rope=rope, n_rope_tiles=ATTN_WIDTH // tn,
                             kv_tile=COL_K // tn)
    return pl.pallas_call(
        kern,
        grid=(t // tm, wn // tn),
        in_specs=in_specs,
        out_specs=[pl.BlockSpec((tm, tn), lambda i, j: (i, j)),
                   pl.BlockSpec((tm, LANES), lambda i, j: (i, 0))],
        out_shape=[jax.ShapeDtypeStruct((t, wn), BF16), jax.ShapeDtypeStruct((t, LANES), F32)],
        scratch_shapes=[pltpu.VMEM((tm, d), BF16)],
        compiler_params=_cparams(("arbitrary", "arbitrary"), 48),
        name="inproj_rope" if rope else "inproj",
    )(*args)


def _conv_kernel(x_ref, w_ref, b_ref, o_ref, pad_scr, *, seq, rows):
    halo = SUBLANES
    cw = x_ref.shape[1]
    pad_scr[0:halo, :] = jnp.zeros((halo, cw), F32)
    pad_scr[seq + halo:seq + 2 * halo, :] = jnp.zeros((halo, cw), F32)
    pad_scr[halo:seq + halo, :] = x_ref[...].astype(F32)
    w = w_ref[...]
    bias = b_ref[...]
    for r0 in range(0, seq, rows):
        acc = jnp.broadcast_to(bias, (rows, cw))
        for tap in range(SSD_CONV):
            start = r0 + halo - SSD_CONV // 2 + tap
            acc = acc + w[tap:tap + 1, :] * pad_scr[start:start + rows, :]
        o_ref[r0:r0 + rows, :] = _silu(acc).astype(o_ref.dtype)


def _conv(main, col0, w8, b, *, seq):
    t = main.shape[0]
    cw = 256
    rows = min(seq, 256)
    cb0 = col0 // cw
    kern = functools.partial(_conv_kernel, seq=seq, rows=rows)
    return pl.pallas_call(
        kern,
        grid=(t // seq, XBC_WIDTH // cw),
        in_specs=[pl.BlockSpec((seq, cw), lambda b_, c: (b_, cb0 + c)),
                  pl.BlockSpec((SUBLANES, cw), lambda b_, c: (0, c)),
                  pl.BlockSpec((1, cw), lambda b_, c: (0, c))],
        out_specs=pl.BlockSpec((seq, cw), lambda b_, c: (b_, c)),
        out_shape=jax.ShapeDtypeStruct((t, XBC_WIDTH), BF16),
        scratch_shapes=[pltpu.VMEM((seq + 2 * SUBLANES, cw), F32)],
        compiler_params=_cparams(("arbitrary", "arbitrary"), 40),
        name="conv",
    )(main, w8, b)


def _attn_kernel(sink_ref, q_ref, kp_ref, kc_ref, kn_ref, vp_ref, vc_ref, vn_ref,
                 kx_ref, vx_ref, o_ref, *, nb):
    qi = pl.program_id(1)
    blk = ATTN_BLOCK
    g = Q_PER_KV
    hd = HEAD_DIM
    scale = hd ** -0.5
    r = lax.broadcasted_iota(I32, (g * blk, 3 * blk), 0) & (blk - 1)
    c = lax.broadcasted_iota(I32, (g * blk, 3 * blk), 1)
    lo = jnp.maximum(r, jnp.where(qi == 0, blk, 0))
    hi = jnp.minimum(r + 2 * WINDOW, jnp.where(qi == nb - 1, 2 * blk - 1, 3 * blk - 1))
    valid = (c >= lo) & (c <= hi)
    row = lax.broadcasted_iota(I32, (g * blk, 1), 0)
    for hk in range(N_KV_HEADS):
        ks = slice(hk * hd, (hk + 1) * hd)
        q = jnp.concatenate([q_ref[:, (hk * g + i) * hd:(hk * g + i + 1) * hd] for i in range(g)],
                            axis=0)
        k_loc = jnp.concatenate([kp_ref[:, ks], kc_ref[:, ks], kn_ref[:, ks]], axis=0)
        v_loc = jnp.concatenate([vp_ref[:, ks], vc_ref[:, ks], vn_ref[:, ks]], axis=0)
        s_loc = _dot(q, k_loc, _NT) * scale
        s_ctx = _dot(q, kx_ref[:, ks], _NT) * scale
        s_loc = jnp.where(valid, s_loc, NEG_BIG)
        sink = jnp.zeros((g * blk, 1), F32)
        for i in range(g):
            sink = jnp.where((row >= i * blk) & (row < (i + 1) * blk), sink_ref[hk * g + i], sink)
        m = jnp.maximum(jnp.maximum(jnp.max(s_loc, axis=1, keepdims=True),
                                    jnp.max(s_ctx, axis=1, keepdims=True)), sink)
        p_loc = jnp.exp(s_loc - m)
        p_ctx = jnp.exp(s_ctx - m)
        den = (jnp.sum(p_loc, axis=1, keepdims=True) + jnp.sum(p_ctx, axis=1, keepdims=True)
               + jnp.exp(sink - m))
        o = _dot(p_loc.astype(BF16), v_loc) + _dot(p_ctx.astype(BF16), vx_ref[:, ks])
        o = o / den
        for i in range(g):
            o_ref[:, (hk * g + i) * hd:(hk * g + i + 1) * hd] = (
                o[i * blk:(i + 1) * blk, :].astype(o_ref.dtype))


def _attn(main, main_c, sink, *, batch, seq, ctx_len):
    t = main.shape[0]
    blk = ATTN_BLOCK
    nb = seq // blk
    kcol = COL_K // KV_WIDTH
    vcol = COL_V // KV_WIDTH
    kcol_c = XBC_WIDTH // KV_WIDTH
    vcol_c = kcol_c + 1

    def nbr(col, off):
        return pl.BlockSpec(
            (blk, KV_WIDTH), lambda b, i: (b * nb + jnp.clip(i + off, 0, nb - 1), col))

    in_specs = [pl.BlockSpec(memory_space=pltpu.SMEM),
                pl.BlockSpec((blk, ATTN_WIDTH), lambda b, i: (b * nb + i, 0)),
                nbr(kcol, -1), nbr(kcol, 0), nbr(kcol, 1),
                nbr(vcol, -1), nbr(vcol, 0), nbr(vcol, 1),
                pl.BlockSpec((ctx_len, KV_WIDTH), lambda b, i: (b, kcol_c)),
                pl.BlockSpec((ctx_len, KV_WIDTH), lambda b, i: (b, vcol_c))]
    return pl.pallas_call(
        functools.partial(_attn_kernel, nb=nb),
        grid=(batch, nb),
        in_specs=in_specs,
        out_specs=pl.BlockSpec((blk, ATTN_WIDTH), lambda b, i: (b * nb + i, 0)),
        out_shape=jax.ShapeDtypeStruct((t, ATTN_WIDTH), BF16),
        compiler_params=_cparams(("arbitrary", "arbitrary"), 40),
        name="attn",
    )(sink, main, main, main, main, main, main, main, main_c, main_c)


def _softplus(x):
    return jnp.maximum(x, 0.0) + jnp.log1p(jnp.exp(-jnp.abs(x)))


def _ssd_chunk(u_ref, dtp_ref, bias_ref, a_ref, e_ref, tri_ref, h_scr, y_ref, d):
    q = u_ref.shape[0]
    rev = d == 1
    p = SSD_HEAD_DIM
    gw = SSD_HEADS_PER_GROUP * p
    dt = _softplus(dtp_ref[...] + bias_ref[...])
    dta = dt * a_ref[...]
    tri = tri_ref[d]
    a_cum = _dot_rhs_f32(tri, dta)
    a_row = _dot_lhs_f32(dta.T, tri, _NT)
    edge = a_cum[0:1, :] if rev else a_cum[q - 1:q, :]
    ea = jnp.exp(a_cum)
    w_state = dt * jnp.exp(edge - a_cum)
    e_mat = e_ref[d]
    dt_e = _dot_lhs_f32(dt, e_mat)
    ws_e = _dot_lhs_f32(w_state, e_mat)
    x = u_ref[:, 0:SSD_WIDTH].astype(F32)
    xdd = (x * ws_e).astype(BF16)
    li = lax.broadcasted_iota(I32, (q, q), 0)
    si = lax.broadcasted_iota(I32, (q, q), 1)
    keep = (si >= li) if rev else (si <= li)
    ea_e = _dot_lhs_f32(ea, e_mat)
    edge_e = ea_e[0:1, :] if rev else ea_e[q - 1:q, :]
    if y_ref is not None:
        xd = (x * dt_e).astype(BF16)
    for g in range(SSD_GROUPS):
        b_g = u_ref[:, SSD_WIDTH + g * SSD_STATE:SSD_WIDTH + (g + 1) * SSD_STATE]
        c0 = SSD_WIDTH + SSD_GROUPS * SSD_STATE + g * SSD_STATE
        c_g = u_ref[:, c0:c0 + SSD_STATE]
        h_t = h_scr[d, g]
        if y_ref is not None:
            cb = _dot(c_g, b_g, _NT)
            ys = []
            for r in range(SSD_HEADS_PER_GROUP):
                hd = g * SSD_HEADS_PER_GROUP + r
                ln = d * SSD_HEADS + hd
                seg = a_cum[:, ln:ln + 1] - a_row[ln:ln + 1, :]
                decay = jnp.exp(jnp.where(keep, seg, NEG_BIG))
                ys.append(_dot((cb * decay).astype(BF16), xd[:, hd * p:(hd + 1) * p]))
            y_off = _dot(c_g, h_t.astype(BF16)) * ea_e[:, g * gw:(g + 1) * gw]
            y_ref[:, g * gw:(g + 1) * gw] = jnp.concatenate(ys, axis=1) + y_off
        st = _dot(b_g, xdd[:, g * gw:(g + 1) * gw], _TN)
        h_scr[d, g] = h_t * edge_e[:, g * gw:(g + 1) * gw] + st


def _ssd_kernel(*refs, emit_y):
    if emit_y:
        (uf_ref, ub_ref, df_ref, db_ref, h0_ref, bias_ref, a_ref, e_ref, tri_ref,
         yf_ref, yb_ref, h_scr) = refs
    else:
        (uf_ref, ub_ref, df_ref, db_ref, bias_ref, a_ref, e_ref, tri_ref, hout_ref, h_scr) = refs
        yf_ref = yb_ref = None
    c = pl.program_id(1)

    @pl.when(c == 0)
    def _():
        if emit_y:
            h_scr[...] = h0_ref[0]
        else:
            h_scr[...] = jnp.zeros(h_scr.shape, F32)

    _ssd_chunk(uf_ref, df_ref, bias_ref, a_ref, e_ref, tri_ref, h_scr, yf_ref, 0)
    _ssd_chunk(ub_ref, db_ref, bias_ref, a_ref, e_ref, tri_ref, h_scr, yb_ref, 1)
    if not emit_y:
        @pl.when(c == pl.num_programs(1) - 1)
        def _():
            hout_ref[0] = h_scr[...]


def _ssd(u, dtp, h0, bias, a_neg, e_mat, tri, *, batch, seq, q):
    t = u.shape[0]
    nc = seq // q
    emit_y = h0 is not None
    hshape = (N_DIRS, SSD_GROUPS, SSD_STATE, SSD_HEADS_PER_GROUP * SSD_HEAD_DIM)
    fwd = lambda b, c: (b * nc + c, 0)
    bwd = lambda b, c: (b * nc + nc - 1 - c, 0)
    const2 = lambda b, c: (0, 0)
    const3 = lambda b, c: (0, 0, 0)
    in_specs = [pl.BlockSpec((q, XBC_WIDTH), fwd), pl.BlockSpec((q, XBC_WIDTH), bwd),
                pl.BlockSpec((q, LANES), fwd), pl.BlockSpec((q, LANES), bwd)]
    args = [u, u, dtp, dtp]
    if emit_y:
        in_specs.append(pl.BlockSpec((1,) + hshape, lambda b, c: (b, 0, 0, 0, 0)))
        args.append(h0)
    in_specs += [pl.BlockSpec((1, LANES), const2), pl.BlockSpec((1, LANES), const2),
                 pl.BlockSpec((N_DIRS, LANES, SSD_WIDTH), const3),
                 pl.BlockSpec((N_DIRS, q, q), const3)]
    args += [bias, a_neg, e_mat, tri]
    if emit_y:
        out_specs = [pl.BlockSpec((q, SSD_WIDTH), fwd), pl.BlockSpec((q, SSD_WIDTH), bwd)]
        out_shape = [jax.ShapeDtypeStruct((t, SSD_WIDTH), F32)] * 2
    else:
        out_specs = pl.BlockSpec((1,) + hshape, lambda b, c: (b, 0, 0, 0, 0))
        out_shape = jax.ShapeDtypeStruct((batch,) + hshape, F32)
    return pl.pallas_call(
        functools.partial(_ssd_kernel, emit_y=emit_y),
        grid=(batch, nc),
        in_specs=in_specs,
        out_specs=out_specs,
        out_shape=out_shape,
        scratch_shapes=[pltpu.VMEM(hshape, F32)],
        compiler_params=_cparams(("arbitrary", "arbitrary"), 48),
        name="ssd_lat" if emit_y else "ssd_ctx",
    )(*args)


def _outproj_kernel(attn_ref, yf_ref, yb_ref, xs_ref, z_ref, x_ref, w_ref, skip_ref, ng_ref,
                    g1_ref, n2_ref, sc2_ref, sh2_ref, rw_ref, x1_ref, h2_ref, lg_ref):
    y = yf_ref[...] + yb_ref[...] + skip_ref[...] * xs_ref[...].astype(F32)
    y = y * _silu(z_ref[...].astype(F32))
    y = (_rms(y) * ng_ref[...]).astype(BF16)
    m = _dot(attn_ref[...], w_ref[0:ATTN_WIDTH, :]) + _dot(y, w_ref[ATTN_WIDTH:, :])
    x1 = x_ref[...] + g1_ref[0] * m
    x1_ref[...] = x1
    h2 = (_rms(x1) * n2_ref[...]) * (1.0 + sc2_ref[0]) + sh2_ref[0]
    _store_rows(h2_ref, h2, _chunks(h2.shape[1]))
    lg_ref[...] = _dot_f32(rw_ref[...], h2, _NT)


def _outproj(attn, yf, yb, u, main, x, w_out, skip_e, ssd_g, g1, n2, sc2, sh2, rw_t, *, seq):
    t, d = x.shape
    tm = TOK_TILE
    per = seq // tm
    row = lambda i: (i, 0)
    c2 = lambda i: (0, 0)
    bat = lambda i: (i // per, 0, 0)
    in_specs = [pl.BlockSpec((tm, ATTN_WIDTH), row),
                pl.BlockSpec((tm, SSD_WIDTH), row),
                pl.BlockSpec((tm, SSD_WIDTH), row),
                pl.BlockSpec((tm, SSD_WIDTH), row),
                pl.BlockSpec((tm, SSD_WIDTH), lambda i: (i, COL_Z // SSD_WIDTH)),
                pl.BlockSpec((tm, d), row),
                pl.BlockSpec((ATTN_WIDTH + SSD_WIDTH, d), c2),
                pl.BlockSpec((1, SSD_WIDTH), c2),
                pl.BlockSpec((1, SSD_WIDTH), c2),
                pl.BlockSpec((1, 1, d), bat),
                pl.BlockSpec((1, d), c2),
                pl.BlockSpec((1, 1, d), bat),
                pl.BlockSpec((1, 1, d), bat),
                pl.BlockSpec((N_EXPERTS, d), c2)]
    return pl.pallas_call(
        _outproj_kernel,
        grid=(t // tm,),
        in_specs=in_specs,
        out_specs=[pl.BlockSpec((tm, d), row), pl.BlockSpec((tm * _chunks(d), LANES), row),
                   pl.BlockSpec((N_EXPERTS, tm), lambda i: (0, i))],
        out_shape=[jax.ShapeDtypeStruct((t, d), F32),
                   jax.ShapeDtypeStruct((t * _chunks(d), LANES), U32),
                   jax.ShapeDtypeStruct((N_EXPERTS, t), F32)],
        compiler_params=_cparams(("arbitrary",), 52),
        name="outproj",
    )(attn, yf, yb, u, main, x, w_out, skip_e, ssd_g, g1, n2, sc2, sh2, rw_t)


def _first_index(hit, iota, size):
    return jnp.min(jnp.where(hit, iota, size), axis=0, keepdims=True)


def _route_kernel(lg_ref, bias_ref, su_ref, idx_ref, w_ref, rank_ref, cnt_ref, carry_scr):
    i = pl.program_id(0)
    tm = lg_ref.shape[1]

    @pl.when(i == 0)
    def _():
        carry_scr[...] = jnp.zeros(carry_scr.shape, F32)

    scores = _sigmoid(lg_ref[...])
    sel = scores + bias_ref[...][:, 0:1]
    e8 = EXPERTS_PER_GROUP
    i8 = lax.broadcasted_iota(I32, (e8, tm), 0).astype(F32)
    gs = []
    for g in range(N_EXPERT_GROUPS):
        blk = sel[g * e8:(g + 1) * e8, :]
        m1 = jnp.max(blk, axis=0, keepdims=True)
        f1 = _first_index(blk == m1, i8, e8)
        m2 = jnp.max(jnp.where(i8 == f1, NEG_BIG, blk), axis=0, keepdims=True)
        gs.append(m1 + m2)
    gsc = jnp.concatenate(gs, axis=0)
    ig = lax.broadcasted_iota(I32, gsc.shape, 0).astype(F32)
    keep = jnp.zeros(gsc.shape, F32)
    for _ in range(TOPK_GROUPS):
        m = jnp.max(gsc, axis=0, keepdims=True)
        f = _first_index(gsc == m, ig, N_EXPERT_GROUPS)
        keep = jnp.where(ig == f, 1.0, keep)
        gsc = jnp.where(ig == f, NEG_BIG, gsc)
    keep_e = jnp.concatenate(
        [jnp.broadcast_to(keep[g:g + 1, :], (e8, tm)) for g in range(N_EXPERT_GROUPS)], axis=0)
    cand = jnp.where(keep_e > 0.5, sel, NEG_BIG)
    ie = lax.broadcasted_iota(I32, cand.shape, 0).astype(F32)
    chosen = jnp.zeros(cand.shape, F32)
    idxs, svals, hits = [], [], []
    for _ in range(TOP_K):
        m = jnp.max(cand, axis=0, keepdims=True)
        f = _first_index(cand == m, ie, N_EXPERTS)
        hit = ie == f
        idxs.append(f)
        svals.append(jnp.sum(jnp.where(hit, scores, 0.0), axis=0, keepdims=True))
        hits.append(hit)
        chosen = jnp.where(hit, 1.0, chosen)
        cand = jnp.where(hit, NEG_BIG, cand)
    s_all = jnp.concatenate(svals, axis=0)
    idx_ref[...] = jnp.concatenate(idxs, axis=0).astype(I32)
    w_ref[...] = ROUTED_SCALE * s_all / jnp.sum(s_all, axis=0, keepdims=True)
    before = _dot(chosen.astype(BF16), su_ref[...]) + carry_scr[...][:, 0:1]
    ranks = [jnp.sum(jnp.where(h, before, 0.0), axis=0, keepdims=True) for h in hits]
    rank_ref[...] = jnp.concatenate(ranks, axis=0).astype(I32)
    carry_scr[...] = carry_scr[...] + jnp.sum(chosen, axis=1, keepdims=True)
    cnt_ref[...] = carry_scr[...].astype(I32)


def _route(logits_t, bias_col):
    e, t = logits_t.shape
    tm = ROUTE_TILE
    su = jnp.triu(jnp.ones((tm, tm), F32), 1).astype(BF16)
    col = lambda i: (0, i)
    return pl.pallas_call(
        _route_kernel,
        grid=(t // tm,),
        in_specs=[pl.BlockSpec((e, tm), col),
                  pl.BlockSpec((e, LANES), lambda i: (0, 0)),
                  pl.BlockSpec((tm, tm), lambda i: (0, 0))],
        out_specs=[pl.BlockSpec((TOP_K, tm), col), pl.BlockSpec((TOP_K, tm), col),
                   pl.BlockSpec((TOP_K, tm), col), pl.BlockSpec((e, LANES), lambda i: (0, 0))],
        out_shape=[jax.ShapeDtypeStruct((TOP_K, t), I32), jax.ShapeDtypeStruct((TOP_K, t), F32),
                   jax.ShapeDtypeStruct((TOP_K, t), I32), jax.ShapeDtypeStruct((e, LANES), I32)],
        scratch_shapes=[pltpu.VMEM((e, LANES), F32)],
        compiler_params=_cparams(("arbitrary",), 40),
        name="route",
    )(logits_t, bias_col, su)


def _slots_kernel(idx_ref, rank_ref, ps_ref, d_ref, dc_ref):
    tm = idx_ref.shape[1]
    ie = lax.broadcasted_iota(I32, (N_EXPERTS, tm), 0)
    ps = ps_ref[...][:, 0:1]
    rows = []
    for k in range(TOP_K):
        hit = ie == idx_ref[k:k + 1, :]
        rows.append(jnp.sum(jnp.where(hit, ps, 0.0), axis=0, keepdims=True))
    dest = jnp.concatenate(rows, axis=0).astype(I32) + rank_ref[...]
    d_ref[0] = dest
    tc = dc_ref.shape[2]
    for s in range(tm // tc):
        dc_ref[s] = dest[:, s * tc:(s + 1) * tc]


def _slots(idx_t, rank_t, ps_col):
    t = idx_t.shape[1]
    tm = TOK_TILE
    tc = COMB_TILE
    col = lambda i: (0, i)
    return pl.pallas_call(
        _slots_kernel,
        grid=(t // tm,),
        in_specs=[pl.BlockSpec((TOP_K, tm), col), pl.BlockSpec((TOP_K, tm), col),
                  pl.BlockSpec((N_EXPERTS, LANES), lambda i: (0, 0))],
        out_specs=[pl.BlockSpec((1, TOP_K, tm), lambda i: (i, 0, 0)),
                   pl.BlockSpec((tm // tc, TOP_K, tc), lambda i: (i, 0, 0))],
        out_shape=[jax.ShapeDtypeStruct((t // tm, TOP_K, tm), I32),
                   jax.ShapeDtypeStruct((t // tc, TOP_K, tc), I32)],
        compiler_params=_cparams(("arbitrary",), 40),
        name="slots",
    )(idx_t, rank_t, ps_col)


def _row_copy(src, dst, sem):
    return pltpu.make_async_copy(src, dst, sem)


def _dispatch_kernel(pend_ref, dest_ref, h_ref, xs_ref, zero_scr, sem, *, nch):
    i = pl.program_id(0)
    tm = h_ref.shape[0] // nch
    tile_rows = zero_scr.shape[0]

    @pl.when(i == 0)
    def _():
        zero_scr[...] = jnp.zeros(zero_scr.shape, zero_scr.dtype)

        def pad_copy(e):
            start_row = pl.multiple_of(pend_ref[e + 1] * nch - tile_rows, tile_rows)
            return _row_copy(zero_scr, xs_ref.at[pl.ds(start_row, tile_rows), :], sem.at[0])

        def start(e, carry):
            @pl.when(pend_ref[e + 1] > pend_ref[e])
            def _():
                pad_copy(e).start()
            return carry

        def wait(e, carry):
            @pl.when(pend_ref[e + 1] > pend_ref[e])
            def _():
                pad_copy(e).wait()
            return carry

        lax.fori_loop(0, N_EXPERTS, start, 0)
        lax.fori_loop(0, N_EXPERTS, wait, 0)

    def tok_copy(t, k):
        src = pl.multiple_of(t * nch, nch)
        dst = pl.multiple_of(dest_ref[0, k, t] * nch, nch)
        return _row_copy(h_ref.at[pl.ds(src, nch), :], xs_ref.at[pl.ds(dst, nch), :], sem.at[1])

    def start(t, carry):
        for k in range(TOP_K):
            tok_copy(t, k).start(priority=k % 2)
        return carry

    def wait(t, carry):
        for k in range(TOP_K):
            tok_copy(t, k).wait()
        return carry

    lax.fori_loop(0, tm, start, 0)
    lax.fori_loop(0, tm, wait, 0)


def _dispatch(pad_end_ext, dest3, h2, n_slots, nch):
    t = h2.shape[0] // nch
    tm = TOK_TILE
    grid_spec = pltpu.PrefetchScalarGridSpec(
        num_scalar_prefetch=1,
        grid=(t // tm,),
        in_specs=[pl.BlockSpec((1, TOP_K, tm), lambda i, pe: (i, 0, 0), memory_space=pltpu.SMEM),
                  pl.BlockSpec((tm * nch, LANES), lambda i, pe: (i, 0))],
        out_specs=pl.BlockSpec(memory_space=pl.ANY),
        scratch_shapes=[pltpu.VMEM((MOE_TILE * nch, LANES), U32), pltpu.SemaphoreType.DMA((2,))],
    )
    return pl.pallas_call(
        functools.partial(_dispatch_kernel, nch=nch),
        grid_spec=grid_spec,
        out_shape=jax.ShapeDtypeStruct((n_slots * nch, LANES), U32),
        compiler_params=_cparams(("arbitrary",), 40),
        name="dispatch",
    )(pad_end_ext, dest3, h2)


def _experts_kernel(first_ref, ntile_ref, xs_ref, wg_ref, wu_ref, wd_ref, ys_ref,
                    wg_scr, wu_scr, wd_scr, wg_buf, wu_buf, wd_buf, xbuf, ybuf, xsem, ysem, wsem):
    e = pl.program_id(0)
    n = ntile_ref[e]
    first = first_ref[e]
    rows = xbuf.shape[1]
    nch = _chunks(wg_scr.shape[0])
    te = rows // nch
    wslot = e % 2

    has_next = e + 1 < pl.num_programs(0)

    def w_copies(ex, s, c):
        out = []
        for i, (src, dst) in enumerate(((wg_ref, wg_buf), (wu_ref, wu_buf), (wd_ref, wd_buf))):
            dr = dst.shape[1] // WEIGHT_CHUNKS
            r0 = pl.multiple_of(c * dr, dr)
            out.append(_row_copy(src.at[ex, pl.ds(r0, dr), :], dst.at[s, pl.ds(r0, dr), :],
                                 wsem.at[s, i]))
        return out

    def start_next_chunk(c):
        @pl.when(has_next & (c < WEIGHT_CHUNKS))
        def _():
            for cp in w_copies(e + 1, 1 - wslot, c):
                cp.start()

    @pl.when(e == 0)
    def _():
        for c in range(WEIGHT_CHUNKS):
            for cp in w_copies(0, 0, c):
                cp.start()

    nx = xbuf.shape[0]
    ny = ybuf.shape[0]

    def x_copy(j):
        r0 = pl.multiple_of((first + j) * rows, rows)
        return _row_copy(xs_ref.at[pl.ds(r0, rows), :], xbuf.at[j % nx], xsem.at[j % nx])

    def y_copy(j):
        r0 = pl.multiple_of((first + j) * rows, rows)
        return _row_copy(ybuf.at[j % ny], ys_ref.at[pl.ds(r0, rows), :], ysem.at[j % ny])

    for j0 in range(nx - 1):
        @pl.when(j0 < n)
        def _():
            x_copy(j0).start(priority=TILE_DMA_PRIORITY)

    for c in range(WEIGHT_CHUNKS):
        for cp in w_copies(e, wslot, c):
            cp.wait()
    wg_scr[...] = wg_buf[wslot].astype(BF16)
    wu_scr[...] = wu_buf[wslot].astype(BF16)
    wd_scr[...] = wd_buf[wslot].astype(BF16)

    @pl.when(n > 0)
    def _():

        def tile(j, carry):
            @pl.when(j + nx - 1 < n)
            def _():
                x_copy(j + nx - 1).start(priority=TILE_DMA_PRIORITY)

            start_next_chunk(j)
            x_copy(j).wait()

            @pl.when(j >= ny)
            def _():
                y_copy(j - ny).wait()

            xb = _load_rows(xbuf.at[j % nx], te, nch).astype(BF16)
            act = _silu(_dot(xb, wg_scr[...])) * _dot(xb, wu_scr[...])
            _store_rows(ybuf.at[j % ny], _dot(act.astype(BF16), wd_scr[...]), nch)
            y_copy(j).start(priority=TILE_DMA_PRIORITY)
            return carry

        lax.fori_loop(0, n, tile, 0)

        for back in range(ny, 0, -1):
            @pl.when(n >= back)
            def _():
                y_copy(n - back).wait()

    def rest(c, carry):
        start_next_chunk(c)
        return carry

    lax.fori_loop(jnp.minimum(n, WEIGHT_CHUNKS), WEIGHT_CHUNKS, rest, 0)


def _experts(first_tile, n_tile, xs, wg, wu, wd):
    n_exp, d, f = wg.shape
    nch = _chunks(d)
    rows = MOE_TILE * nch
    grid_spec = pltpu.PrefetchScalarGridSpec(
        num_scalar_prefetch=2,
        grid=(n_exp,),
        in_specs=[pl.BlockSpec(memory_space=pl.ANY), pl.BlockSpec(memory_space=pl.ANY),
                  pl.BlockSpec(memory_space=pl.ANY), pl.BlockSpec(memory_space=pl.ANY)],
        out_specs=pl.BlockSpec(memory_space=pl.ANY),
        scratch_shapes=[pltpu.VMEM((d, f), BF16), pltpu.VMEM((d, f), BF16), pltpu.VMEM((f, d), BF16),
                        pltpu.VMEM((2, d, f), F32), pltpu.VMEM((2, d, f), F32),
                        pltpu.VMEM((2, f, d), F32),
                        pltpu.VMEM((3, rows, LANES), U32), pltpu.VMEM((2, rows, LANES), U32),
                        pltpu.SemaphoreType.DMA((3,)), pltpu.SemaphoreType.DMA((2,)),
                        pltpu.SemaphoreType.DMA((2, 3))],
    )
    return pl.pallas_call(
        _experts_kernel,
        grid_spec=grid_spec,
        out_shape=jax.ShapeDtypeStruct(xs.shape, U32),
        compiler_params=_cparams(("arbitrary",), 56),
        name="experts",
    )(first_tile, n_tile, xs, wg, wu, wd)


def _combine_kernel(dest_ref, dnext_ref, ys_ref, wt_ref, h_ref, x1_ref, sg_ref, su_ref, sd_ref,
                    g2_ref, fg_ref, eye_ref, o_ref, buf, sem):
    i = pl.program_id(0)
    n = pl.num_programs(0)
    tm = x1_ref.shape[0]
    nch = _chunks(x1_ref.shape[1])
    slot = i % 2

    def row_copy(d_ref, s, t, k):
        src = pl.multiple_of(d_ref[0, k, t] * nch, nch)
        dst = pl.multiple_of(t * nch, nch)
        return _row_copy(ys_ref.at[pl.ds(src, nch), :], buf.at[s, k, pl.ds(dst, nch), :], sem.at[s])

    def start_tile(d_ref, s):
        def body(t, carry):
            for k in range(TOP_K):
                row_copy(d_ref, s, t, k).start(priority=k % 2)
            return carry
        lax.fori_loop(0, tm, body, 0)

    @pl.when(i == 0)
    def _():
        start_tile(dest_ref, 0)

    @pl.when(i + 1 < n)
    def _():
        start_tile(dnext_ref, 1 - slot)

    hb = _load_rows(h_ref, tm, nch).astype(BF16)
    act = _silu(_dot(hb, sg_ref[...])) * _dot(hb, su_ref[...])
    ffn = _dot(act.astype(BF16), sd_ref[...])
    w_col = _dot_rhs_f32(eye_ref[...], wt_ref[...], _NT)

    def wait(t, carry):
        for k in range(TOP_K):
            row_copy(dest_ref, slot, t, k).wait()
        return carry

    lax.fori_loop(0, tm, wait, 0)
    for k in range(TOP_K):
        ffn = ffn + w_col[:, k:k + 1] * _load_rows(buf.at[slot, k], tm, nch)
    x2 = x1_ref[...] + g2_ref[0] * ffn
    o_ref[...] = _rms(x2) * fg_ref[...]


def _combine(dest3, ys, w_t, h2, x1, sg, su, sd, g2, fg, *, seq):
    t, d = x1.shape
    nch = _chunks(d)
    tm = COMB_TILE
    per = seq // tm
    f = sg.shape[1]
    eye = jnp.eye(tm, dtype=BF16)
    row = lambda i: (i, 0)
    c2 = lambda i: (0, 0)
    last = t // tm - 1
    return pl.pallas_call(
        _combine_kernel,
        grid=(t // tm,),
        in_specs=[pl.BlockSpec((1, TOP_K, tm), lambda i: (i, 0, 0), memory_space=pltpu.SMEM),
                  pl.BlockSpec((1, TOP_K, tm), lambda i: (jnp.minimum(i + 1, last), 0, 0),
                               memory_space=pltpu.SMEM),
                  pl.BlockSpec(memory_space=pl.ANY),
                  pl.BlockSpec((TOP_K, tm), lambda i: (0, i)),
                  pl.BlockSpec((tm * nch, LANES), row),
                  pl.BlockSpec((tm, d), row),
                  pl.BlockSpec((d, f), c2), pl.BlockSpec((d, f), c2), pl.BlockSpec((f, d), c2),
                  pl.BlockSpec((1, 1, d), lambda i: (i // per, 0, 0)),
                  pl.BlockSpec((1, d), c2),
                  pl.BlockSpec((tm, tm), c2)],
        out_specs=pl.BlockSpec((tm, d), row),
        out_shape=jax.ShapeDtypeStruct((t, d), F32),
        scratch_shapes=[pltpu.VMEM((2, TOP_K, tm * nch, LANES), U32),
                        pltpu.SemaphoreType.DMA((2,))],
        compiler_params=_cparams(("arbitrary",), 48),
        name="combine",
    )(dest3, dest3, ys, w_t, h2, x1, sg, su, sd, g2, fg, eye)


def _rope_tables(seq):
    n_rows = seq // GRID_W
    rows = jnp.repeat(jnp.arange(n_rows, dtype=F32), GRID_W)
    cols = jnp.tile(jnp.arange(GRID_W, dtype=F32), n_rows)
    inv = ROPE_THETA ** (-jnp.arange(0, ROPE_AXIS_DIM, 2, dtype=F32) / ROPE_AXIS_DIM)
    ar = rows[:, None] * inv[None, :]
    ac = cols[:, None] * inv[None, :]
    cos = jnp.concatenate([jnp.cos(ar), jnp.cos(ac), jnp.cos(ar), jnp.cos(ac)], axis=1)
    sin = jnp.concatenate([-jnp.sin(ar), -jnp.sin(ac), jnp.sin(ar), jnp.sin(ac)], axis=1)
    return cos, sin


def _rope_head_order(n_heads):
    q = ROPE_AXIS_DIM // 2
    one = np.concatenate([np.arange(0, q), np.arange(2 * q, 3 * q), np.arange(q, 2 * q),
                          np.arange(3 * q, 4 * q)])
    return np.concatenate([h * HEAD_DIM + one for h in range(n_heads)])


def _layer(x, c, ctx, c_ctx, w_ada, b_ada, norm1_g, norm2_g, w_in, attn_sink, conv_w, conv_b,
           dt_bias, a_log, d_skip, ssd_norm_g, w_out, router_w, router_bias, wg, wu, wd,
           sw_gate, sw_up, sw_down, final_norm_g):
    batch, seq, d = x.shape
    ctx_len = ctx.shape[1]
    t = batch * seq

    cvec = jnp.zeros((SUBLANES, d), F32).at[:batch].set(c).at[batch].set(c_ctx)
    mod = _ada(cvec, w_ada, b_ada[None, :])
    sh1, sc1, g1, sh2, sc2, g2 = [mod[:batch, i * d:(i + 1) * d][:, None, :] for i in range(6)]
    sh1c, sc1c = [jnp.broadcast_to(mod[batch, i * d:(i + 1) * d][None, None, :], (batch, 1, d))
                  for i in range(2)]

    o_q, o_k, o_v, o_z, o_x, o_dt = np.cumsum((0, ATTN_WIDTH, KV_WIDTH, KV_WIDTH, SSD_WIDTH,
                                               XBC_WIDTH)).tolist()
    w_q = w_in[:, o_q:o_k][:, _rope_head_order(N_Q_HEADS)]
    w_k = w_in[:, o_k:o_v][:, _rope_head_order(N_KV_HEADS)]
    w_main = jnp.concatenate([w_q, w_in[:, o_z:o_x], w_in[:, o_x:o_dt],
                              w_k, w_in[:, o_v:o_z]], axis=1).astype(BF16)
    w_dt = jnp.pad(w_in[:, o_dt:], ((0, 0), (0, LANES - N_DIRS * SSD_HEADS))).astype(BF16)
    cos, sin = _rope_tables(seq)
    n1 = norm1_g[None, :]
    main, dtp = _inproj(x.reshape(t, d), n1, sc1, sh1, w_main, w_dt, cos, sin,
                        seq=seq, tm=min(seq, 1024), rope=True)
    main_c, dtp_c = _inproj(ctx.reshape(batch * ctx_len, d), n1, sc1c, sh1c,
                            w_main[:, COL_XBC:], w_dt, None, None,
                            seq=ctx_len, tm=ctx_len, rope=False)

    attn = _attn(main, main_c, attn_sink, batch=batch, seq=seq, ctx_len=ctx_len)

    cw8 = jnp.pad(conv_w, ((0, SUBLANES - SSD_CONV), (0, 0)))
    u = _conv(main, COL_XBC, cw8, conv_b[None, :], seq=seq)
    u_c = _conv(main_c, 0, cw8, conv_b[None, :], seq=ctx_len)
    nd = N_DIRS * SSD_HEADS
    bias = jnp.pad(dt_bias.reshape(1, nd), ((0, 0), (0, LANES - nd)))
    a_neg = jnp.pad(-jnp.exp(a_log.reshape(1, nd)), ((0, 0), (0, LANES - nd)))
    lane_head = np.arange(SSD_WIDTH) // SSD_HEAD_DIM
    e_np = np.zeros((N_DIRS, LANES, SSD_WIDTH), np.float32)
    for dd in range(N_DIRS):
        e_np[dd, dd * SSD_HEADS + lane_head, np.arange(SSD_WIDTH)] = 1.0
    e_mat = jnp.asarray(e_np, BF16)

    def tri(q):
        lo = np.tril(np.ones((q, q), np.float32))
        return jnp.asarray(np.stack([lo, lo.T]), BF16)

    qc = min(SSD_CHUNK, ctx_len)
    h_ctx = _ssd(u_c, dtp_c, None, bias, a_neg, e_mat, tri(qc), batch=batch, seq=ctx_len, q=qc)
    ql = min(SSD_CHUNK, seq)
    yf, yb = _ssd(u, dtp, h_ctx, bias, a_neg, e_mat, tri(ql), batch=batch, seq=seq, q=ql)

    skip_e = jnp.repeat(d_skip, SSD_HEAD_DIM)[None, :]
    x1, h2, logits_t = _outproj(attn, yf, yb, u, main, x.reshape(t, d), w_out.astype(BF16),
                                skip_e, ssd_norm_g[None, :], g1, norm2_g[None, :], sc2, sh2,
                                router_w.T, seq=seq)

    bias_col = jnp.broadcast_to(router_bias[:, None], (N_EXPERTS, LANES))
    idx_t, w_t, rank_t, cnt = _route(logits_t, bias_col)
    counts = cnt[:, 0]
    te = MOE_TILE
    padded = (counts + te - 1) // te * te
    pad_end = jnp.cumsum(padded)
    pad_start = pad_end - padded
    n_slots = -(-(t * TOP_K + N_EXPERTS * (te - 1)) // te) * te
    ps_col = jnp.broadcast_to(pad_start.astype(F32)[:, None], (N_EXPERTS, LANES))
    dest3, dest3c = _slots(idx_t, rank_t, ps_col)
    pad_end_ext = jnp.concatenate([jnp.zeros((1,), I32), pad_end.astype(I32)])

    xs = _dispatch(pad_end_ext, dest3, h2, n_slots, _chunks(d))
    ys = _experts((pad_start // te).astype(I32), (padded // te).astype(I32), xs, wg, wu, wd)
    out = _combine(dest3c, ys, w_t, h2, x1, sw_gate.astype(BF16), sw_up.astype(BF16),
                   sw_down.astype(BF16), g2, final_norm_g[None, :], seq=seq)
    return out.reshape(batch, seq, d)


def kernel(x, c, ctx, c_ctx, w_ada, b_ada, norm1_g, norm2_g, w_in, attn_sink, conv_w, conv_b, dt_bias, a_log, d_skip, ssd_norm_g, w_out, router_w, router_bias, expert_w_gate, expert_w_up, expert_w_down, shared_w_gate, shared_w_up, shared_w_down, final_norm_g):
    assert w_ada.shape[0] == 1, "single-layer stack only"
    return _layer(x, c, ctx, c_ctx, w_ada[0], b_ada[0], norm1_g[0], norm2_g[0], w_in[0],
                  attn_sink[0], conv_w[0], conv_b[0], dt_bias[0], a_log[0], d_skip[0],
                  ssd_norm_g[0], w_out[0], router_w[0], router_bias[0], expert_w_gate[0],
                  expert_w_up[0], expert_w_down[0], shared_w_gate[0], shared_w_up[0],
                  shared_w_down[0], final_norm_g)
```

```python
import functools
import math

import jax
import jax.numpy as jnp
import numpy as np
from jax import lax
from jax.experimental import pallas as pl
from jax.experimental.pallas import tpu as pltpu

F32 = jnp.float32
BF16 = jnp.bfloat16
I32 = jnp.int32

EPS = 1e-6
GRID_W = 64
N_Q_HEADS = 8
N_KV_HEADS = 2
Q_PER_KV = N_Q_HEADS // N_KV_HEADS
HEAD_DIM = 128
ATTN_WIDTH = N_Q_HEADS * HEAD_DIM
KV_WIDTH = N_KV_HEADS * HEAD_DIM
WINDOW = 128
ATTN_BLOCK = 128
ROPE_THETA = 10000.0
ROPE_AXIS_DIM = HEAD_DIM // 2
SSD_HEADS = 16
SSD_HEAD_DIM = 64
SSD_WIDTH = SSD_HEADS * SSD_HEAD_DIM
SSD_GROUPS = 4
SSD_HEADS_PER_GROUP = SSD_HEADS // SSD_GROUPS
SSD_STATE = 128
SSD_CONV = 5
N_DIRS = 2
XBC_WIDTH = SSD_WIDTH + 2 * SSD_GROUPS * SSD_STATE
N_EXPERTS = 64
N_EXPERT_GROUPS = 8
EXPERTS_PER_GROUP = N_EXPERTS // N_EXPERT_GROUPS
TOPK_GROUPS = 4
TOP_K = 8
EXPERT_DIM = 512
ROUTED_SCALE = 2.5

LANES = 128
SUBLANES = 8
NEG_BIG = -1e30

COL_Q = 0
COL_Z = ATTN_WIDTH
COL_XBC = COL_Z + SSD_WIDTH
COL_K = COL_XBC + XBC_WIDTH
COL_V = COL_K + KV_WIDTH
MAIN_WIDTH = COL_V + KV_WIDTH

IN_TILE_N = 512
SSD_CHUNK = 256
MOE_TILE = 256
ROUTE_TILE = 512
TOK_TILE = 256
COMB_TILE = 128
TILE_DMA_PRIORITY = 1
WEIGHT_CHUNKS = 8

_NT = (((1,), (1,)), ((), ()))
_TN = (((0,), (0,)), ((), ()))


def _cparams(sem, vmem_mb):
    return pltpu.CompilerParams(dimension_semantics=sem, vmem_limit_bytes=vmem_mb * 1024 * 1024)


def _dot(a, b, dims=None):
    if dims is None:
        return jnp.dot(a, b, preferred_element_type=F32)
    return lax.dot_general(a, b, dims, preferred_element_type=F32)


def _split(a):
    hi = a.astype(BF16)
    lo = (a - hi.astype(F32)).astype(BF16)
    return hi, lo


def _dot_lhs_f32(a, b_exact, dims=None):
    hi, lo = _split(a)
    return _dot(hi, b_exact, dims) + _dot(lo, b_exact, dims)


def _dot_rhs_f32(a_exact, b, dims=None):
    hi, lo = _split(b)
    return _dot(a_exact, hi, dims) + _dot(a_exact, lo, dims)


def _dot_f32(a, b, dims=None):
    ah, al = _split(a)
    bh, bl = _split(b)
    return _dot(ah, bh, dims) + (_dot(al, bh, dims) + _dot(ah, bl, dims))


def _sigmoid(x):
    return 1.0 / (1.0 + jnp.exp(-x))


def _silu(x):
    return x * _sigmoid(x)


def _rms(x):
    return x * lax.rsqrt(jnp.mean(x * x, axis=-1, keepdims=True) + EPS)


U32 = jnp.uint32
_HI_MASK = 0xFFFF0000


def _chunks(d):
    return d // (2 * LANES)


def _load_rows(ref, rows, nch):
    words = [ref[pl.ds(j, rows, stride=nch), :] for j in range(nch)]
    lo = [lax.bitcast_convert_type(w << 16, F32) for w in words]
    hi = [lax.bitcast_convert_type(w & jnp.uint32(_HI_MASK), F32) for w in words]
    return jnp.concatenate(lo + hi, axis=1)


def _store_rows(ref, val, nch):
    rows, d = val.shape
    bits = lax.bitcast_convert_type(val.astype(BF16).astype(F32), U32)
    for j in range(nch):
        lo = bits[:, j * LANES:(j + 1) * LANES] >> 16
        hi = bits[:, d // 2 + j * LANES:d // 2 + (j + 1) * LANES] & jnp.uint32(_HI_MASK)
        ref[pl.ds(j, rows, stride=nch), :] = lo | hi


def _ada_kernel(c_ref, w_ref, b_ref, o_ref):
    o_ref[...] = _dot_f32(_silu(c_ref[...]), w_ref[...]) + b_ref[...]


def _ada(cvec, w, b):
    m, d = cvec.shape
    n = w.shape[1]
    tn = 1024
    return pl.pallas_call(
        _ada_kernel,
        grid=(n // tn,),
        in_specs=[pl.BlockSpec((m, d), lambda j: (0, 0)),
                  pl.BlockSpec((d, tn), lambda j: (0, j)),
                  pl.BlockSpec((1, tn), lambda j: (0, j))],
        out_specs=pl.BlockSpec((m, tn), lambda j: (0, j)),
        out_shape=jax.ShapeDtypeStruct((m, n), F32),
        compiler_params=_cparams(("arbitrary",), 40),
        name="ada",
    )(cvec, w, b)


def _rope_mix(acc, cos, sin):
    heads = acc.shape[1] // HEAD_DIM
    partner = jnp.concatenate(
        [pltpu.roll(acc[:, h * HEAD_DIM:(h + 1) * HEAD_DIM], HEAD_DIM // 2, 1) for h in range(heads)],
        axis=1)
    return acc * cos + partner * sin


def _inproj_kernel(*refs, rope, n_rope_tiles, kv_tile):
    if rope:
        x_ref, g_ref, sc_ref, sh_ref, w_ref, wdt_ref, cos_ref, sin_ref, o_ref, dt_ref, h_scr = refs
    else:
        x_ref, g_ref, sc_ref, sh_ref, w_ref, wdt_ref, o_ref, dt_ref, h_scr = refs
    j = pl.program_id(1)

    @pl.when(j == 0)
    def _():
        h = (_rms(x_ref[...]) * g_ref[...]) * (1.0 + sc_ref[0]) + sh_ref[0]
        hb = h.astype(BF16)
        h_scr[...] = hb
        dt_ref[...] = _dot(hb, wdt_ref[...])

    acc = _dot(h_scr[...], w_ref[0])
    if not rope:
        o_ref[...] = acc.astype(o_ref.dtype)
        return
    reps = acc.shape[1] // HEAD_DIM

    @pl.when(j < n_rope_tiles)
    def _():
        cos = jnp.concatenate([cos_ref[...]] * reps, axis=1)
        sin = jnp.concatenate([sin_ref[...]] * reps, axis=1)
        o_ref[...] = _rope_mix(acc, cos, sin).astype(o_ref.dtype)

    @pl.when(j == kv_tile)
    def _():
        nk = KV_WIDTH // HEAD_DIM
        ones = jnp.ones_like(cos_ref[...])
        cos = jnp.concatenate([cos_ref[...]] * nk + [ones] * (reps - nk), axis=1)
        sin = jnp.concatenate([sin_ref[...]] * nk + [ones * 0.0] * (reps - nk), axis=1)
        o_ref[...] = _rope_mix(acc, cos, sin).astype(o_ref.dtype)

    @pl.when((j >= n_rope_tiles) & (j != kv_tile))
    def _():
        o_ref[...] = acc.astype(o_ref.dtype)


def _wprep_kernel(w_ref, wdt_ref, o_ref, odt_ref):
    j = pl.program_id(0)
    x = w_ref[...].T
    tn = x.shape[1]
    q = ROPE_AXIS_DIM // 2
    lane = lax.broadcasted_iota(I32, x.shape, 1)
    blk = (lane & (HEAD_DIM - 1)) // q
    swapped = jnp.where(blk == 1, pltpu.roll(x, tn - q, 1), jnp.where(blk == 2, pltpu.roll(x, q, 1), x))
    n_q = ATTN_WIDTH // tn
    kv = COL_K // tn

    @pl.when(j == 0)
    def _():
        rows = wdt_ref[...]
        pad = jnp.zeros((LANES - rows.shape[0], rows.shape[1]), rows.dtype)
        odt_ref[...] = jnp.concatenate([rows, pad], axis=0).T.astype(odt_ref.dtype)

    @pl.when(j < n_q)
    def _():
        o_ref[0] = swapped.astype(o_ref.dtype)

    @pl.when(j == kv)
    def _():
        o_ref[0] = jnp.where(lane < KV_WIDTH, swapped, x).astype(o_ref.dtype)

    @pl.when((j >= n_q) & (j != kv))
    def _():
        o_ref[0] = x.astype(o_ref.dtype)


def _wprep(w_in_t):
    d = w_in_t.shape[1]
    tn = IN_TILE_N
    n_q = ATTN_WIDTH // tn
    n_tiles = MAIN_WIDTH // tn
    src = lambda j: (jnp.where(j < n_q, j, jnp.where(j < n_tiles - 1, j + 1, n_q)), 0)
    n_dt = w_in_t.shape[0] - MAIN_WIDTH
    return pl.pallas_call(
        _wprep_kernel,
        grid=(n_tiles,),
        in_specs=[pl.BlockSpec((tn, d), src),
                  pl.BlockSpec((n_dt, d), lambda j: (MAIN_WIDTH // n_dt, 0))],
        out_specs=[pl.BlockSpec((1, d, tn), lambda j: (j, 0, 0)),
                   pl.BlockSpec((d, LANES), lambda j: (0, 0))],
        out_shape=[jax.ShapeDtypeStruct((n_tiles, d, tn), BF16),
                   jax.ShapeDtypeStruct((d, LANES), BF16)],
        compiler_params=_cparams(("arbitrary",), 40),
        name="wprep",
    )(w_in_t, w_in_t)


def _inproj(x, g, sc, sh, w, wdt, cos, sin, *, seq, tm, rope, tile0, n_tiles):
    t, d = x.shape
    tn = IN_TILE_N
    wn = n_tiles * tn
    per = seq // tm
    in_specs = [pl.BlockSpec((tm, d), lambda i, j: (i, 0)),
                pl.BlockSpec((1, d), lambda i, j: (0, 0)),
                pl.BlockSpec((1, 1, d), lambda i, j: (i // per, 0, 0)),
                pl.BlockSpec((1, 1, d), lambda i, j: (i // per, 0, 0)),
                pl.BlockSpec((1, d, tn), lambda i, j: (j + tile0, 0, 0)),
                pl.BlockSpec((d, LANES), lambda i, j: (0, 0))]
    args = [x, g, sc, sh, w, wdt]
    if rope:
        in_specs += [pl.BlockSpec((tm, HEAD_DIM), lambda i, j: (i % per, 0)),
                     pl.BlockSpec((tm, HEAD_DIM), lambda i, j: (i % per, 0))]
        args += [cos, sin]
    kern = functools.partial(_inproj_kernel, rope=rope, n_rope_tiles=ATTN_WIDTH // tn,
                             kv_tile=COL_K // tn)
    return pl.pallas_call(
        kern,
        grid=(t // tm, wn // tn),
        in_specs=in_specs,
        out_specs=[pl.BlockSpec((tm, tn), lambda i, j: (i, j)),
                   pl.BlockSpec((tm, LANES), lambda i, j: (i, 0))],
        out_shape=[jax.ShapeDtypeStruct((t, wn), BF16), jax.ShapeDtypeStruct((t, LANES), F32)],
        scratch_shapes=[pltpu.VMEM((tm, d), BF16)],
        compiler_params=_cparams(("arbitrary", "arbitrary"), 48),
        name="inproj_rope" if rope else "inproj",
    )(*args)


def _conv_kernel(x_ref, w_ref, b_ref, o_ref, pad_scr, *, seq, rows):
    halo = SUBLANES
    cw = x_ref.shape[1]
    pad_scr[0:halo, :] = jnp.zeros((halo, cw), F32)
    pad_scr[seq + halo:seq + 2 * halo, :] = jnp.zeros((halo, cw), F32)
    pad_scr[halo:seq + halo, :] = x_ref[...].astype(F32)
    w = w_ref[...]
    bias = b_ref[...]
    for r0 in range(0, seq, rows):
        acc = jnp.broadcast_to(bias, (rows, cw))
        for tap in range(SSD_CONV):
            start = r0 + halo - SSD_CONV // 2 + tap
            acc = acc + w[tap:tap + 1, :] * pad_scr[start:start + rows, :]
        o_ref[r0:r0 + rows, :] = _silu(acc).astype(o_ref.dtype)


def _conv(main, col0, w8, b, *, seq):
    t = main.shape[0]
    cw = 256
    rows = min(seq, 256)
    cb0 = col0 // cw
    kern = functools.partial(_conv_kernel, seq=seq, rows=rows)
    return pl.pallas_call(
        kern,
        grid=(t // seq, XBC_WIDTH // cw),
        in_specs=[pl.BlockSpec((seq, cw), lambda b_, c: (b_, cb0 + c)),
                  pl.BlockSpec((SUBLANES, cw), lambda b_, c: (0, c)),
                  pl.BlockSpec((1, cw), lambda b_, c: (0, c))],
        out_specs=pl.BlockSpec((seq, cw), lambda b_, c: (b_, c)),
        out_shape=jax.ShapeDtypeStruct((t, XBC_WIDTH), BF16),
        scratch_shapes=[pltpu.VMEM((seq + 2 * SUBLANES, cw), F32)],
        compiler_params=_cparams(("arbitrary", "arbitrary"), 40),
        name="conv",
    )(main, w8, b)


def _attn_kernel(sink_ref, q_ref, kp_ref, kc_ref, kn_ref, vp_ref, vc_ref, vn_ref,
                 kx_ref, vx_ref, o_ref, *, nb):
    qi = pl.program_id(1)
    blk = ATTN_BLOCK
    g = Q_PER_KV
    hd = HEAD_DIM
    scale = hd ** -0.5
    r = lax.broadcasted_iota(I32, (g * blk, 3 * blk), 0) & (blk - 1)
    c = lax.broadcasted_iota(I32, (g * blk, 3 * blk), 1)
    lo = jnp.maximum(r, jnp.where(qi == 0, blk, 0))
    hi = jnp.minimum(r + 2 * WINDOW, jnp.where(qi == nb - 1, 2 * blk - 1, 3 * blk - 1))
    valid = (c >= lo) & (c <= hi)
    row = lax.broadcasted_iota(I32, (g * blk, 1), 0)
    for hk in range(N_KV_HEADS):
        ks = slice(hk * hd, (hk + 1) * hd)
        q = jnp.concatenate([q_ref[:, (hk * g + i) * hd:(hk * g + i + 1) * hd] for i in range(g)],
                            axis=0)
        k_loc = jnp.concatenate([kp_ref[:, ks], kc_ref[:, ks], kn_ref[:, ks]], axis=0)
        v_loc = jnp.concatenate([vp_ref[:, ks], vc_ref[:, ks], vn_ref[:, ks]], axis=0)
        s_loc = _dot(q, k_loc, _NT) * scale
        s_ctx = _dot(q, kx_ref[:, ks], _NT) * scale
        s_loc = jnp.where(valid, s_loc, NEG_BIG)
        sink = jnp.zeros((g * blk, 1), F32)
        for i in range(g):
            sink = jnp.where((row >= i * blk) & (row < (i + 1) * blk), sink_ref[hk * g + i], sink)
        m = jnp.maximum(jnp.maximum(jnp.max(s_loc, axis=1, keepdims=True),
                                    jnp.max(s_ctx, axis=1, keepdims=True)), sink)
        p_loc = jnp.exp(s_loc - m)
        p_ctx = jnp.exp(s_ctx - m)
        den = (jnp.sum(p_loc, axis=1, keepdims=True) + jnp.sum(p_ctx, axis=1, keepdims=True)
               + jnp.exp(sink - m))
        o = _dot(p_loc.astype(BF16), v_loc) + _dot(p_ctx.astype(BF16), vx_ref[:, ks])
        o = o / den
        for i in range(g):
            o_ref[:, (hk * g + i) * hd:(hk * g + i + 1) * hd] = (
                o[i * blk:(i + 1) * blk, :].astype(o_ref.dtype))


def _attn(main, main_c, sink, *, batch, seq, ctx_len):
    t = main.shape[0]
    blk = ATTN_BLOCK
    nb = seq // blk
    kcol = COL_K // KV_WIDTH
    vcol = COL_V // KV_WIDTH
    kcol_c = XBC_WIDTH // KV_WIDTH
    vcol_c = kcol_c + 1

    def nbr(col, off):
        return pl.BlockSpec(
            (blk, KV_WIDTH), lambda b, i: (b * nb + jnp.clip(i + off, 0, nb - 1), col))

    in_specs = [pl.BlockSpec(memory_space=pltpu.SMEM),
                pl.BlockSpec((blk, ATTN_WIDTH), lambda b, i: (b * nb + i, 0)),
                nbr(kcol, -1), nbr(kcol, 0), nbr(kcol, 1),
                nbr(vcol, -1), nbr(vcol, 0), nbr(vcol, 1),
                pl.BlockSpec((ctx_len, KV_WIDTH), lambda b, i: (b, kcol_c)),
                pl.BlockSpec((ctx_len, KV_WIDTH), lambda b, i: (b, vcol_c))]
    return pl.pallas_call(
        functools.partial(_attn_kernel, nb=nb),
        grid=(batch, nb),
        in_specs=in_specs,
        out_specs=pl.BlockSpec((blk, ATTN_WIDTH), lambda b, i: (b * nb + i, 0)),
        out_shape=jax.ShapeDtypeStruct((t, ATTN_WIDTH), BF16),
        compiler_params=_cparams(("arbitrary", "arbitrary"), 40),
        name="attn",
    )(sink, main, main, main, main, main, main, main, main_c, main_c)


def _softplus(x):
    return jnp.maximum(x, 0.0) + jnp.log1p(jnp.exp(-jnp.abs(x)))


def _ssd_chunk(u_ref, dtp_ref, bias_ref, a_ref, e_ref, tri_ref, h_scr, y_ref, d):
    q = u_ref.shape[0]
    rev = d == 1
    p = SSD_HEAD_DIM
    gw = SSD_HEADS_PER_GROUP * p
    dt = _softplus(dtp_ref[...] + bias_ref[...])
    dta = dt * a_ref[...]
    tri = tri_ref[d]
    a_cum = _dot_rhs_f32(tri, dta)
    a_row = _dot_lhs_f32(dta.T, tri, _NT)
    edge = a_cum[0:1, :] if rev else a_cum[q - 1:q, :]
    ea = jnp.exp(a_cum)
    w_state = dt * jnp.exp(edge - a_cum)
    e_mat = e_ref[d]
    dt_e = _dot_lhs_f32(dt, e_mat)
    ws_e = _dot_lhs_f32(w_state, e_mat)
    x = u_ref[:, 0:SSD_WIDTH].astype(F32)
    xdd = (x * ws_e).astype(BF16)
    li = lax.broadcasted_iota(I32, (q, q), 0)
    si = lax.broadcasted_iota(I32, (q, q), 1)
    keep = (si >= li) if rev else (si <= li)
    ea_e = _dot_lhs_f32(ea, e_mat)
    edge_e = ea_e[0:1, :] if rev else ea_e[q - 1:q, :]
    if y_ref is not None:
        xd = (x * dt_e).astype(BF16)
    for g in range(SSD_GROUPS):
        b_g = u_ref[:, SSD_WIDTH + g * SSD_STATE:SSD_WIDTH + (g + 1) * SSD_STATE]
        c0 = SSD_WIDTH + SSD_GROUPS * SSD_STATE + g * SSD_STATE
        c_g = u_ref[:, c0:c0 + SSD_STATE]
        h_t = h_scr[d, g]
        if y_ref is not None:
            cb = _dot(c_g, b_g, _NT)
            ys = []
            for r in range(SSD_HEADS_PER_GROUP):
                hd = g * SSD_HEADS_PER_GROUP + r
                ln = d * SSD_HEADS + hd
                seg = a_cum[:, ln:ln + 1] - a_row[ln:ln + 1, :]
                decay = jnp.exp(jnp.where(keep, seg, NEG_BIG))
                ys.append(_dot((cb * decay).astype(BF16), xd[:, hd * p:(hd + 1) * p]))
            y_off = _dot(c_g, h_t.astype(BF16)) * ea_e[:, g * gw:(g + 1) * gw]
            y_ref[:, g * gw:(g + 1) * gw] = jnp.concatenate(ys, axis=1) + y_off
        st = _dot(b_g, xdd[:, g * gw:(g + 1) * gw], _TN)
        h_scr[d, g] = h_t * edge_e[:, g * gw:(g + 1) * gw] + st


def _ssd_kernel(*refs, emit_y):
    if emit_y:
        (uf_ref, ub_ref, df_ref, db_ref, h0_ref, bias_ref, a_ref, e_ref, tri_ref,
         yf_ref, yb_ref, h_scr) = refs
    else:
        (uf_ref, ub_ref, df_ref, db_ref, bias_ref, a_ref, e_ref, tri_ref, hout_ref, h_scr) = refs
        yf_ref = yb_ref = None
    c = pl.program_id(1)

    @pl.when(c == 0)
    def _():
        if emit_y:
            h_scr[...] = h0_ref[0]
        else:
            h_scr[...] = jnp.zeros(h_scr.shape, F32)

    _ssd_chunk(uf_ref, df_ref, bias_ref, a_ref, e_ref, tri_ref, h_scr, yf_ref, 0)
    _ssd_chunk(ub_ref, db_ref, bias_ref, a_ref, e_ref, tri_ref, h_scr, yb_ref, 1)
    if not emit_y:
        @pl.when(c == pl.num_programs(1) - 1)
        def _():
            hout_ref[0] = h_scr[...]


def _ssd(u, dtp, h0, bias, a_neg, e_mat, tri, *, batch, seq, q):
    t = u.shape[0]
    nc = seq // q
    emit_y = h0 is not None
    hshape = (N_DIRS, SSD_GROUPS, SSD_STATE, SSD_HEADS_PER_GROUP * SSD_HEAD_DIM)
    fwd = lambda b, c: (b * nc + c, 0)
    bwd = lambda b, c: (b * nc + nc - 1 - c, 0)
    const2 = lambda b, c: (0, 0)
    const3 = lambda b, c: (0, 0, 0)
    in_specs = [pl.BlockSpec((q, XBC_WIDTH), fwd), pl.BlockSpec((q, XBC_WIDTH), bwd),
                pl.BlockSpec((q, LANES), fwd), pl.BlockSpec((q, LANES), bwd)]
    args = [u, u, dtp, dtp]
    if emit_y:
        in_specs.append(pl.BlockSpec((1,) + hshape, lambda b, c: (b, 0, 0, 0, 0)))
        args.append(h0)
    in_specs += [pl.BlockSpec((1, LANES), const2), pl.BlockSpec((1, LANES), const2),
                 pl.BlockSpec((N_DIRS, LANES, SSD_WIDTH), const3),
                 pl.BlockSpec((N_DIRS, q, q), const3)]
    args += [bias, a_neg, e_mat, tri]
    if emit_y:
        out_specs = [pl.BlockSpec((q, SSD_WIDTH), fwd), pl.BlockSpec((q, SSD_WIDTH), bwd)]
        out_shape = [jax.ShapeDtypeStruct((t, SSD_WIDTH), F32)] * 2
    else:
        out_specs = pl.BlockSpec((1,) + hshape, lambda b, c: (b, 0, 0, 0, 0))
        out_shape = jax.ShapeDtypeStruct((batch,) + hshape, F32)
    return pl.pallas_call(
        functools.partial(_ssd_kernel, emit_y=emit_y),
        grid=(batch, nc),
        in_specs=in_specs,
        out_specs=out_specs,
        out_shape=out_shape,
        scratch_shapes=[pltpu.VMEM(hshape, F32)],
        compiler_params=_cparams(("arbitrary", "arbitrary"), 48),
        name="ssd_lat" if emit_y else "ssd_ctx",
    )(*args)


def _outproj_kernel(attn_ref, yf_ref, yb_ref, xs_ref, z_ref, x_ref, w_ref, skip_ref, ng_ref,
                    g1_ref, n2_ref, sc2_ref, sh2_ref, rw_ref, x1_ref, h2_ref, lg_ref):
    y = yf_ref[...] + yb_ref[...] + skip_ref[...] * xs_ref[...].astype(F32)
    y = y * _silu(z_ref[...].astype(F32))
    y = (_rms(y) * ng_ref[...]).astype(BF16)
    m = _dot(attn_ref[...], w_ref[0:ATTN_WIDTH, :]) + _dot(y, w_ref[ATTN_WIDTH:, :])
    x1 = x_ref[...] + g1_ref[0] * m
    x1_ref[...] = x1
    h2 = (_rms(x1) * n2_ref[...]) * (1.0 + sc2_ref[0]) + sh2_ref[0]
    _store_rows(h2_ref, h2, _chunks(h2.shape[1]))
    lg_ref[...] = _dot_f32(rw_ref[...], h2, _NT)


def _outproj(attn, yf, yb, u, main, x, w_out, skip_e, ssd_g, g1, n2, sc2, sh2, rw_t, *, seq):
    t, d = x.shape
    tm = TOK_TILE
    per = seq // tm
    row = lambda i: (i, 0)
    c2 = lambda i: (0, 0)
    bat = lambda i: (i // per, 0, 0)
    in_specs = [pl.BlockSpec((tm, ATTN_WIDTH), row),
                pl.BlockSpec((tm, SSD_WIDTH), row),
                pl.BlockSpec((tm, SSD_WIDTH), row),
                pl.BlockSpec((tm, SSD_WIDTH), row),
                pl.BlockSpec((tm, SSD_WIDTH), lambda i: (i, COL_Z // SSD_WIDTH)),
                pl.BlockSpec((tm, d), row),
                pl.BlockSpec((ATTN_WIDTH + SSD_WIDTH, d), c2),
                pl.BlockSpec((1, SSD_WIDTH), c2),
                pl.BlockSpec((1, SSD_WIDTH), c2),
                pl.BlockSpec((1, 1, d), bat),
                pl.BlockSpec((1, d), c2),
                pl.BlockSpec((1, 1, d), bat),
                pl.BlockSpec((1, 1, d), bat),
                pl.BlockSpec((N_EXPERTS, d), c2)]
    return pl.pallas_call(
        _outproj_kernel,
        grid=(t // tm,),
        in_specs=in_specs,
        out_specs=[pl.BlockSpec((tm, d), row), pl.BlockSpec((tm * _chunks(d), LANES), row),
                   pl.BlockSpec((N_EXPERTS, tm), lambda i: (0, i))],
        out_shape=[jax.ShapeDtypeStruct((t, d), F32),
                   jax.ShapeDtypeStruct((t * _chunks(d), LANES), U32),
                   jax.ShapeDtypeStruct((N_EXPERTS, t), F32)],
        compiler_params=_cparams(("arbitrary",), 52),
        name="outproj",
    )(attn, yf, yb, u, main, x, w_out, skip_e, ssd_g, g1, n2, sc2, sh2, rw_t)


def _first_index(hit, iota, size):
    return jnp.min(jnp.where(hit, iota, size), axis=0, keepdims=True)


def _route_kernel(lg_ref, bias_ref, su_ref, idx_ref, w_ref, rank_ref, cnt_ref, carry_scr):
    i = pl.program_id(0)
    tm = lg_ref.shape[1]

    @pl.when(i == 0)
    def _():
        carry_scr[...] = jnp.zeros(carry_scr.shape, F32)

    scores = _sigmoid(lg_ref[...])
    sel = scores + bias_ref[...][:, 0:1]
    e8 = EXPERTS_PER_GROUP
    i8 = lax.broadcasted_iota(I32, (e8, tm), 0).astype(F32)
    gs = []
    for g in range(N_EXPERT_GROUPS):
        blk = sel[g * e8:(g + 1) * e8, :]
        m1 = jnp.max(blk, axis=0, keepdims=True)
        f1 = _first_index(blk == m1, i8, e8)
        m2 = jnp.max(jnp.where(i8 == f1, NEG_BIG, blk), axis=0, keepdims=True)
        gs.append(m1 + m2)
    gsc = jnp.concatenate(gs, axis=0)
    ig = lax.broadcasted_iota(I32, gsc.shape, 0).astype(F32)
    keep = jnp.zeros(gsc.shape, F32)
    for _ in range(TOPK_GROUPS):
        m = jnp.max(gsc, axis=0, keepdims=True)
        f = _first_index(gsc == m, ig, N_EXPERT_GROUPS)
        keep = jnp.where(ig == f, 1.0, keep)
        gsc = jnp.where(ig == f, NEG_BIG, gsc)
    keep_e = jnp.concatenate(
        [jnp.broadcast_to(keep[g:g + 1, :], (e8, tm)) for g in range(N_EXPERT_GROUPS)], axis=0)
    cand = jnp.where(keep_e > 0.5, sel, NEG_BIG)
    ie = lax.broadcasted_iota(I32, cand.shape, 0).astype(F32)
    chosen = jnp.zeros(cand.shape, F32)
    idxs, svals, hits = [], [], []
    for _ in range(TOP_K):
        m = jnp.max(cand, axis=0, keepdims=True)
        f = _first_index(cand == m, ie, N_EXPERTS)
        hit = ie == f
        idxs.append(f)
        svals.append(jnp.sum(jnp.where(hit, scores, 0.0), axis=0, keepdims=True))
        hits.append(hit)
        chosen = jnp.where(hit, 1.0, chosen)
        cand = jnp.where(hit, NEG_BIG, cand)
    s_all = jnp.concatenate(svals, axis=0)
    idx_ref[...] = jnp.concatenate(idxs, axis=0).astype(I32)
    w_ref[...] = ROUTED_SCALE * s_all / jnp.sum(s_all, axis=0, keepdims=True)
    before = _dot(chosen.astype(BF16), su_ref[...]) + carry_scr[...][:, 0:1]
    ranks = [jnp.sum(jnp.where(h, before, 0.0), axis=0, keepdims=True) for h in hits]
    rank_ref[...] = jnp.concatenate(ranks, axis=0).astype(I32)
    carry_scr[...] = carry_scr[...] + jnp.sum(chosen, axis=1, keepdims=True)
    cnt_ref[...] = carry_scr[...].astype(I32)


def _route(logits_t, bias_col):
    e, t = logits_t.shape
    tm = ROUTE_TILE
    su = jnp.triu(jnp.ones((tm, tm), F32), 1).astype(BF16)
    col = lambda i: (0, i)
    return pl.pallas_call(
        _route_kernel,
        grid=(t // tm,),
        in_specs=[pl.BlockSpec((e, tm), col),
                  pl.BlockSpec((e, LANES), lambda i: (0, 0)),
                  pl.BlockSpec((tm, tm), lambda i: (0, 0))],
        out_specs=[pl.BlockSpec((TOP_K, tm), col), pl.BlockSpec((TOP_K, tm), col),
                   pl.BlockSpec((TOP_K, tm), col), pl.BlockSpec((e, LANES), lambda i: (0, 0))],
        out_shape=[jax.ShapeDtypeStruct((TOP_K, t), I32), jax.ShapeDtypeStruct((TOP_K, t), F32),
                   jax.ShapeDtypeStruct((TOP_K, t), I32), jax.ShapeDtypeStruct((e, LANES), I32)],
        scratch_shapes=[pltpu.VMEM((e, LANES), F32)],
        compiler_params=_cparams(("arbitrary",), 40),
        name="route",
    )(logits_t, bias_col, su)


def _slots_kernel(idx_ref, rank_ref, ps_ref, d_ref, dc_ref):
    tm = idx_ref.shape[1]
    ie = lax.broadcasted_iota(I32, (N_EXPERTS, tm), 0)
    ps = ps_ref[...][:, 0:1]
    rows = []
    for k in range(TOP_K):
        hit = ie == idx_ref[k:k + 1, :]
        rows.append(jnp.sum(jnp.where(hit, ps, 0.0), axis=0, keepdims=True))
    dest = jnp.concatenate(rows, axis=0).astype(I32) + rank_ref[...]
    d_ref[0] = dest
    tc = dc_ref.shape[2]
    for s in range(tm // tc):
        dc_ref[s] = dest[:, s * tc:(s + 1) * tc]


def _slots(idx_t, rank_t, ps_col):
    t = idx_t.shape[1]
    tm = TOK_TILE
    tc = COMB_TILE
    col = lambda i: (0, i)
    return pl.pallas_call(
        _slots_kernel,
        grid=(t // tm,),
        in_specs=[pl.BlockSpec((TOP_K, tm), col), pl.BlockSpec((TOP_K, tm), col),
                  pl.BlockSpec((N_EXPERTS, LANES), lambda i: (0, 0))],
        out_specs=[pl.BlockSpec((1, TOP_K, tm), lambda i: (i, 0, 0)),
                   pl.BlockSpec((tm // tc, TOP_K, tc), lambda i: (i, 0, 0))],
        out_shape=[jax.ShapeDtypeStruct((t // tm, TOP_K, tm), I32),
                   jax.ShapeDtypeStruct((t // tc, TOP_K, tc), I32)],
        compiler_params=_cparams(("arbitrary",), 40),
        name="slots",
    )(idx_t, rank_t, ps_col)


def _row_copy(src, dst, sem):
    return pltpu.make_async_copy(src, dst, sem)


def _dispatch_kernel(pend_ref, dest_ref, h_ref, xs_ref, zero_scr, sem, *, nch):
    i = pl.program_id(0)
    tm = h_ref.shape[0] // nch
    tile_rows = zero_scr.shape[0]

    @pl.when(i == 0)
    def _():
        zero_scr[...] = jnp.zeros(zero_scr.shape, zero_scr.dtype)

        def pad_copy(e):
            start_row = pl.multiple_of(pend_ref[e + 1] * nch - tile_rows, tile_rows)
            return _row_copy(zero_scr, xs_ref.at[pl.ds(start_row, tile_rows), :], sem.at[0])

        def start(e, carry):
            @pl.when(pend_ref[e + 1] > pend_ref[e])
            def _():
                pad_copy(e).start()
            return carry

        def wait(e, carry):
            @pl.when(pend_ref[e + 1] > pend_ref[e])
            def _():
                pad_copy(e).wait()
            return carry

        lax.fori_loop(0, N_EXPERTS, start, 0)
        lax.fori_loop(0, N_EXPERTS, wait, 0)

    def tok_copy(t, k):
        src = pl.multiple_of(t * nch, nch)
        dst = pl.multiple_of(dest_ref[0, k, t] * nch, nch)
        return _row_copy(h_ref.at[pl.ds(src, nch), :], xs_ref.at[pl.ds(dst, nch), :], sem.at[1])

    def start(t, carry):
        for k in range(TOP_K):
            tok_copy(t, k).start(priority=k % 2)
        return carry

    def wait(t, carry):
        for k in range(TOP_K):
            tok_copy(t, k).wait()
        return carry

    lax.fori_loop(0, tm, start, 0)
    lax.fori_loop(0, tm, wait, 0)


def _dispatch(pad_end_ext, dest3, h2, n_slots, nch):
    t = h2.shape[0] // nch
    tm = TOK_TILE
    grid_spec = pltpu.PrefetchScalarGridSpec(
        num_scalar_prefetch=1,
        grid=(t // tm,),
        in_specs=[pl.BlockSpec((1, TOP_K, tm), lambda i, pe: (i, 0, 0), memory_space=pltpu.SMEM),
                  pl.BlockSpec((tm * nch, LANES), lambda i, pe: (i, 0))],
        out_specs=pl.BlockSpec(memory_space=pl.ANY),
        scratch_shapes=[pltpu.VMEM((MOE_TILE * nch, LANES), U32), pltpu.SemaphoreType.DMA((2,))],
    )
    return pl.pallas_call(
        functools.partial(_dispatch_kernel, nch=nch),
        grid_spec=grid_spec,
        out_shape=jax.ShapeDtypeStruct((n_slots * nch, LANES), U32),
        compiler_params=_cparams(("arbitrary",), 40),
        name="dispatch",
    )(pad_end_ext, dest3, h2)


def _experts_kernel(first_ref, ntile_ref, xs_ref, wg_ref, wu_ref, wd_ref, ys_ref,
                    wg_scr, wu_scr, wd_scr, wg_buf, wu_buf, wd_buf, xbuf, ybuf, xsem, ysem, wsem):
    e = pl.program_id(0)
    n = ntile_ref[e]
    first = first_ref[e]
    rows = xbuf.shape[1]
    nch = _chunks(wg_scr.shape[0])
    te = rows // nch
    wslot = e % 2

    has_next = e + 1 < pl.num_programs(0)

    def w_copies(ex, s, c):
        out = []
        for i, (src, dst) in enumerate(((wg_ref, wg_buf), (wu_ref, wu_buf), (wd_ref, wd_buf))):
            dr = dst.shape[1] // WEIGHT_CHUNKS
            r0 = pl.multiple_of(c * dr, dr)
            out.append(_row_copy(src.at[ex, pl.ds(r0, dr), :], dst.at[s, pl.ds(r0, dr), :],
                                 wsem.at[s, i]))
        return out

    def start_next_chunk(c):
        @pl.when(has_next & (c < WEIGHT_CHUNKS))
        def _():
            for cp in w_copies(e + 1, 1 - wslot, c):
                cp.start()

    @pl.when(e == 0)
    def _():
        for c in range(WEIGHT_CHUNKS):
            for cp in w_copies(0, 0, c):
                cp.start()

    nx = xbuf.shape[0]
    ny = ybuf.shape[0]

    def x_copy(j):
        r0 = pl.multiple_of((first + j) * rows, rows)
        return _row_copy(xs_ref.at[pl.ds(r0, rows), :], xbuf.at[j % nx], xsem.at[j % nx])

    def y_copy(j):
        r0 = pl.multiple_of((first + j) * rows, rows)
        return _row_copy(ybuf.at[j % ny], ys_ref.at[pl.ds(r0, rows), :], ysem.at[j % ny])

    for j0 in range(nx - 1):
        @pl.when(j0 < n)
        def _():
            x_copy(j0).start(priority=TILE_DMA_PRIORITY)

    for c in range(WEIGHT_CHUNKS):
        for cp in w_copies(e, wslot, c):
            cp.wait()
    wg_scr[...] = wg_buf[wslot].astype(BF16)
    wu_scr[...] = wu_buf[wslot].astype(BF16)
    wd_scr[...] = wd_buf[wslot].astype(BF16)

    @pl.when(n > 0)
    def _():

        def tile(j, carry):
            @pl.when(j + nx - 1 < n)
            def _():
                x_copy(j + nx - 1).start(priority=TILE_DMA_PRIORITY)

            start_next_chunk(j)
            x_copy(j).wait()

            @pl.when(j >= ny)
            def _():
                y_copy(j - ny).wait()

            xb = _load_rows(xbuf.at[j % nx], te, nch).astype(BF16)
            act = _silu(_dot(xb, wg_scr[...])) * _dot(xb, wu_scr[...])
            _store_rows(ybuf.at[j % ny], _dot(act.astype(BF16), wd_scr[...]), nch)
            y_copy(j).start(priority=TILE_DMA_PRIORITY)
            return carry

        lax.fori_loop(0, n, tile, 0)

        for back in range(ny, 0, -1):
            @pl.when(n >= back)
            def _():
                y_copy(n - back).wait()

    def rest(c, carry):
        start_next_chunk(c)
        return carry

    lax.fori_loop(jnp.minimum(n, WEIGHT_CHUNKS), WEIGHT_CHUNKS, rest, 0)


def _experts(first_tile, n_tile, xs, wg, wu, wd):
    n_exp, d, f = wg.shape
    nch = _chunks(d)
    rows = MOE_TILE * nch
    grid_spec = pltpu.PrefetchScalarGridSpec(
        num_scalar_prefetch=2,
        grid=(n_exp,),
        in_specs=[pl.BlockSpec(memory_space=pl.ANY), pl.BlockSpec(memory_space=pl.ANY),
                  pl.BlockSpec(memory_space=pl.ANY), pl.BlockSpec(memory_space=pl.ANY)],
        out_specs=pl.BlockSpec(memory_space=pl.ANY),
        scratch_shapes=[pltpu.VMEM((d, f), BF16), pltpu.VMEM((d, f), BF16), pltpu.VMEM((f, d), BF16),
                        pltpu.VMEM((2, d, f), F32), pltpu.VMEM((2, d, f), F32),
                        pltpu.VMEM((2, f, d), F32),
                        pltpu.VMEM((3, rows, LANES), U32), pltpu.VMEM((2, rows, LANES), U32),
                        pltpu.SemaphoreType.DMA((3,)), pltpu.SemaphoreType.DMA((2,)),
                        pltpu.SemaphoreType.DMA((2, 3))],
    )
    return pl.pallas_call(
        _experts_kernel,
        grid_spec=grid_spec,
        out_shape=jax.ShapeDtypeStruct(xs.shape, U32),
        compiler_params=_cparams(("arbitrary",), 56),
        name="experts",
    )(first_tile, n_tile, xs, wg, wu, wd)


def _combine_kernel(dest_ref, dnext_ref, ys_ref, wt_ref, h_ref, x1_ref, sg_ref, su_ref, sd_ref,
                    g2_ref, fg_ref, eye_ref, o_ref, buf, sem):
    i = pl.program_id(0)
    n = pl.num_programs(0)
    tm = x1_ref.shape[0]
    nch = _chunks(x1_ref.shape[1])
    slot = i % 2

    def row_copy(d_ref, s, t, k):
        src = pl.multiple_of(d_ref[0, k, t] * nch, nch)
        dst = pl.multiple_of(t * nch, nch)
        return _row_copy(ys_ref.at[pl.ds(src, nch), :], buf.at[s, k, pl.ds(dst, nch), :], sem.at[s])

    def start_tile(d_ref, s):
        def body(t, carry):
            for k in range(TOP_K):
                row_copy(d_ref, s, t, k).start(priority=k % 2)
            return carry
        lax.fori_loop(0, tm, body, 0)

    @pl.when(i == 0)
    def _():
        start_tile(dest_ref, 0)

    @pl.when(i + 1 < n)
    def _():
        start_tile(dnext_ref, 1 - slot)

    hb = _load_rows(h_ref, tm, nch).astype(BF16)
    act = _silu(_dot(hb, sg_ref[...])) * _dot(hb, su_ref[...])
    ffn = _dot(act.astype(BF16), sd_ref[...])
    w_col = _dot_rhs_f32(eye_ref[...], wt_ref[...], _NT)

    def wait(t, carry):
        for k in range(TOP_K):
            row_copy(dest_ref, slot, t, k).wait()
        return carry

    lax.fori_loop(0, tm, wait, 0)
    for k in range(TOP_K):
        ffn = ffn + w_col[:, k:k + 1] * _load_rows(buf.at[slot, k], tm, nch)
    x2 = x1_ref[...] + g2_ref[0] * ffn
    o_ref[...] = _rms(x2) * fg_ref[...]


def _combine(dest3, ys, w_t, h2, x1, sg, su, sd, g2, fg, *, seq):
    t, d = x1.shape
    nch = _chunks(d)
    tm = COMB_TILE
    per = seq // tm
    f = sg.shape[1]
    eye = jnp.eye(tm, dtype=BF16)
    row = lambda i: (i, 0)
    c2 = lambda i: (0, 0)
    last = t // tm - 1
    return pl.pallas_call(
        _combine_kernel,
        grid=(t // tm,),
        in_specs=[pl.BlockSpec((1, TOP_K, tm), lambda i: (i, 0, 0), memory_space=pltpu.SMEM),
                  pl.BlockSpec((1, TOP_K, tm), lambda i: (jnp.minimum(i + 1, last), 0, 0),
                               memory_space=pltpu.SMEM),
                  pl.BlockSpec(memory_space=pl.ANY),
                  pl.BlockSpec((TOP_K, tm), lambda i: (0, i)),
                  pl.BlockSpec((tm * nch, LANES), row),
                  pl.BlockSpec((tm, d), row),
                  pl.BlockSpec((d, f), c2), pl.BlockSpec((d, f), c2), pl.BlockSpec((f, d), c2),
                  pl.BlockSpec((1, 1, d), lambda i: (i // per, 0, 0)),
                  pl.BlockSpec((1, d), c2),
                  pl.BlockSpec((tm, tm), c2)],
        out_specs=pl.BlockSpec((tm, d), row),
        out_shape=jax.ShapeDtypeStruct((t, d), F32),
        scratch_shapes=[pltpu.VMEM((2, TOP_K, tm * nch, LANES), U32),
                        pltpu.SemaphoreType.DMA((2,))],
        compiler_params=_cparams(("arbitrary",), 48),
        name="combine",
    )(dest3, dest3, ys, w_t, h2, x1, sg, su, sd, g2, fg, eye)


def _rope_tables(seq):
    n_rows = seq // GRID_W
    rows = np.repeat(np.arange(n_rows, dtype=np.float32), GRID_W)
    cols = np.tile(np.arange(GRID_W, dtype=np.float32), n_rows)
    expo = -np.arange(0, ROPE_AXIS_DIM, 2, dtype=np.float32) / np.float32(ROPE_AXIS_DIM)
    inv = np.power(np.float32(ROPE_THETA), expo).astype(np.float32)
    ar = (rows[:, None] * inv[None, :]).astype(np.float64)
    ac = (cols[:, None] * inv[None, :]).astype(np.float64)
    cos = np.concatenate([np.cos(ar), np.cos(ac), np.cos(ar), np.cos(ac)], axis=1)
    sin = np.concatenate([-np.sin(ar), -np.sin(ac), np.sin(ar), np.sin(ac)], axis=1)
    return jnp.asarray(cos, F32), jnp.asarray(sin, F32)


def _layer(x, c, ctx, c_ctx, w_ada, b_ada, norm1_g, norm2_g, w_in, attn_sink, conv_w, conv_b,
           dt_bias, a_log, d_skip, ssd_norm_g, w_out, router_w, router_bias, wg, wu, wd,
           sw_gate, sw_up, sw_down, final_norm_g):
    batch, seq, d = x.shape
    ctx_len = ctx.shape[1]
    t = batch * seq

    cvec = jnp.zeros((SUBLANES, d), F32).at[:batch].set(c).at[batch].set(c_ctx)
    mod = _ada(cvec, w_ada, b_ada[None, :])
    sh1, sc1, g1, sh2, sc2, g2 = [mod[:batch, i * d:(i + 1) * d][:, None, :] for i in range(6)]
    sh1c, sc1c = [jnp.broadcast_to(mod[batch, i * d:(i + 1) * d][None, None, :], (batch, 1, d))
                  for i in range(2)]

    w_main, w_dt = _wprep(w_in.T)
    cos, sin = _rope_tables(seq)
    n1 = norm1_g[None, :]
    n_main = MAIN_WIDTH // IN_TILE_N
    ctx0 = COL_XBC // IN_TILE_N
    main, dtp = _inproj(x.reshape(t, d), n1, sc1, sh1, w_main, w_dt, cos, sin,
                        seq=seq, tm=min(seq, 1024), rope=True, tile0=0, n_tiles=n_main)
    main_c, dtp_c = _inproj(ctx.reshape(batch * ctx_len, d), n1, sc1c, sh1c, w_main, w_dt, None, None,
                            seq=ctx_len, tm=ctx_len, rope=False, tile0=ctx0, n_tiles=n_main - ctx0)

    attn = _attn(main, main_c, attn_sink, batch=batch, seq=seq, ctx_len=ctx_len)

    cw8 = jnp.pad(conv_w, ((0, SUBLANES - SSD_CONV), (0, 0)))
    u = _conv(main, COL_XBC, cw8, conv_b[None, :], seq=seq)
    u_c = _conv(main_c, 0, cw8, conv_b[None, :], seq=ctx_len)
    nd = N_DIRS * SSD_HEADS
    bias = jnp.pad(dt_bias.reshape(1, nd), ((0, 0), (0, LANES - nd)))
    a_neg = jnp.pad(-jnp.exp(a_log.reshape(1, nd)), ((0, 0), (0, LANES - nd)))
    lane_head = np.arange(SSD_WIDTH) // SSD_HEAD_DIM
    e_np = np.zeros((N_DIRS, LANES, SSD_WIDTH), np.float32)
    for dd in range(N_DIRS):
        e_np[dd, dd * SSD_HEADS + lane_head, np.arange(SSD_WIDTH)] = 1.0
    e_mat = jnp.asarray(e_np, BF16)

    def tri(q):
        lo = np.tril(np.ones((q, q), np.float32))
        return jnp.asarray(np.stack([lo, lo.T]), BF16)

    qc = min(SSD_CHUNK, ctx_len)
    h_ctx = _ssd(u_c, dtp_c, None, bias, a_neg, e_mat, tri(qc), batch=batch, seq=ctx_len, q=qc)
    ql = min(SSD_CHUNK, seq)
    yf, yb = _ssd(u, dtp, h_ctx, bias, a_neg, e_mat, tri(ql), batch=batch, seq=seq, q=ql)

    skip_e = jnp.repeat(d_skip, SSD_HEAD_DIM)[None, :]
    x1, h2, logits_t = _outproj(attn, yf, yb, u, main, x.reshape(t, d), w_out.astype(BF16),
                                skip_e, ssd_norm_g[None, :], g1, norm2_g[None, :], sc2, sh2,
                                router_w.T, seq=seq)

    bias_col = jnp.broadcast_to(router_bias[:, None], (N_EXPERTS, LANES))
    idx_t, w_t, rank_t, cnt = _route(logits_t, bias_col)
    counts = cnt[:, 0]
    te = MOE_TILE
    padded = (counts + te - 1) // te * te
    pad_end = jnp.cumsum(padded)
    pad_start = pad_end - padded
    n_slots = -(-(t * TOP_K + N_EXPERTS * (te - 1)) // te) * te
    ps_col = jnp.broadcast_to(pad_start.astype(F32)[:, None], (N_EXPERTS, LANES))
    dest3, dest3c = _slots(idx_t, rank_t, ps_col)
    pad_end_ext = jnp.concatenate([jnp.zeros((1,), I32), pad_end.astype(I32)])

    xs = _dispatch(pad_end_ext, dest3, h2, n_slots, _chunks(d))
    ys = _experts((pad_start // te).astype(I32), (padded // te).astype(I32), xs, wg, wu, wd)
    out = _combine(dest3c, ys, w_t, h2, x1, sw_gate.astype(BF16), sw_up.astype(BF16),
                   sw_down.astype(BF16), g2, final_norm_g[None, :], seq=seq)
    return out.reshape(batch, seq, d)


def kernel(x, c, ctx, c_ctx, w_ada, b_ada, norm1_g, norm2_g, w_in, attn_sink, conv_w, conv_b, dt_bias, a_log, d_skip, ssd_norm_g, w_out, router_w, router_bias, expert_w_gate, expert_w_up, expert_w_down, shared_w_gate, shared_w_up, shared_w_down, final_norm_g):
    assert w_ada.shape[0] == 1, "single-layer stack only"
    return _layer(x, c, ctx, c_ctx, w_ada[0], b_ada[0], norm1_g[0], norm2_g[0], w_in[0],
                  attn_sink[0], conv_w[0], conv_b[0], dt_bias[0], a_log[0], d_skip[0],
                  ssd_norm_g[0], w_out[0], router_w[0], router_bias[0], expert_w_gate[0],
                  expert_w_up[0], expert_w_down[0], shared_w_gate[0], shared_w_up[0],
                  shared_w_down[0], final_norm_g)
```

```python
import functools
import math

import jax
import jax.numpy as jnp
import numpy as np
from jax import lax
from jax.experimental import pallas as pl
from jax.experimental.pallas import tpu as pltpu

F32 = jnp.float32
BF16 = jnp.bfloat16
I32 = jnp.int32

EPS = 1e-6
GRID_W = 64
N_Q_HEADS = 8
N_KV_HEADS = 2
Q_PER_KV = N_Q_HEADS // N_KV_HEADS
HEAD_DIM = 128
ATTN_WIDTH = N_Q_HEADS * HEAD_DIM
KV_WIDTH = N_KV_HEADS * HEAD_DIM
WINDOW = 128
ATTN_BLOCK = 128
ROPE_THETA = 10000.0
ROPE_AXIS_DIM = HEAD_DIM // 2
SSD_HEADS = 16
SSD_HEAD_DIM = 64
SSD_WIDTH = SSD_HEADS * SSD_HEAD_DIM
SSD_GROUPS = 4
SSD_HEADS_PER_GROUP = SSD_HEADS // SSD_GROUPS
SSD_STATE = 128
SSD_CONV = 5
N_DIRS = 2
XBC_WIDTH = SSD_WIDTH + 2 * SSD_GROUPS * SSD_STATE
N_EXPERTS = 64
N_EXPERT_GROUPS = 8
EXPERTS_PER_GROUP = N_EXPERTS // N_EXPERT_GROUPS
TOPK_GROUPS = 4
TOP_K = 8
EXPERT_DIM = 512
ROUTED_SCALE = 2.5

LANES = 128
SUBLANES = 8
NEG_BIG = -1e30

COL_Q = 0
COL_Z = ATTN_WIDTH
COL_XBC = COL_Z + SSD_WIDTH
COL_K = COL_XBC + XBC_WIDTH
COL_V = COL_K + KV_WIDTH
MAIN_WIDTH = COL_V + KV_WIDTH

IN_TILE_N = 512
SSD_CHUNK = 256
MOE_TILE = 256
ROUTE_TILE = 512
TOK_TILE = 256
COMB_TILE = 128
TILE_DMA_PRIORITY = 1
WEIGHT_CHUNKS = 8

_NT = (((1,), (1,)), ((), ()))
_TN = (((0,), (0,)), ((), ()))


def _cparams(sem, vmem_mb):
    return pltpu.CompilerParams(dimension_semantics=sem, vmem_limit_bytes=vmem_mb * 1024 * 1024)


def _dot(a, b, dims=None):
    if dims is None:
        return jnp.dot(a, b, preferred_element_type=F32)
    return lax.dot_general(a, b, dims, preferred_element_type=F32)


def _split(a):
    hi = a.astype(BF16)
    lo = (a - hi.astype(F32)).astype(BF16)
    return hi, lo


def _dot_lhs_f32(a, b_exact, dims=None):
    hi, lo = _split(a)
    return _dot(hi, b_exact, dims) + _dot(lo, b_exact, dims)


def _dot_rhs_f32(a_exact, b, dims=None):
    hi, lo = _split(b)
    return _dot(a_exact, hi, dims) + _dot(a_exact, lo, dims)


def _dot_f32(a, b, dims=None):
    ah, al = _split(a)
    bh, bl = _split(b)
    return _dot(ah, bh, dims) + (_dot(al, bh, dims) + _dot(ah, bl, dims))


def _sigmoid(x):
    return 1.0 / (1.0 + jnp.exp(-x))


def _silu(x):
    return x * _sigmoid(x)


def _rms(x):
    return x * lax.rsqrt(jnp.mean(x * x, axis=-1, keepdims=True) + EPS)


U32 = jnp.uint32
_HI_MASK = 0xFFFF0000


def _chunks(d):
    return d // (2 * LANES)


def _load_rows(ref, rows, nch):
    words = [ref[pl.ds(j, rows, stride=nch), :] for j in range(nch)]
    lo = [lax.bitcast_convert_type(w << 16, F32) for w in words]
    hi = [lax.bitcast_convert_type(w & jnp.uint32(_HI_MASK), F32) for w in words]
    return jnp.concatenate(lo + hi, axis=1)


def _store_rows(ref, val, nch):
    rows, d = val.shape
    bits = lax.bitcast_convert_type(val.astype(BF16).astype(F32), U32)
    for j in range(nch):
        lo = bits[:, j * LANES:(j + 1) * LANES] >> 16
        hi = bits[:, d // 2 + j * LANES:d // 2 + (j + 1) * LANES] & jnp.uint32(_HI_MASK)
        ref[pl.ds(j, rows, stride=nch), :] = lo | hi


def _ada_kernel(c_ref, w_ref, b_ref, o_ref):
    o_ref[...] = _dot_f32(_silu(c_ref[...]), w_ref[...]) + b_ref[...]


def _ada(cvec, w, b):
    m, d = cvec.shape
    n = w.shape[1]
    tn = 1024
    return pl.pallas_call(
        _ada_kernel,
        grid=(n // tn,),
        in_specs=[pl.BlockSpec((m, d), lambda j: (0, 0)),
                  pl.BlockSpec((d, tn), lambda j: (0, j)),
                  pl.BlockSpec((1, tn), lambda j: (0, j))],
        out_specs=pl.BlockSpec((m, tn), lambda j: (0, j)),
        out_shape=jax.ShapeDtypeStruct((m, n), F32),
        compiler_params=_cparams(("arbitrary",), 40),
        name="ada",
    )(cvec, w, b)


def _rope_mix(acc, cos, sin):
    heads = acc.shape[1] // HEAD_DIM
    partner = jnp.concatenate(
        [pltpu.roll(acc[:, h * HEAD_DIM:(h + 1) * HEAD_DIM], HEAD_DIM // 2, 1) for h in range(heads)],
        axis=1)
    return acc * cos + partner * sin


def _inproj_kernel(*refs, rope, n_rope_tiles, kv_tile):
    if rope:
        x_ref, g_ref, sc_ref, sh_ref, w_ref, wdt_ref, cos_ref, sin_ref, o_ref, dt_ref, h_scr = refs
    else:
        x_ref, g_ref, sc_ref, sh_ref, w_ref, wdt_ref, o_ref, dt_ref, h_scr = refs
    j = pl.program_id(1)

    @pl.when(j == 0)
    def _():
        h = (_rms(x_ref[...]) * g_ref[...]) * (1.0 + sc_ref[0]) + sh_ref[0]
        hb = h.astype(BF16)
        h_scr[...] = hb
        dt_ref[...] = _dot(hb, wdt_ref[...])

    acc = _dot(h_scr[...], w_ref[0])
    if not rope:
        o_ref[...] = acc.astype(o_ref.dtype)
        return
    reps = acc.shape[1] // HEAD_DIM

    @pl.when(j < n_rope_tiles)
    def _():
        cos = jnp.concatenate([cos_ref[...]] * reps, axis=1)
        sin = jnp.concatenate([sin_ref[...]] * reps, axis=1)
        o_ref[...] = _rope_mix(acc, cos, sin).astype(o_ref.dtype)

    @pl.when(j == kv_tile)
    def _():
        nk = KV_WIDTH // HEAD_DIM
        ones = jnp.ones_like(cos_ref[...])
        cos = jnp.concatenate([cos_ref[...]] * nk + [ones] * (reps - nk), axis=1)
        sin = jnp.concatenate([sin_ref[...]] * nk + [ones * 0.0] * (reps - nk), axis=1)
        o_ref[...] = _rope_mix(acc, cos, sin).astype(o_ref.dtype)

    @pl.when((j >= n_rope_tiles) & (j != kv_tile))
    def _():
        o_ref[...] = acc.astype(o_ref.dtype)


def _wprep_kernel(w_ref, wdt_ref, o_ref, odt_ref):
    j = pl.program_id(0)
    x = w_ref[...].T
    tn = x.shape[1]
    q = ROPE_AXIS_DIM // 2
    lane = lax.broadcasted_iota(I32, x.shape, 1)
    blk = (lane & (HEAD_DIM - 1)) // q
    swapped = jnp.where(blk == 1, pltpu.roll(x, tn - q, 1), jnp.where(blk == 2, pltpu.roll(x, q, 1), x))
    n_q = ATTN_WIDTH // tn
    kv = COL_K // tn

    @pl.when(j == 0)
    def _():
        rows = wdt_ref[...]
        pad = jnp.zeros((LANES - rows.shape[0], rows.shape[1]), rows.dtype)
        odt_ref[...] = jnp.concatenate([rows, pad], axis=0).T.astype(odt_ref.dtype)

    @pl.when(j < n_q)
    def _():
        o_ref[0] = swapped.astype(o_ref.dtype)

    @pl.when(j == kv)
    def _():
        o_ref[0] = jnp.where(lane < KV_WIDTH, swapped, x).astype(o_ref.dtype)

    @pl.when((j >= n_q) & (j != kv))
    def _():
        o_ref[0] = x.astype(o_ref.dtype)


def _wprep(w_in_t):
    d = w_in_t.shape[1]
    tn = IN_TILE_N
    n_q = ATTN_WIDTH // tn
    n_tiles = MAIN_WIDTH // tn
    src = lambda j: (jnp.where(j < n_q, j, jnp.where(j < n_tiles - 1, j + 1, n_q)), 0)
    n_dt = w_in_t.shape[0] - MAIN_WIDTH
    return pl.pallas_call(
        _wprep_kernel,
        grid=(n_tiles,),
        in_specs=[pl.BlockSpec((tn, d), src),
                  pl.BlockSpec((n_dt, d), lambda j: (MAIN_WIDTH // n_dt, 0))],
        out_specs=[pl.BlockSpec((1, d, tn), lambda j: (j, 0, 0)),
                   pl.BlockSpec((d, LANES), lambda j: (0, 0))],
        out_shape=[jax.ShapeDtypeStruct((n_tiles, d, tn), BF16),
                   jax.ShapeDtypeStruct((d, LANES), BF16)],
        compiler_params=_cparams(("arbitrary",), 40),
        name="wprep",
    )(w_in_t, w_in_t)


def _inproj(x, g, sc, sh, w, wdt, cos, sin, *, seq, tm, rope, tile0, n_tiles):
    t, d = x.shape
    tn = IN_TILE_N
    wn = n_tiles * tn
    per = seq // tm
    in_specs = [pl.BlockSpec((tm, d), lambda i, j: (i, 0)),
                pl.BlockSpec((1, d), lambda i, j: (0, 0)),
                pl.BlockSpec((1, 1, d), lambda i, j: (i // per, 0, 0)),
                pl.BlockSpec((1, 1, d), lambda i, j: (i // per, 0, 0)),
                pl.BlockSpec((1, d, tn), lambda i, j: (j + tile0, 0, 0)),
                pl.BlockSpec((d, LANES), lambda i, j: (0, 0))]
    args = [x, g, sc, sh, w, wdt]
    if rope:
        in_specs += [pl.BlockSpec((tm, HEAD_DIM), lambda i, j: (i % per, 0)),
                     pl.BlockSpec((tm, HEAD_DIM), lambda i, j: (i % per, 0))]
        args += [cos, sin]
    kern = functools.partial(_inproj_kernel, rope=rope, n_rope_tiles=ATTN_WIDTH // tn,
                             kv_tile=COL_K // tn)
    return pl.pallas_call(
        kern,
        grid=(t // tm, wn // tn),
        in_specs=in_specs,
        out_specs=[pl.BlockSpec((tm, tn), lambda i, j: (i, j)),
                   pl.BlockSpec((tm, LANES), lambda i, j: (i, 0))],
        out_shape=[jax.ShapeDtypeStruct((t, wn), BF16), jax.ShapeDtypeStruct((t, LANES), F32)],
        scratch_shapes=[pltpu.VMEM((tm, d), BF16)],
        compiler_params=_cparams(("arbitrary", "arbitrary"), 48),
        name="inproj_rope" if rope else "inproj",
    )(*args)


def _conv_kernel(x_ref, w_ref, b_ref, o_ref, pad_scr, *, seq, rows):
    halo = SUBLANES
    cw = x_ref.shape[1]
    pad_scr[0:halo, :] = jnp.zeros((halo, cw), F32)
    pad_scr[seq + halo:seq + 2 * halo, :] = jnp.zeros((halo, cw), F32)
    pad_scr[halo:seq + halo, :] = x_ref[...].astype(F32)
    w = w_ref[...]
    bias = b_ref[...]
    for r0 in range(0, seq, rows):
        acc = jnp.broadcast_to(bias, (rows, cw))
        for tap in range(SSD_CONV):
            start = r0 + halo - SSD_CONV // 2 + tap
            acc = acc + w[tap:tap + 1, :] * pad_scr[start:start + rows, :]
        o_ref[r0:r0 + rows, :] = _silu(acc).astype(o_ref.dtype)


def _conv(main, col0, w8, b, *, seq):
    t = main.shape[0]
    cw = 256
    rows = min(seq, 256)
    cb0 = col0 // cw
    kern = functools.partial(_conv_kernel, seq=seq, rows=rows)
    return pl.pallas_call(
        kern,
        grid=(t // seq, XBC_WIDTH // cw),
        in_specs=[pl.BlockSpec((seq, cw), lambda b_, c: (b_, cb0 + c)),
                  pl.BlockSpec((SUBLANES, cw), lambda b_, c: (0, c)),
                  pl.BlockSpec((1, cw), lambda b_, c: (0, c))],
        out_specs=pl.BlockSpec((seq, cw), lambda b_, c: (b_, c)),
        out_shape=jax.ShapeDtypeStruct((t, XBC_WIDTH), BF16),
        scratch_shapes=[pltpu.VMEM((seq + 2 * SUBLANES, cw), F32)],
        compiler_params=_cparams(("arbitrary", "arbitrary"), 40),
        name="conv",
    )(main, w8, b)


def _attn_kernel(sink_ref, q_ref, kp_ref, kc_ref, kn_ref, vp_ref, vc_ref, vn_ref,
                 kx_ref, vx_ref, o_ref, *, nb):
    qi = pl.program_id(1)
    blk = ATTN_BLOCK
    g = Q_PER_KV
    hd = HEAD_DIM
    scale = hd ** -0.5
    r = lax.broadcasted_iota(I32, (g * blk, 3 * blk), 0) & (blk - 1)
    c = lax.broadcasted_iota(I32, (g * blk, 3 * blk), 1)
    lo = jnp.maximum(r, jnp.where(qi == 0, blk, 0))
    hi = jnp.minimum(r + 2 * WINDOW, jnp.where(qi == nb - 1, 2 * blk - 1, 3 * blk - 1))
    valid = (c >= lo) & (c <= hi)
    row = lax.broadcasted_iota(I32, (g * blk, 1), 0)
    for hk in range(N_KV_HEADS):
        ks = slice(hk * hd, (hk + 1) * hd)
        q = jnp.concatenate([q_ref[:, (hk * g + i) * hd:(hk * g + i + 1) * hd] for i in range(g)],
                            axis=0)
        k_loc = jnp.concatenate([kp_ref[:, ks], kc_ref[:, ks], kn_ref[:, ks]], axis=0)
        v_loc = jnp.concatenate([vp_ref[:, ks], vc_ref[:, ks], vn_ref[:, ks]], axis=0)
        s_loc = _dot(q, k_loc, _NT) * scale
        s_ctx = _dot(q, kx_ref[:, ks], _NT) * scale
        s_loc = jnp.where(valid, s_loc, NEG_BIG)
        sink = jnp.zeros((g * blk, 1), F32)
        for i in range(g):
            sink = jnp.where((row >= i * blk) & (row < (i + 1) * blk), sink_ref[hk * g + i], sink)
        m = jnp.maximum(jnp.maximum(jnp.max(s_loc, axis=1, keepdims=True),
                                    jnp.max(s_ctx, axis=1, keepdims=True)), sink)
        p_loc = jnp.exp(s_loc - m)
        p_ctx = jnp.exp(s_ctx - m)
        den = (jnp.sum(p_loc, axis=1, keepdims=True) + jnp.sum(p_ctx, axis=1, keepdims=True)
               + jnp.exp(sink - m))
        o = _dot(p_loc.astype(BF16), v_loc) + _dot(p_ctx.astype(BF16), vx_ref[:, ks])
        o = o / den
        for i in range(g):
            o_ref[:, (hk * g + i) * hd:(hk * g + i + 1) * hd] = (
                o[i * blk:(i + 1) * blk, :].astype(o_ref.dtype))


def _attn(main, main_c, sink, *, batch, seq, ctx_len):
    t = main.shape[0]
    blk = ATTN_BLOCK
    nb = seq // blk
    kcol = COL_K // KV_WIDTH
    vcol = COL_V // KV_WIDTH
    kcol_c = XBC_WIDTH // KV_WIDTH
    vcol_c = kcol_c + 1

    def nbr(col, off):
        return pl.BlockSpec(
            (blk, KV_WIDTH), lambda b, i: (b * nb + jnp.clip(i + off, 0, nb - 1), col))

    in_specs = [pl.BlockSpec(memory_space=pltpu.SMEM),
                pl.BlockSpec((blk, ATTN_WIDTH), lambda b, i: (b * nb + i, 0)),
                nbr(kcol, -1), nbr(kcol, 0), nbr(kcol, 1),
                nbr(vcol, -1), nbr(vcol, 0), nbr(vcol, 1),
                pl.BlockSpec((ctx_len, KV_WIDTH), lambda b, i: (b, kcol_c)),
                pl.BlockSpec((ctx_len, KV_WIDTH), lambda b, i: (b, vcol_c))]
    return pl.pallas_call(
        functools.partial(_attn_kernel, nb=nb),
        grid=(batch, nb),
        in_specs=in_specs,
        out_specs=pl.BlockSpec((blk, ATTN_WIDTH), lambda b, i: (b * nb + i, 0)),
        out_shape=jax.ShapeDtypeStruct((t, ATTN_WIDTH), BF16),
        compiler_params=_cparams(("arbitrary", "arbitrary"), 40),
        name="attn",
    )(sink, main, main, main, main, main, main, main, main_c, main_c)


def _softplus(x):
    return jnp.maximum(x, 0.0) + jnp.log1p(jnp.exp(-jnp.abs(x)))


def _ssd_chunk(u_ref, dtp_ref, bias_ref, a_ref, e_ref, tri_ref, h_scr, y_ref, d):
    q = u_ref.shape[0]
    rev = d == 1
    p = SSD_HEAD_DIM
    gw = SSD_HEADS_PER_GROUP * p
    dt = _softplus(dtp_ref[...] + bias_ref[...])
    dta = dt * a_ref[...]
    tri = tri_ref[d]
    a_cum = _dot_rhs_f32(tri, dta)
    a_row = _dot_lhs_f32(dta.T, tri_ref[1 - d])
    edge = a_cum[0:1, :] if rev else a_cum[q - 1:q, :]
    ea = jnp.exp(a_cum)
    w_state = dt * jnp.exp(edge - a_cum)
    e_mat = e_ref[d]
    dt_e = _dot(dt.astype(BF16), e_mat)
    ws_e = _dot(w_state.astype(BF16), e_mat)
    x = u_ref[:, 0:SSD_WIDTH].astype(F32)
    xdd = (x * ws_e).astype(BF16)
    li = lax.broadcasted_iota(I32, (q, q), 0)
    si = lax.broadcasted_iota(I32, (q, q), 1)
    keep = (si >= li) if rev else (si <= li)
    ea_e = _dot_lhs_f32(ea, e_mat)
    edge_e = ea_e[0:1, :] if rev else ea_e[q - 1:q, :]
    if y_ref is not None:
        xd = (x * dt_e).astype(BF16)
    for g in range(SSD_GROUPS):
        b_t = u_ref[:, SSD_WIDTH + g * SSD_STATE:SSD_WIDTH + (g + 1) * SSD_STATE].T
        c0 = SSD_WIDTH + SSD_GROUPS * SSD_STATE + g * SSD_STATE
        c_g = u_ref[:, c0:c0 + SSD_STATE]
        h_t = h_scr[d, g]
        if y_ref is not None:
            cb = _dot(c_g, b_t)
            ys = []
            for r in range(SSD_HEADS_PER_GROUP):
                hd = g * SSD_HEADS_PER_GROUP + r
                ln = d * SSD_HEADS + hd
                seg = a_cum[:, ln:ln + 1] - a_row[ln:ln + 1, :]
                decay = jnp.exp(jnp.where(keep, seg, NEG_BIG))
                ys.append(_dot((cb * decay).astype(BF16), xd[:, hd * p:(hd + 1) * p]))
            y_off = _dot(c_g, h_t.astype(BF16)) * ea_e[:, g * gw:(g + 1) * gw]
            y_ref[:, g * gw:(g + 1) * gw] = jnp.concatenate(ys, axis=1) + y_off
        st = _dot(b_t, xdd[:, g * gw:(g + 1) * gw])
        h_scr[d, g] = h_t * edge_e[:, g * gw:(g + 1) * gw] + st


def _ssd_kernel(*refs, emit_y):
    if emit_y:
        (uf_ref, ub_ref, df_ref, db_ref, h0_ref, bias_ref, a_ref, e_ref, tri_ref,
         yf_ref, yb_ref, h_scr) = refs
    else:
        (uf_ref, ub_ref, df_ref, db_ref, bias_ref, a_ref, e_ref, tri_ref, hout_ref, h_scr) = refs
        yf_ref = yb_ref = None
    c = pl.program_id(1)

    @pl.when(c == 0)
    def _():
        if emit_y:
            h_scr[...] = h0_ref[0]
        else:
            h_scr[...] = jnp.zeros(h_scr.shape, F32)

    _ssd_chunk(uf_ref, df_ref, bias_ref, a_ref, e_ref, tri_ref, h_scr, yf_ref, 0)
    _ssd_chunk(ub_ref, db_ref, bias_ref, a_ref, e_ref, tri_ref, h_scr, yb_ref, 1)
    if not emit_y:
        @pl.when(c == pl.num_programs(1) - 1)
        def _():
            hout_ref[0] = h_scr[...]


def _ssd(u, dtp, h0, bias, a_neg, e_mat, tri, *, batch, seq, q):
    t = u.shape[0]
    nc = seq // q
    emit_y = h0 is not None
    hshape = (N_DIRS, SSD_GROUPS, SSD_STATE, SSD_HEADS_PER_GROUP * SSD_HEAD_DIM)
    fwd = lambda b, c: (b * nc + c, 0)
    bwd = lambda b, c: (b * nc + nc - 1 - c, 0)
    const2 = lambda b, c: (0, 0)
    const3 = lambda b, c: (0, 0, 0)
    in_specs = [pl.BlockSpec((q, XBC_WIDTH), fwd), pl.BlockSpec((q, XBC_WIDTH), bwd),
                pl.BlockSpec((q, LANES), fwd), pl.BlockSpec((q, LANES), bwd)]
    args = [u, u, dtp, dtp]
    if emit_y:
        in_specs.append(pl.BlockSpec((1,) + hshape, lambda b, c: (b, 0, 0, 0, 0)))
        args.append(h0)
    in_specs += [pl.BlockSpec((1, LANES), const2), pl.BlockSpec((1, LANES), const2),
                 pl.BlockSpec((N_DIRS, LANES, SSD_WIDTH), const3),
                 pl.BlockSpec((N_DIRS, q, q), const3)]
    args += [bias, a_neg, e_mat, tri]
    if emit_y:
        out_specs = [pl.BlockSpec((q, SSD_WIDTH), fwd), pl.BlockSpec((q, SSD_WIDTH), bwd)]
        out_shape = [jax.ShapeDtypeStruct((t, SSD_WIDTH), F32)] * 2
    else:
        out_specs = pl.BlockSpec((1,) + hshape, lambda b, c: (b, 0, 0, 0, 0))
        out_shape = jax.ShapeDtypeStruct((batch,) + hshape, F32)
    return pl.pallas_call(
        functools.partial(_ssd_kernel, emit_y=emit_y),
        grid=(batch, nc),
        in_specs=in_specs,
        out_specs=out_specs,
        out_shape=out_shape,
        scratch_shapes=[pltpu.VMEM(hshape, F32)],
        compiler_params=_cparams(("arbitrary", "arbitrary"), 48),
        name="ssd_lat" if emit_y else "ssd_ctx",
    )(*args)


def _outproj_kernel(attn_ref, yf_ref, yb_ref, xs_ref, z_ref, x_ref, w_ref, skip_ref, ng_ref,
                    g1_ref, n2_ref, sc2_ref, sh2_ref, rw_ref, x1_ref, h2_ref, lg_ref):
    y = yf_ref[...] + yb_ref[...] + skip_ref[...] * xs_ref[...].astype(F32)
    y = y * _silu(z_ref[...].astype(F32))
    y = (_rms(y) * ng_ref[...]).astype(BF16)
    m = _dot(attn_ref[...], w_ref[0:ATTN_WIDTH, :]) + _dot(y, w_ref[ATTN_WIDTH:, :])
    x1 = x_ref[...] + g1_ref[0] * m
    x1_ref[...] = x1
    h2 = (_rms(x1) * n2_ref[...]) * (1.0 + sc2_ref[0]) + sh2_ref[0]
    _store_rows(h2_ref, h2, _chunks(h2.shape[1]))
    logits = _dot_f32(h2, rw_ref[...])
    lg_ref[...] = logits.T[0:lg_ref.shape[0], :]


def _outproj(attn, yf, yb, u, main, x, w_out, skip_e, ssd_g, g1, n2, sc2, sh2, rw_t, *, seq):
    t, d = x.shape
    tm = TOK_TILE
    per = seq // tm
    row = lambda i: (i, 0)
    c2 = lambda i: (0, 0)
    bat = lambda i: (i // per, 0, 0)
    in_specs = [pl.BlockSpec((tm, ATTN_WIDTH), row),
                pl.BlockSpec((tm, SSD_WIDTH), row),
                pl.BlockSpec((tm, SSD_WIDTH), row),
                pl.BlockSpec((tm, SSD_WIDTH), row),
                pl.BlockSpec((tm, SSD_WIDTH), lambda i: (i, COL_Z // SSD_WIDTH)),
                pl.BlockSpec((tm, d), row),
                pl.BlockSpec((ATTN_WIDTH + SSD_WIDTH, d), c2),
                pl.BlockSpec((1, SSD_WIDTH), c2),
                pl.BlockSpec((1, SSD_WIDTH), c2),
                pl.BlockSpec((1, 1, d), bat),
                pl.BlockSpec((1, d), c2),
                pl.BlockSpec((1, 1, d), bat),
                pl.BlockSpec((1, 1, d), bat),
                pl.BlockSpec((d, LANES), c2)]
    return pl.pallas_call(
        _outproj_kernel,
        grid=(t // tm,),
        in_specs=in_specs,
        out_specs=[pl.BlockSpec((tm, d), row), pl.BlockSpec((tm * _chunks(d), LANES), row),
                   pl.BlockSpec((N_EXPERTS, tm), lambda i: (0, i))],
        out_shape=[jax.ShapeDtypeStruct((t, d), F32),
                   jax.ShapeDtypeStruct((t * _chunks(d), LANES), U32),
                   jax.ShapeDtypeStruct((N_EXPERTS, t), F32)],
        compiler_params=_cparams(("arbitrary",), 52),
        name="outproj",
    )(attn, yf, yb, u, main, x, w_out, skip_e, ssd_g, g1, n2, sc2, sh2, rw_t)


def _first_index(hit, iota, size):
    return jnp.min(jnp.where(hit, iota, size), axis=0, keepdims=True)


def _route_kernel(lg_ref, bias_ref, su_ref, idx_ref, w_ref, rank_ref, cnt_ref, carry_scr):
    i = pl.program_id(0)
    tm = lg_ref.shape[1]

    @pl.when(i == 0)
    def _():
        carry_scr[...] = jnp.zeros(carry_scr.shape, F32)

    scores = _sigmoid(lg_ref[...])
    sel = scores + bias_ref[...][:, 0:1]
    e8 = EXPERTS_PER_GROUP
    i8 = lax.broadcasted_iota(I32, (e8, tm), 0).astype(F32)
    gs = []
    for g in range(N_EXPERT_GROUPS):
        blk = sel[g * e8:(g + 1) * e8, :]
        m1 = jnp.max(blk, axis=0, keepdims=True)
        f1 = _first_index(blk == m1, i8, e8)
        m2 = jnp.max(jnp.where(i8 == f1, NEG_BIG, blk), axis=0, keepdims=True)
        gs.append(m1 + m2)
    gsc = jnp.concatenate(gs, axis=0)
    ig = lax.broadcasted_iota(I32, gsc.shape, 0).astype(F32)
    keep = jnp.zeros(gsc.shape, F32)
    for _ in range(TOPK_GROUPS):
        m = jnp.max(gsc, axis=0, keepdims=True)
        f = _first_index(gsc == m, ig, N_EXPERT_GROUPS)
        keep = jnp.where(ig == f, 1.0, keep)
        gsc = jnp.where(ig == f, NEG_BIG, gsc)
    keep_e = jnp.concatenate(
        [jnp.broadcast_to(keep[g:g + 1, :], (e8, tm)) for g in range(N_EXPERT_GROUPS)], axis=0)
    cand = jnp.where(keep_e > 0.5, sel, NEG_BIG)
    ie = lax.broadcasted_iota(I32, cand.shape, 0).astype(F32)
    chosen = jnp.zeros(cand.shape, F32)
    idxs, svals, hits = [], [], []
    for _ in range(TOP_K):
        m = jnp.max(cand, axis=0, keepdims=True)
        f = _first_index(cand == m, ie, N_EXPERTS)
        hit = ie == f
        idxs.append(f)
        svals.append(jnp.sum(jnp.where(hit, scores, 0.0), axis=0, keepdims=True))
        hits.append(hit)
        chosen = jnp.where(hit, 1.0, chosen)
        cand = jnp.where(hit, NEG_BIG, cand)
    s_all = jnp.concatenate(svals, axis=0)
    idx_ref[...] = jnp.concatenate(idxs, axis=0).astype(I32)
    w_ref[...] = ROUTED_SCALE * s_all / jnp.sum(s_all, axis=0, keepdims=True)
    before = _dot(chosen.astype(BF16), su_ref[...]) + carry_scr[...][:, 0:1]
    ranks = [jnp.sum(jnp.where(h, before, 0.0), axis=0, keepdims=True) for h in hits]
    rank_ref[...] = jnp.concatenate(ranks, axis=0).astype(I32)
    carry_scr[...] = carry_scr[...] + jnp.sum(chosen, axis=1, keepdims=True)
    cnt_ref[...] = carry_scr[...].astype(I32)


def _route(logits_t, bias_col):
    e, t = logits_t.shape
    tm = ROUTE_TILE
    su = jnp.triu(jnp.ones((tm, tm), F32), 1).astype(BF16)
    col = lambda i: (0, i)
    return pl.pallas_call(
        _route_kernel,
        grid=(t // tm,),
        in_specs=[pl.BlockSpec((e, tm), col),
                  pl.BlockSpec((e, LANES), lambda i: (0, 0)),
                  pl.BlockSpec((tm, tm), lambda i: (0, 0))],
        out_specs=[pl.BlockSpec((TOP_K, tm), col), pl.BlockSpec((TOP_K, tm), col),
                   pl.BlockSpec((TOP_K, tm), col), pl.BlockSpec((e, LANES), lambda i: (0, 0))],
        out_shape=[jax.ShapeDtypeStruct((TOP_K, t), I32), jax.ShapeDtypeStruct((TOP_K, t), F32),
                   jax.ShapeDtypeStruct((TOP_K, t), I32), jax.ShapeDtypeStruct((e, LANES), I32)],
        scratch_shapes=[pltpu.VMEM((e, LANES), F32)],
        compiler_params=_cparams(("arbitrary",), 40),
        name="route",
    )(logits_t, bias_col, su)


def _slots_kernel(idx_ref, rank_ref, ps_ref, d_ref, dc_ref):
    tm = idx_ref.shape[1]
    ie = lax.broadcasted_iota(I32, (N_EXPERTS, tm), 0)
    ps = ps_ref[...][:, 0:1]
    rows = []
    for k in range(TOP_K):
        hit = ie == idx_ref[k:k + 1, :]
        rows.append(jnp.sum(jnp.where(hit, ps, 0.0), axis=0, keepdims=True))
    dest = jnp.concatenate(rows, axis=0).astype(I32) + rank_ref[...]
    d_ref[0] = dest
    tc = dc_ref.shape[2]
    for s in range(tm // tc):
        dc_ref[s] = dest[:, s * tc:(s + 1) * tc]


def _slots(idx_t, rank_t, ps_col):
    t = idx_t.shape[1]
    tm = TOK_TILE
    tc = COMB_TILE
    col = lambda i: (0, i)
    return pl.pallas_call(
        _slots_kernel,
        grid=(t // tm,),
        in_specs=[pl.BlockSpec((TOP_K, tm), col), pl.BlockSpec((TOP_K, tm), col),
                  pl.BlockSpec((N_EXPERTS, LANES), lambda i: (0, 0))],
        out_specs=[pl.BlockSpec((1, TOP_K, tm), lambda i: (i, 0, 0)),
                   pl.BlockSpec((tm // tc, TOP_K, tc), lambda i: (i, 0, 0))],
        out_shape=[jax.ShapeDtypeStruct((t // tm, TOP_K, tm), I32),
                   jax.ShapeDtypeStruct((t // tc, TOP_K, tc), I32)],
        compiler_params=_cparams(("arbitrary",), 40),
        name="slots",
    )(idx_t, rank_t, ps_col)


def _row_copy(src, dst, sem):
    return pltpu.make_async_copy(src, dst, sem)


def _dispatch_kernel(pend_ref, dest_ref, h_ref, xs_ref, zero_scr, sem, *, nch):
    i = pl.program_id(0)
    tm = h_ref.shape[0] // nch
    tile_rows = zero_scr.shape[0]

    @pl.when(i == 0)
    def _():
        zero_scr[...] = jnp.zeros(zero_scr.shape, zero_scr.dtype)

        def pad_copy(e):
            start_row = pl.multiple_of(pend_ref[e + 1] * nch - tile_rows, tile_rows)
            return _row_copy(zero_scr, xs_ref.at[pl.ds(start_row, tile_rows), :], sem.at[0])

        def start(e, carry):
            @pl.when(pend_ref[e + 1] > pend_ref[e])
            def _():
                pad_copy(e).start()
            return carry

        def wait(e, carry):
            @pl.when(pend_ref[e + 1] > pend_ref[e])
            def _():
                pad_copy(e).wait()
            return carry

        lax.fori_loop(0, N_EXPERTS, start, 0)
        lax.fori_loop(0, N_EXPERTS, wait, 0)

    def tok_copy(t, k):
        src = pl.multiple_of(t * nch, nch)
        dst = pl.multiple_of(dest_ref[0, k, t] * nch, nch)
        return _row_copy(h_ref.at[pl.ds(src, nch), :], xs_ref.at[pl.ds(dst, nch), :], sem.at[1])

    def start(t, carry):
        for k in range(TOP_K):
            tok_copy(t, k).start(priority=k % 2)
        return carry

    def wait(t, carry):
        for k in range(TOP_K):
            tok_copy(t, k).wait()
        return carry

    lax.fori_loop(0, tm, start, 0)
    lax.fori_loop(0, tm, wait, 0)


def _dispatch(pad_end_ext, dest3, h2, n_slots, nch):
    t = h2.shape[0] // nch
    tm = TOK_TILE
    grid_spec = pltpu.PrefetchScalarGridSpec(
        num_scalar_prefetch=1,
        grid=(t // tm,),
        in_specs=[pl.BlockSpec((1, TOP_K, tm), lambda i, pe: (i, 0, 0), memory_space=pltpu.SMEM),
                  pl.BlockSpec((tm * nch, LANES), lambda i, pe: (i, 0))],
        out_specs=pl.BlockSpec(memory_space=pl.ANY),
        scratch_shapes=[pltpu.VMEM((MOE_TILE * nch, LANES), U32), pltpu.SemaphoreType.DMA((2,))],
    )
    return pl.pallas_call(
        functools.partial(_dispatch_kernel, nch=nch),
        grid_spec=grid_spec,
        out_shape=jax.ShapeDtypeStruct((n_slots * nch, LANES), U32),
        compiler_params=_cparams(("arbitrary",), 40),
        name="dispatch",
    )(pad_end_ext, dest3, h2)


def _experts_kernel(first_ref, ntile_ref, xs_ref, wg_ref, wu_ref, wd_ref, ys_ref,
                    wg_scr, wu_scr, wd_scr, wg_buf, wu_buf, wd_buf, xbuf, ybuf, xsem, ysem, wsem):
    e = pl.program_id(0)
    n = ntile_ref[e]
    first = first_ref[e]
    rows = xbuf.shape[1]
    nch = _chunks(wg_scr.shape[0])
    te = rows // nch
    wslot = e % 2

    has_next = e + 1 < pl.num_programs(0)

    def w_copies(ex, s, c):
        out = []
        for i, (src, dst) in enumerate(((wg_ref, wg_buf), (wu_ref, wu_buf), (wd_ref, wd_buf))):
            dr = dst.shape[1] // WEIGHT_CHUNKS
            r0 = pl.multiple_of(c * dr, dr)
            out.append(_row_copy(src.at[ex, pl.ds(r0, dr), :], dst.at[s, pl.ds(r0, dr), :],
                                 wsem.at[s, i]))
        return out

    def start_next_chunk(c):
        @pl.when(has_next & (c < WEIGHT_CHUNKS))
        def _():
            for cp in w_copies(e + 1, 1 - wslot, c):
                cp.start()

    @pl.when(e == 0)
    def _():
        for c in range(WEIGHT_CHUNKS):
            for cp in w_copies(0, 0, c):
                cp.start()

    nx = xbuf.shape[0]
    ny = ybuf.shape[0]

    def x_copy(j):
        r0 = pl.multiple_of((first + j) * rows, rows)
        return _row_copy(xs_ref.at[pl.ds(r0, rows), :], xbuf.at[j % nx], xsem.at[j % nx])

    def y_copy(j):
        r0 = pl.multiple_of((first + j) * rows, rows)
        return _row_copy(ybuf.at[j % ny], ys_ref.at[pl.ds(r0, rows), :], ysem.at[j % ny])

    for j0 in range(nx - 1):
        @pl.when(j0 < n)
        def _():
            x_copy(j0).start(priority=TILE_DMA_PRIORITY)

    for c in range(WEIGHT_CHUNKS):
        for cp in w_copies(e, wslot, c):
            cp.wait()
    wg_scr[...] = wg_buf[wslot].astype(BF16)
    wu_scr[...] = wu_buf[wslot].astype(BF16)
    wd_scr[...] = wd_buf[wslot].astype(BF16)

    @pl.when(n > 0)
    def _():

        def tile(j, carry):
            @pl.when(j + nx - 1 < n)
            def _():
                x_copy(j + nx - 1).start(priority=TILE_DMA_PRIORITY)

            start_next_chunk(j)
            x_copy(j).wait()

            @pl.when(j >= ny)
            def _():
                y_copy(j - ny).wait()

            xb = _load_rows(xbuf.at[j % nx], te, nch).astype(BF16)
            act = _silu(_dot(xb, wg_scr[...])) * _dot(xb, wu_scr[...])
            _store_rows(ybuf.at[j % ny], _dot(act.astype(BF16), wd_scr[...]), nch)
            y_copy(j).start(priority=TILE_DMA_PRIORITY)
            return carry

        lax.fori_loop(0, n, tile, 0)

        for back in range(ny, 0, -1):
            @pl.when(n >= back)
            def _():
                y_copy(n - back).wait()

    def rest(c, carry):
        start_next_chunk(c)
        return carry

    lax.fori_loop(jnp.minimum(n, WEIGHT_CHUNKS), WEIGHT_CHUNKS, rest, 0)


def _experts(first_tile, n_tile, xs, wg, wu, wd):
    n_exp, d, f = wg.shape
    nch = _chunks(d)
    rows = MOE_TILE * nch
    grid_spec = pltpu.PrefetchScalarGridSpec(
        num_scalar_prefetch=2,
        grid=(n_exp,),
        in_specs=[pl.BlockSpec(memory_space=pl.ANY), pl.BlockSpec(memory_space=pl.ANY),
                  pl.BlockSpec(memory_space=pl.ANY), pl.BlockSpec(memory_space=pl.ANY)],
        out_specs=pl.BlockSpec(memory_space=pl.ANY),
        scratch_shapes=[pltpu.VMEM((d, f), BF16), pltpu.VMEM((d, f), BF16), pltpu.VMEM((f, d), BF16),
                        pltpu.VMEM((2, d, f), F32), pltpu.VMEM((2, d, f), F32),
                        pltpu.VMEM((2, f, d), F32),
                        pltpu.VMEM((3, rows, LANES), U32), pltpu.VMEM((2, rows, LANES), U32),
                        pltpu.SemaphoreType.DMA((3,)), pltpu.SemaphoreType.DMA((2,)),
                        pltpu.SemaphoreType.DMA((2, 3))],
    )
    return pl.pallas_call(
        _experts_kernel,
        grid_spec=grid_spec,
        out_shape=jax.ShapeDtypeStruct(xs.shape, U32),
        compiler_params=_cparams(("arbitrary",), 56),
        name="experts",
    )(first_tile, n_tile, xs, wg, wu, wd)


def _combine_kernel(dest_ref, dnext_ref, ys_ref, wt_ref, h_ref, x1_ref, sg_ref, su_ref, sd_ref,
                    g2_ref, fg_ref, eye_ref, o_ref, buf, sem):
    i = pl.program_id(0)
    n = pl.num_programs(0)
    tm = x1_ref.shape[0]
    nch = _chunks(x1_ref.shape[1])
    slot = i % 2

    def row_copy(d_ref, s, t, k):
        src = pl.multiple_of(d_ref[0, k, t] * nch, nch)
        dst = pl.multiple_of(t * nch, nch)
        return _row_copy(ys_ref.at[pl.ds(src, nch), :], buf.at[s, k, pl.ds(dst, nch), :], sem.at[s])

    def start_tile(d_ref, s):
        def body(t, carry):
            for k in range(TOP_K):
                row_copy(d_ref, s, t, k).start(priority=k % 2)
            return carry
        lax.fori_loop(0, tm, body, 0)

    @pl.when(i == 0)
    def _():
        start_tile(dest_ref, 0)

    @pl.when(i + 1 < n)
    def _():
        start_tile(dnext_ref, 1 - slot)

    hb = _load_rows(h_ref, tm, nch).astype(BF16)
    act = _silu(_dot(hb, sg_ref[...])) * _dot(hb, su_ref[...])
    ffn = _dot(act.astype(BF16), sd_ref[...])
    w_col = _dot_rhs_f32(eye_ref[...], wt_ref[...], _NT)

    def wait(t, carry):
        for k in range(TOP_K):
            row_copy(dest_ref, slot, t, k).wait()
        return carry

    lax.fori_loop(0, tm, wait, 0)
    for k in range(TOP_K):
        ffn = ffn + w_col[:, k:k + 1] * _load_rows(buf.at[slot, k], tm, nch)
    x2 = x1_ref[...] + g2_ref[0] * ffn
    o_ref[...] = _rms(x2) * fg_ref[...]


def _combine(dest3, ys, w_t, h2, x1, sg, su, sd, g2, fg, *, seq):
    t, d = x1.shape
    nch = _chunks(d)
    tm = COMB_TILE
    per = seq // tm
    f = sg.shape[1]
    eye = jnp.eye(tm, dtype=BF16)
    row = lambda i: (i, 0)
    c2 = lambda i: (0, 0)
    last = t // tm - 1
    return pl.pallas_call(
        _combine_kernel,
        grid=(t // tm,),
        in_specs=[pl.BlockSpec((1, TOP_K, tm), lambda i: (i, 0, 0), memory_space=pltpu.SMEM),
                  pl.BlockSpec((1, TOP_K, tm), lambda i: (jnp.minimum(i + 1, last), 0, 0),
                               memory_space=pltpu.SMEM),
                  pl.BlockSpec(memory_space=pl.ANY),
                  pl.BlockSpec((TOP_K, tm), lambda i: (0, i)),
                  pl.BlockSpec((tm * nch, LANES), row),
                  pl.BlockSpec((tm, d), row),
                  pl.BlockSpec((d, f), c2), pl.BlockSpec((d, f), c2), pl.BlockSpec((f, d), c2),
                  pl.BlockSpec((1, 1, d), lambda i: (i // per, 0, 0)),
                  pl.BlockSpec((1, d), c2),
                  pl.BlockSpec((tm, tm), c2)],
        out_specs=pl.BlockSpec((tm, d), row),
        out_shape=jax.ShapeDtypeStruct((t, d), F32),
        scratch_shapes=[pltpu.VMEM((2, TOP_K, tm * nch, LANES), U32),
                        pltpu.SemaphoreType.DMA((2,))],
        compiler_params=_cparams(("arbitrary",), 48),
        name="combine",
    )(dest3, dest3, ys, w_t, h2, x1, sg, su, sd, g2, fg, eye)


def _rope_tables(seq):
    n_rows = seq // GRID_W
    rows = np.repeat(np.arange(n_rows, dtype=np.float32), GRID_W)
    cols = np.tile(np.arange(GRID_W, dtype=np.float32), n_rows)
    expo = -np.arange(0, ROPE_AXIS_DIM, 2, dtype=np.float32) / np.float32(ROPE_AXIS_DIM)
    inv = np.power(np.float32(ROPE_THETA), expo).astype(np.float32)
    ar = (rows[:, None] * inv[None, :]).astype(np.float64)
    ac = (cols[:, None] * inv[None, :]).astype(np.float64)
    cos = np.concatenate([np.cos(ar), np.cos(ac), np.cos(ar), np.cos(ac)], axis=1)
    sin = np.concatenate([-np.sin(ar), -np.sin(ac), np.sin(ar), np.sin(ac)], axis=1)
    return jnp.asarray(cos, F32), jnp.asarray(sin, F32)


def _layer(x, c, ctx, c_ctx, w_ada, b_ada, norm1_g, norm2_g, w_in, attn_sink, conv_w, conv_b,
           dt_bias, a_log, d_skip, ssd_norm_g, w_out, router_w, router_bias, wg, wu, wd,
           sw_gate, sw_up, sw_down, final_norm_g):
    batch, seq, d = x.shape
    ctx_len = ctx.shape[1]
    t = batch * seq

    cvec = jnp.zeros((SUBLANES, d), F32).at[:batch].set(c).at[batch].set(c_ctx)
    mod = _ada(cvec, w_ada, b_ada[None, :])
    sh1, sc1, g1, sh2, sc2, g2 = [mod[:batch, i * d:(i + 1) * d][:, None, :] for i in range(6)]
    sh1c, sc1c = [jnp.broadcast_to(mod[batch, i * d:(i + 1) * d][None, None, :], (batch, 1, d))
                  for i in range(2)]

    w_main, w_dt = _wprep(w_in.T)
    cos, sin = _rope_tables(seq)
    n1 = norm1_g[None, :]
    n_main = MAIN_WIDTH // IN_TILE_N
    ctx0 = COL_XBC // IN_TILE_N
    main, dtp = _inproj(x.reshape(t, d), n1, sc1, sh1, w_main, w_dt, cos, sin,
                        seq=seq, tm=min(seq, 1024), rope=True, tile0=0, n_tiles=n_main)
    main_c, dtp_c = _inproj(ctx.reshape(batch * ctx_len, d), n1, sc1c, sh1c, w_main, w_dt, None, None,
                            seq=ctx_len, tm=ctx_len, rope=False, tile0=ctx0, n_tiles=n_main - ctx0)

    attn = _attn(main, main_c, attn_sink, batch=batch, seq=seq, ctx_len=ctx_len)

    cw8 = jnp.pad(conv_w, ((0, SUBLANES - SSD_CONV), (0, 0)))
    u = _conv(main, COL_XBC, cw8, conv_b[None, :], seq=seq)
    u_c = _conv(main_c, 0, cw8, conv_b[None, :], seq=ctx_len)
    nd = N_DIRS * SSD_HEADS
    bias = jnp.pad(dt_bias.reshape(1, nd), ((0, 0), (0, LANES - nd)))
    a_neg = jnp.pad(-jnp.exp(a_log.reshape(1, nd)), ((0, 0), (0, LANES - nd)))
    lane_head = np.arange(SSD_WIDTH) // SSD_HEAD_DIM
    e_np = np.zeros((N_DIRS, LANES, SSD_WIDTH), np.float32)
    for dd in range(N_DIRS):
        e_np[dd, dd * SSD_HEADS + lane_head, np.arange(SSD_WIDTH)] = 1.0
    e_mat = jnp.asarray(e_np, BF16)

    def tri(q):
        lo = np.tril(np.ones((q, q), np.float32))
        return jnp.asarray(np.stack([lo, lo.T]), BF16)

    qc = min(SSD_CHUNK, ctx_len)
    h_ctx = _ssd(u_c, dtp_c, None, bias, a_neg, e_mat, tri(qc), batch=batch, seq=ctx_len, q=qc)
    ql = min(SSD_CHUNK, seq)
    yf, yb = _ssd(u, dtp, h_ctx, bias, a_neg, e_mat, tri(ql), batch=batch, seq=seq, q=ql)

    skip_e = jnp.repeat(d_skip, SSD_HEAD_DIM)[None, :]
    x1, h2, logits_t = _outproj(attn, yf, yb, u, main, x.reshape(t, d), w_out.astype(BF16),
                                skip_e, ssd_norm_g[None, :], g1, norm2_g[None, :], sc2, sh2,
                                jnp.pad(router_w, ((0, 0), (0, LANES - N_EXPERTS))), seq=seq)

    bias_col = jnp.broadcast_to(router_bias[:, None], (N_EXPERTS, LANES))
    idx_t, w_t, rank_t, cnt = _route(logits_t, bias_col)
    counts = cnt[:, 0]
    te = MOE_TILE
    padded = (counts + te - 1) // te * te
    pad_end = jnp.cumsum(padded)
    pad_start = pad_end - padded
    n_slots = -(-(t * TOP_K + N_EXPERTS * (te - 1)) // te) * te
    ps_col = jnp.broadcast_to(pad_start.astype(F32)[:, None], (N_EXPERTS, LANES))
    dest3, dest3c = _slots(idx_t, rank_t, ps_col)
    pad_end_ext = jnp.concatenate([jnp.zeros((1,), I32), pad_end.astype(I32)])

    xs = _dispatch(pad_end_ext, dest3, h2, n_slots, _chunks(d))
    ys = _experts((pad_start // te).astype(I32), (padded // te).astype(I32), xs, wg, wu, wd)
    out = _combine(dest3c, ys, w_t, h2, x1, sw_gate.astype(BF16), sw_up.astype(BF16),
                   sw_down.astype(BF16), g2, final_norm_g[None, :], seq=seq)
    return out.reshape(batch, seq, d)


def kernel(x, c, ctx, c_ctx, w_ada, b_ada, norm1_g, norm2_g, w_in, attn_sink, conv_w, conv_b, dt_bias, a_log, d_skip, ssd_norm_g, w_out, router_w, router_bias, expert_w_gate, expert_w_up, expert_w_down, shared_w_gate, shared_w_up, shared_w_down, final_norm_g):
    assert w_ada.shape[0] == 1, "single-layer stack only"
    return _layer(x, c, ctx, c_ctx, w_ada[0], b_ada[0], norm1_g[0], norm2_g[0], w_in[0],
                  attn_sink[0], conv_w[0], conv_b[0], dt_bias[0], a_log[0], d_skip[0],
                  ssd_norm_g[0], w_out[0], router_w[0], router_bias[0], expert_w_gate[0],
                  expert_w_up[0], expert_w_down[0], shared_w_gate[0], shared_w_up[0],
                  shared_w_down[0], final_norm_g)
```

```python
import functools

import jax
import jax.numpy as jnp
import numpy as np
from jax import lax
from jax.experimental import pallas as pl
from jax.experimental.pallas import tpu as pltpu

F32 = jnp.float32
BF16 = jnp.bfloat16
I32 = jnp.int32

EPS = 1e-6
GRID_W = 64
N_Q_HEADS = 8
N_KV_HEADS = 2
Q_PER_KV = N_Q_HEADS // N_KV_HEADS
HEAD_DIM = 128
ATTN_WIDTH = N_Q_HEADS * HEAD_DIM
KV_WIDTH = N_KV_HEADS * HEAD_DIM
WINDOW = 128
ATTN_BLOCK = 128
ROPE_THETA = 10000.0
ROPE_AXIS_DIM = HEAD_DIM // 2
SSD_HEADS = 16
SSD_HEAD_DIM = 64
SSD_WIDTH = SSD_HEADS * SSD_HEAD_DIM
SSD_GROUPS = 4
SSD_HEADS_PER_GROUP = SSD_HEADS // SSD_GROUPS
SSD_STATE = 128
SSD_CONV = 5
N_DIRS = 2
XBC_WIDTH = SSD_WIDTH + 2 * SSD_GROUPS * SSD_STATE
N_EXPERTS = 64
N_EXPERT_GROUPS = 8
EXPERTS_PER_GROUP = N_EXPERTS // N_EXPERT_GROUPS
TOPK_GROUPS = 4
TOP_K = 8
EXPERT_DIM = 512
ROUTED_SCALE = 2.5

LANES = 128
SUBLANES = 8
NEG_BIG = -1e30

COL_Q = 0
COL_Z = ATTN_WIDTH
COL_XBC = COL_Z + SSD_WIDTH
COL_K = COL_XBC + XBC_WIDTH
COL_V = COL_K + KV_WIDTH
MAIN_WIDTH = COL_V + KV_WIDTH

IN_TILE_N = 512
SSD_CHUNK = 256
MOE_TILE = 256
ROUTE_TILE = 512
TOK_TILE = 256
COMB_TILE = 128
TILE_DMA_PRIORITY = 1
WEIGHT_CHUNKS = 8

_NT = (((1,), (1,)), ((), ()))


def _cparams(sem, vmem_mb):
    return pltpu.CompilerParams(dimension_semantics=sem, vmem_limit_bytes=vmem_mb * 1024 * 1024)


def _dot(a, b, dims=None):
    if dims is None:
        return jnp.dot(a, b, preferred_element_type=F32)
    return lax.dot_general(a, b, dims, preferred_element_type=F32)


def _split(a):
    hi = a.astype(BF16)
    lo = (a - hi.astype(F32)).astype(BF16)
    return hi, lo


def _dot_lhs_f32(a, b_exact, dims=None):
    hi, lo = _split(a)
    return _dot(hi, b_exact, dims) + _dot(lo, b_exact, dims)


def _dot_rhs_f32(a_exact, b, dims=None):
    hi, lo = _split(b)
    return _dot(a_exact, hi, dims) + _dot(a_exact, lo, dims)


def _dot_f32(a, b, dims=None):
    ah, al = _split(a)
    bh, bl = _split(b)
    return _dot(ah, bh, dims) + (_dot(al, bh, dims) + _dot(ah, bl, dims))


def _sigmoid(x):
    return 1.0 / (1.0 + jnp.exp(-x))


def _silu(x):
    return x * _sigmoid(x)


def _rms(x):
    return x * lax.rsqrt(jnp.mean(x * x, axis=-1, keepdims=True) + EPS)


U32 = jnp.uint32
_HI_MASK = 0xFFFF0000


def _chunks(d):
    return d // (2 * LANES)


def _load_rows(ref, rows, nch):
    words = [ref[pl.ds(j, rows, stride=nch), :] for j in range(nch)]
    lo = [lax.bitcast_convert_type(w << 16, F32) for w in words]
    hi = [lax.bitcast_convert_type(w & jnp.uint32(_HI_MASK), F32) for w in words]
    return jnp.concatenate(lo + hi, axis=1)


def _store_rows(ref, val, nch):
    rows, d = val.shape
    bits = lax.bitcast_convert_type(val.astype(BF16).astype(F32), U32)
    for j in range(nch):
        lo = bits[:, j * LANES:(j + 1) * LANES] >> 16
        hi = bits[:, d // 2 + j * LANES:d // 2 + (j + 1) * LANES] & jnp.uint32(_HI_MASK)
        ref[pl.ds(j, rows, stride=nch), :] = lo | hi


def _ada_kernel(c_ref, w_ref, b_ref, o_ref):
    o_ref[...] = _dot_f32(_silu(c_ref[...]), w_ref[...]) + b_ref[...]


def _ada(cvec, w, b):
    m, d = cvec.shape
    n = w.shape[1]
    tn = 1024
    return pl.pallas_call(
        _ada_kernel,
        grid=(n // tn,),
        in_specs=[pl.BlockSpec((m, d), lambda j: (0, 0)),
                  pl.BlockSpec((d, tn), lambda j: (0, j)),
                  pl.BlockSpec((1, tn), lambda j: (0, j))],
        out_specs=pl.BlockSpec((m, tn), lambda j: (0, j)),
        out_shape=jax.ShapeDtypeStruct((m, n), F32),
        compiler_params=_cparams(("arbitrary",), 40),
        name="ada",
    )(cvec, w, b)


def _rope_mix(acc, cos, sin):
    heads = acc.shape[1] // HEAD_DIM
    partner = jnp.concatenate(
        [pltpu.roll(acc[:, h * HEAD_DIM:(h + 1) * HEAD_DIM], HEAD_DIM // 2, 1) for h in range(heads)],
        axis=1)
    return acc * cos + partner * sin


def _inproj_kernel(*refs, rope, n_rope_tiles, kv_tile):
    if rope:
        x_ref, g_ref, sc_ref, sh_ref, w_ref, wdt_ref, cos_ref, sin_ref, o_ref, dt_ref, h_scr = refs
    else:
        x_ref, g_ref, sc_ref, sh_ref, w_ref, wdt_ref, o_ref, dt_ref, h_scr = refs
    j = pl.program_id(1)

    @pl.when(j == 0)
    def _():
        h = (_rms(x_ref[...]) * g_ref[...]) * (1.0 + sc_ref[0]) + sh_ref[0]
        hb = h.astype(BF16)
        h_scr[...] = hb
        dt_ref[...] = _dot(hb, wdt_ref[...])

    acc = _dot(h_scr[...], w_ref[0])
    if not rope:
        o_ref[...] = acc.astype(o_ref.dtype)
        return
    reps = acc.shape[1] // HEAD_DIM

    @pl.when(j < n_rope_tiles)
    def _():
        cos = jnp.concatenate([cos_ref[...]] * reps, axis=1)
        sin = jnp.concatenate([sin_ref[...]] * reps, axis=1)
        o_ref[...] = _rope_mix(acc, cos, sin).astype(o_ref.dtype)

    @pl.when(j == kv_tile)
    def _():
        nk = KV_WIDTH // HEAD_DIM
        ones = jnp.ones_like(cos_ref[...])
        cos = jnp.concatenate([cos_ref[...]] * nk + [ones] * (reps - nk), axis=1)
        sin = jnp.concatenate([sin_ref[...]] * nk + [ones * 0.0] * (reps - nk), axis=1)
        o_ref[...] = _rope_mix(acc, cos, sin).astype(o_ref.dtype)

    @pl.when((j >= n_rope_tiles) & (j != kv_tile))
    def _():
        o_ref[...] = acc.astype(o_ref.dtype)


def _wprep_kernel(w_ref, wdt_ref, o_ref, odt_ref):
    j = pl.program_id(0)
    x = w_ref[...].T
    tn = x.shape[1]
    q = ROPE_AXIS_DIM // 2
    lane = lax.broadcasted_iota(I32, x.shape, 1)
    blk = (lane & (HEAD_DIM - 1)) // q
    swapped = jnp.where(blk == 1, pltpu.roll(x, tn - q, 1), jnp.where(blk == 2, pltpu.roll(x, q, 1), x))
    n_q = ATTN_WIDTH // tn
    kv = COL_K // tn

    @pl.when(j == 0)
    def _():
        rows = wdt_ref[...]
        pad = jnp.zeros((LANES - rows.shape[0], rows.shape[1]), rows.dtype)
        odt_ref[...] = jnp.concatenate([rows, pad], axis=0).T.astype(odt_ref.dtype)

    @pl.when(j < n_q)
    def _():
        o_ref[0] = swapped.astype(o_ref.dtype)

    @pl.when(j == kv)
    def _():
        o_ref[0] = jnp.where(lane < KV_WIDTH, swapped, x).astype(o_ref.dtype)

    @pl.when((j >= n_q) & (j != kv))
    def _():
        o_ref[0] = x.astype(o_ref.dtype)


def _wprep(w_in_t):
    d = w_in_t.shape[1]
    tn = IN_TILE_N
    n_q = ATTN_WIDTH // tn
    n_tiles = MAIN_WIDTH // tn
    src = lambda j: (jnp.where(j < n_q, j, jnp.where(j < n_tiles - 1, j + 1, n_q)), 0)
    n_dt = w_in_t.shape[0] - MAIN_WIDTH
    return pl.pallas_call(
        _wprep_kernel,
        grid=(n_tiles,),
        in_specs=[pl.BlockSpec((tn, d), src),
                  pl.BlockSpec((n_dt, d), lambda j: (MAIN_WIDTH // n_dt, 0))],
        out_specs=[pl.BlockSpec((1, d, tn), lambda j: (j, 0, 0)),
                   pl.BlockSpec((d, LANES), lambda j: (0, 0))],
        out_shape=[jax.ShapeDtypeStruct((n_tiles, d, tn), BF16),
                   jax.ShapeDtypeStruct((d, LANES), BF16)],
        compiler_params=_cparams(("arbitrary",), 40),
        name="wprep",
    )(w_in_t, w_in_t)


def _inproj(x, g, sc, sh, w, wdt, cos, sin, *, seq, tm, rope, tile0, n_tiles):
    t, d = x.shape
    tn = IN_TILE_N
    wn = n_tiles * tn
    per = seq // tm
    in_specs = [pl.BlockSpec((tm, d), lambda i, j: (i, 0)),
                pl.BlockSpec((1, d), lambda i, j: (0, 0)),
                pl.BlockSpec((1, 1, d), lambda i, j: (i // per, 0, 0)),
                pl.BlockSpec((1, 1, d), lambda i, j: (i // per, 0, 0)),
                pl.BlockSpec((1, d, tn), lambda i, j: (j + tile0, 0, 0)),
                pl.BlockSpec((d, LANES), lambda i, j: (0, 0))]
    args = [x, g, sc, sh, w, wdt]
    if rope:
        in_specs += [pl.BlockSpec((tm, HEAD_DIM), lambda i, j: (i % per, 0)),
                     pl.BlockSpec((tm, HEAD_DIM), lambda i, j: (i % per, 0))]
        args += [cos, sin]
    kern = functools.partial(_inproj_kernel, rope=rope, n_rope_tiles=ATTN_WIDTH // tn,
                             kv_tile=COL_K // tn)
    return pl.pallas_call(
        kern,
        grid=(t // tm, wn // tn),
        in_specs=in_specs,
        out_specs=[pl.BlockSpec((tm, tn), lambda i, j: (i, j)),
                   pl.BlockSpec((tm, LANES), lambda i, j: (i, 0))],
        out_shape=[jax.ShapeDtypeStruct((t, wn), BF16), jax.ShapeDtypeStruct((t, LANES), F32)],
        scratch_shapes=[pltpu.VMEM((tm, d), BF16)],
        compiler_params=_cparams(("arbitrary", "arbitrary"), 48),
        name="inproj_rope" if rope else "inproj",
    )(*args)


def _conv_kernel(x_ref, w_ref, b_ref, o_ref, pad_scr, *, seq, rows):
    halo = SUBLANES
    cw = x_ref.shape[1]
    pad_scr[0:halo, :] = jnp.zeros((halo, cw), F32)
    pad_scr[seq + halo:seq + 2 * halo, :] = jnp.zeros((halo, cw), F32)
    pad_scr[halo:seq + halo, :] = x_ref[...].astype(F32)
    w = w_ref[...]
    bias = b_ref[...]
    for r0 in range(0, seq, rows):
        acc = jnp.broadcast_to(bias, (rows, cw))
        for tap in range(SSD_CONV):
            start = r0 + halo - SSD_CONV // 2 + tap
            acc = acc + w[tap:tap + 1, :] * pad_scr[start:start + rows, :]
        o_ref[r0:r0 + rows, :] = _silu(acc).astype(o_ref.dtype)


def _conv(main, col0, w8, b, *, seq):
    t = main.shape[0]
    cw = 256
    rows = min(seq, 256)
    cb0 = col0 // cw
    kern = functools.partial(_conv_kernel, seq=seq, rows=rows)
    return pl.pallas_call(
        kern,
        grid=(t // seq, XBC_WIDTH // cw),
        in_specs=[pl.BlockSpec((seq, cw), lambda b_, c: (b_, cb0 + c)),
                  pl.BlockSpec((SUBLANES, cw), lambda b_, c: (0, c)),
                  pl.BlockSpec((1, cw), lambda b_, c: (0, c))],
        out_specs=pl.BlockSpec((seq, cw), lambda b_, c: (b_, c)),
        out_shape=jax.ShapeDtypeStruct((t, XBC_WIDTH), BF16),
        scratch_shapes=[pltpu.VMEM((seq + 2 * SUBLANES, cw), F32)],
        compiler_params=_cparams(("arbitrary", "arbitrary"), 40),
        name="conv",
    )(main, w8, b)


def _attn_kernel(sink_ref, bias_ref, q_ref, kp_ref, kc_ref, kn_ref, vp_ref, vc_ref, vn_ref,
                 kx_ref, vx_ref, o_ref):
    blk = ATTN_BLOCK
    g = Q_PER_KV
    hd = HEAD_DIM
    scale = hd ** -0.5
    bias = jnp.concatenate([bias_ref[0]] * g, axis=0)
    row = lax.broadcasted_iota(I32, (g * blk, 1), 0)
    for hk in range(N_KV_HEADS):
        ks = slice(hk * hd, (hk + 1) * hd)
        q = jnp.concatenate([q_ref[:, (hk * g + i) * hd:(hk * g + i + 1) * hd] for i in range(g)],
                            axis=0)
        k_loc = jnp.concatenate([kp_ref[:, ks], kc_ref[:, ks], kn_ref[:, ks]], axis=0)
        v_loc = jnp.concatenate([vp_ref[:, ks], vc_ref[:, ks], vn_ref[:, ks]], axis=0)
        s_loc = _dot(q, k_loc, _NT) * scale + bias
        s_ctx = _dot(q, kx_ref[:, ks], _NT) * scale
        sink = jnp.zeros((g * blk, 1), F32)
        for i in range(g):
            sink = jnp.where((row >= i * blk) & (row < (i + 1) * blk), sink_ref[hk * g + i], sink)
        m = jnp.maximum(jnp.maximum(jnp.max(s_loc, axis=1, keepdims=True),
                                    jnp.max(s_ctx, axis=1, keepdims=True)), sink)
        p_loc = jnp.exp(s_loc - m)
        p_ctx = jnp.exp(s_ctx - m)
        den = (jnp.sum(p_loc, axis=1, keepdims=True) + jnp.sum(p_ctx, axis=1, keepdims=True)
               + jnp.exp(sink - m))
        o = _dot(p_loc.astype(BF16), v_loc) + _dot(p_ctx.astype(BF16), vx_ref[:, ks])
        o = o / den
        for i in range(g):
            o_ref[:, (hk * g + i) * hd:(hk * g + i + 1) * hd] = (
                o[i * blk:(i + 1) * blk, :].astype(o_ref.dtype))


def _attn(main, main_c, sink, *, batch, seq, ctx_len):
    t = main.shape[0]
    blk = ATTN_BLOCK
    nb = seq // blk
    kcol = COL_K // KV_WIDTH
    vcol = COL_V // KV_WIDTH
    kcol_c = XBC_WIDTH // KV_WIDTH
    vcol_c = kcol_c + 1

    def nbr(col, off):
        return pl.BlockSpec(
            (blk, KV_WIDTH), lambda b, i: (b * nb + jnp.clip(i + off, 0, nb - 1), col))

    r = np.arange(blk)[:, None]
    c = np.arange(3 * blk)[None, :]
    variants = []
    for v in range(4):
        lo = np.maximum(r, blk if v & 1 else 0)
        hi = np.minimum(r + 2 * WINDOW, 2 * blk - 1 if v & 2 else 3 * blk - 1)
        variants.append(np.where((c >= lo) & (c <= hi), 0.0, NEG_BIG))
    bias = jnp.asarray(np.stack(variants), F32)
    variant = lambda b, i: ((i == 0).astype(I32) + 2 * (i == nb - 1).astype(I32), 0, 0)

    in_specs = [pl.BlockSpec(memory_space=pltpu.SMEM),
                pl.BlockSpec((1, blk, 3 * blk), variant),
                pl.BlockSpec((blk, ATTN_WIDTH), lambda b, i: (b * nb + i, 0)),
                nbr(kcol, -1), nbr(kcol, 0), nbr(kcol, 1),
                nbr(vcol, -1), nbr(vcol, 0), nbr(vcol, 1),
                pl.BlockSpec((ctx_len, KV_WIDTH), lambda b, i: (b, kcol_c)),
                pl.BlockSpec((ctx_len, KV_WIDTH), lambda b, i: (b, vcol_c))]
    return pl.pallas_call(
        _attn_kernel,
        grid=(batch, nb),
        in_specs=in_specs,
        out_specs=pl.BlockSpec((blk, ATTN_WIDTH), lambda b, i: (b * nb + i, 0)),
        out_shape=jax.ShapeDtypeStruct((t, ATTN_WIDTH), BF16),
        compiler_params=_cparams(("arbitrary", "arbitrary"), 40),
        name="attn",
    )(sink, bias, main, main, main, main, main, main, main, main_c, main_c)


def _softplus(x):
    return jnp.maximum(x, 0.0) + jnp.log1p(jnp.exp(-jnp.abs(x)))


def _ssd_chunk(u_ref, dtp_ref, bias_ref, a_ref, e_ref, tri_ref, h_scr, y_ref, d):
    q = u_ref.shape[0]
    rev = d == 1
    p = SSD_HEAD_DIM
    gw = SSD_HEADS_PER_GROUP * p
    dt = _softplus(dtp_ref[...] + bias_ref[...])
    dta = dt * a_ref[...]
    tri = tri_ref[d]
    a_cum = _dot_rhs_f32(tri, dta)
    a_row = _dot_lhs_f32(dta.T, tri_ref[1 - d])
    edge = a_cum[0:1, :] if rev else a_cum[q - 1:q, :]
    ea = jnp.exp(a_cum)
    w_state = dt * jnp.exp(edge - a_cum)
    e_mat = e_ref[d]
    dt_e = _dot(dt.astype(BF16), e_mat)
    ws_e = _dot(w_state.astype(BF16), e_mat)
    x = u_ref[:, 0:SSD_WIDTH].astype(F32)
    xdd = (x * ws_e).astype(BF16)
    li = lax.broadcasted_iota(I32, (q, q), 0)
    si = lax.broadcasted_iota(I32, (q, q), 1)
    keep = (si >= li) if rev else (si <= li)
    ea_e = _dot_lhs_f32(ea, e_mat)
    edge_e = ea_e[0:1, :] if rev else ea_e[q - 1:q, :]
    if y_ref is not None:
        xd = (x * dt_e).astype(BF16)
    for g in range(SSD_GROUPS):
        b_t = u_ref[:, SSD_WIDTH + g * SSD_STATE:SSD_WIDTH + (g + 1) * SSD_STATE].T
        c0 = SSD_WIDTH + SSD_GROUPS * SSD_STATE + g * SSD_STATE
        c_g = u_ref[:, c0:c0 + SSD_STATE]
        h_t = h_scr[d, g]
        if y_ref is not None:
            cb = _dot(c_g, b_t)
            ys = []
            for r in range(SSD_HEADS_PER_GROUP):
                hd = g * SSD_HEADS_PER_GROUP + r
                ln = d * SSD_HEADS + hd
                seg = a_cum[:, ln:ln + 1] - a_row[ln:ln + 1, :]
                decay = jnp.exp(jnp.where(keep, seg, NEG_BIG))
                ys.append(_dot((cb * decay).astype(BF16), xd[:, hd * p:(hd + 1) * p]))
            y_off = _dot(c_g, h_t.astype(BF16)) * ea_e[:, g * gw:(g + 1) * gw]
            y_ref[:, g * gw:(g + 1) * gw] = jnp.concatenate(ys, axis=1) + y_off
        st = _dot(b_t, xdd[:, g * gw:(g + 1) * gw])
        h_scr[d, g] = h_t * edge_e[:, g * gw:(g + 1) * gw] + st


def _ssd_kernel(*refs, emit_y):
    if emit_y:
        (uf_ref, ub_ref, df_ref, db_ref, h0_ref, bias_ref, a_ref, e_ref, tri_ref,
         yf_ref, yb_ref, h_scr) = refs
    else:
        (uf_ref, ub_ref, df_ref, db_ref, bias_ref, a_ref, e_ref, tri_ref, hout_ref, h_scr) = refs
        yf_ref = yb_ref = None
    c = pl.program_id(1)

    @pl.when(c == 0)
    def _():
        if emit_y:
            h_scr[...] = h0_ref[0]
        else:
            h_scr[...] = jnp.zeros(h_scr.shape, F32)

    _ssd_chunk(uf_ref, df_ref, bias_ref, a_ref, e_ref, tri_ref, h_scr, yf_ref, 0)
    _ssd_chunk(ub_ref, db_ref, bias_ref, a_ref, e_ref, tri_ref, h_scr, yb_ref, 1)
    if not emit_y:
        @pl.when(c == pl.num_programs(1) - 1)
        def _():
            hout_ref[0] = h_scr[...]


def _ssd(u, dtp, h0, bias, a_neg, e_mat, tri, *, batch, seq, q):
    t = u.shape[0]
    nc = seq // q
    emit_y = h0 is not None
    hshape = (N_DIRS, SSD_GROUPS, SSD_STATE, SSD_HEADS_PER_GROUP * SSD_HEAD_DIM)
    fwd = lambda b, c: (b * nc + c, 0)
    bwd = lambda b, c: (b * nc + nc - 1 - c, 0)
    const2 = lambda b, c: (0, 0)
    const3 = lambda b, c: (0, 0, 0)
    in_specs = [pl.BlockSpec((q, XBC_WIDTH), fwd), pl.BlockSpec((q, XBC_WIDTH), bwd),
                pl.BlockSpec((q, LANES), fwd), pl.BlockSpec((q, LANES), bwd)]
    args = [u, u, dtp, dtp]
    if emit_y:
        in_specs.append(pl.BlockSpec((1,) + hshape, lambda b, c: (b, 0, 0, 0, 0)))
        args.append(h0)
    in_specs += [pl.BlockSpec((1, LANES), const2), pl.BlockSpec((1, LANES), const2),
                 pl.BlockSpec((N_DIRS, LANES, SSD_WIDTH), const3),
                 pl.BlockSpec((N_DIRS, q, q), const3)]
    args += [bias, a_neg, e_mat, tri]
    if emit_y:
        out_specs = [pl.BlockSpec((q, SSD_WIDTH), fwd), pl.BlockSpec((q, SSD_WIDTH), bwd)]
        out_shape = [jax.ShapeDtypeStruct((t, SSD_WIDTH), F32)] * 2
    else:
        out_specs = pl.BlockSpec((1,) + hshape, lambda b, c: (b, 0, 0, 0, 0))
        out_shape = jax.ShapeDtypeStruct((batch,) + hshape, F32)
    return pl.pallas_call(
        functools.partial(_ssd_kernel, emit_y=emit_y),
        grid=(batch, nc),
        in_specs=in_specs,
        out_specs=out_specs,
        out_shape=out_shape,
        scratch_shapes=[pltpu.VMEM(hshape, F32)],
        compiler_params=_cparams(("arbitrary", "arbitrary"), 48),
        name="ssd_lat" if emit_y else "ssd_ctx",
    )(*args)


def _outproj_kernel(attn_ref, yf_ref, yb_ref, xs_ref, z_ref, x_ref, w_ref, skip_ref, ng_ref,
                    g1_ref, n2_ref, sc2_ref, sh2_ref, rw_ref, x1_ref, h2_ref, lg_ref):
    y = yf_ref[...] + yb_ref[...] + skip_ref[...] * xs_ref[...].astype(F32)
    y = y * _silu(z_ref[...].astype(F32))
    y = (_rms(y) * ng_ref[...]).astype(BF16)
    m = _dot(attn_ref[...], w_ref[0:ATTN_WIDTH, :]) + _dot(y, w_ref[ATTN_WIDTH:, :])
    x1 = x_ref[...] + g1_ref[0] * m
    x1_ref[...] = x1
    h2 = (_rms(x1) * n2_ref[...]) * (1.0 + sc2_ref[0]) + sh2_ref[0]
    _store_rows(h2_ref, h2, _chunks(h2.shape[1]))
    logits = _dot_f32(h2, rw_ref[...])
    lg_ref[...] = logits.T[0:lg_ref.shape[0], :]


def _outproj(attn, yf, yb, u, main, x, w_out, skip_e, ssd_g, g1, n2, sc2, sh2, rw, *, seq):
    t, d = x.shape
    tm = TOK_TILE
    per = seq // tm
    row = lambda i: (i, 0)
    c2 = lambda i: (0, 0)
    bat = lambda i: (i // per, 0, 0)
    in_specs = [pl.BlockSpec((tm, ATTN_WIDTH), row),
                pl.BlockSpec((tm, SSD_WIDTH), row),
                pl.BlockSpec((tm, SSD_WIDTH), row),
                pl.BlockSpec((tm, SSD_WIDTH), row),
                pl.BlockSpec((tm, SSD_WIDTH), lambda i: (i, COL_Z // SSD_WIDTH)),
                pl.BlockSpec((tm, d), row),
                pl.BlockSpec((ATTN_WIDTH + SSD_WIDTH, d), c2),
                pl.BlockSpec((1, SSD_WIDTH), c2),
                pl.BlockSpec((1, SSD_WIDTH), c2),
                pl.BlockSpec((1, 1, d), bat),
                pl.BlockSpec((1, d), c2),
                pl.BlockSpec((1, 1, d), bat),
                pl.BlockSpec((1, 1, d), bat),
                pl.BlockSpec((d, LANES), c2)]
    return pl.pallas_call(
        _outproj_kernel,
        grid=(t // tm,),
        in_specs=in_specs,
        out_specs=[pl.BlockSpec((tm, d), row), pl.BlockSpec((tm * _chunks(d), LANES), row),
                   pl.BlockSpec((N_EXPERTS, tm), lambda i: (0, i))],
        out_shape=[jax.ShapeDtypeStruct((t, d), F32),
                   jax.ShapeDtypeStruct((t * _chunks(d), LANES), U32),
                   jax.ShapeDtypeStruct((N_EXPERTS, t), F32)],
        compiler_params=_cparams(("arbitrary",), 52),
        name="outproj",
    )(attn, yf, yb, u, main, x, w_out, skip_e, ssd_g, g1, n2, sc2, sh2, rw)


def _first_index(hit, iota, size):
    return jnp.min(jnp.where(hit, iota, size), axis=0, keepdims=True)


def _route_kernel(lg_ref, bias_ref, su_ref, idx_ref, w_ref, rank_ref, cnt_ref, carry_scr):
    i = pl.program_id(0)
    tm = lg_ref.shape[1]

    @pl.when(i == 0)
    def _():
        carry_scr[...] = jnp.zeros(carry_scr.shape, F32)

    scores = _sigmoid(lg_ref[...])
    sel = scores + bias_ref[...][:, 0:1]
    e8 = EXPERTS_PER_GROUP
    i8 = lax.broadcasted_iota(I32, (e8, tm), 0).astype(F32)
    gs = []
    for g in range(N_EXPERT_GROUPS):
        blk = sel[g * e8:(g + 1) * e8, :]
        m1 = jnp.max(blk, axis=0, keepdims=True)
        f1 = _first_index(blk == m1, i8, e8)
        m2 = jnp.max(jnp.where(i8 == f1, NEG_BIG, blk), axis=0, keepdims=True)
        gs.append(m1 + m2)
    gsc = jnp.concatenate(gs, axis=0)
    ig = lax.broadcasted_iota(I32, gsc.shape, 0).astype(F32)
    keep = jnp.zeros(gsc.shape, F32)
    for _ in range(TOPK_GROUPS):
        m = jnp.max(gsc, axis=0, keepdims=True)
        f = _first_index(gsc == m, ig, N_EXPERT_GROUPS)
        keep = jnp.where(ig == f, 1.0, keep)
        gsc = jnp.where(ig == f, NEG_BIG, gsc)
    keep_e = jnp.concatenate(
        [jnp.broadcast_to(keep[g:g + 1, :], (e8, tm)) for g in range(N_EXPERT_GROUPS)], axis=0)
    cand = jnp.where(keep_e > 0.5, sel, NEG_BIG)
    ie = lax.broadcasted_iota(I32, cand.shape, 0).astype(F32)
    chosen = jnp.zeros(cand.shape, F32)
    idxs, svals, hits = [], [], []
    for _ in range(TOP_K):
        m = jnp.max(cand, axis=0, keepdims=True)
        f = _first_index(cand == m, ie, N_EXPERTS)
        hit = ie == f
        idxs.append(f)
        svals.append(jnp.sum(jnp.where(hit, scores, 0.0), axis=0, keepdims=True))
        hits.append(hit)
        chosen = jnp.where(hit, 1.0, chosen)
        cand = jnp.where(hit, NEG_BIG, cand)
    s_all = jnp.concatenate(svals, axis=0)
    idx_ref[...] = jnp.concatenate(idxs, axis=0).astype(I32)
    w_ref[...] = ROUTED_SCALE * s_all / jnp.sum(s_all, axis=0, keepdims=True)
    before = _dot(chosen.astype(BF16), su_ref[...]) + carry_scr[...][:, 0:1]
    ranks = [jnp.sum(jnp.where(h, before, 0.0), axis=0, keepdims=True) for h in hits]
    rank_ref[...] = jnp.concatenate(ranks, axis=0).astype(I32)
    carry_scr[...] = carry_scr[...] + jnp.sum(chosen, axis=1, keepdims=True)
    cnt_ref[...] = carry_scr[...].astype(I32)


def _route(logits_t, bias_col):
    e, t = logits_t.shape
    tm = ROUTE_TILE
    su = jnp.triu(jnp.ones((tm, tm), F32), 1).astype(BF16)
    col = lambda i: (0, i)
    return pl.pallas_call(
        _route_kernel,
        grid=(t // tm,),
        in_specs=[pl.BlockSpec((e, tm), col),
                  pl.BlockSpec((e, LANES), lambda i: (0, 0)),
                  pl.BlockSpec((tm, tm), lambda i: (0, 0))],
        out_specs=[pl.BlockSpec((TOP_K, tm), col), pl.BlockSpec((TOP_K, tm), col),
                   pl.BlockSpec((TOP_K, tm), col), pl.BlockSpec((e, LANES), lambda i: (0, 0))],
        out_shape=[jax.ShapeDtypeStruct((TOP_K, t), I32), jax.ShapeDtypeStruct((TOP_K, t), F32),
                   jax.ShapeDtypeStruct((TOP_K, t), I32), jax.ShapeDtypeStruct((e, LANES), I32)],
        scratch_shapes=[pltpu.VMEM((e, LANES), F32)],
        compiler_params=_cparams(("arbitrary",), 40),
        name="route",
    )(logits_t, bias_col, su)


def _slots_kernel(idx_ref, rank_ref, ps_ref, d_ref, dc_ref):
    tm = idx_ref.shape[1]
    ie = lax.broadcasted_iota(I32, (N_EXPERTS, tm), 0)
    ps = ps_ref[...][:, 0:1]
    rows = []
    for k in range(TOP_K):
        hit = ie == idx_ref[k:k + 1, :]
        rows.append(jnp.sum(jnp.where(hit, ps, 0.0), axis=0, keepdims=True))
    dest = jnp.concatenate(rows, axis=0).astype(I32) + rank_ref[...]
    d_ref[0] = dest
    tc = dc_ref.shape[2]
    for s in range(tm // tc):
        dc_ref[s] = dest[:, s * tc:(s + 1) * tc]


def _slots(idx_t, rank_t, ps_col):
    t = idx_t.shape[1]
    tm = TOK_TILE
    tc = COMB_TILE
    col = lambda i: (0, i)
    return pl.pallas_call(
        _slots_kernel,
        grid=(t // tm,),
        in_specs=[pl.BlockSpec((TOP_K, tm), col), pl.BlockSpec((TOP_K, tm), col),
                  pl.BlockSpec((N_EXPERTS, LANES), lambda i: (0, 0))],
        out_specs=[pl.BlockSpec((1, TOP_K, tm), lambda i: (i, 0, 0)),
                   pl.BlockSpec((tm // tc, TOP_K, tc), lambda i: (i, 0, 0))],
        out_shape=[jax.ShapeDtypeStruct((t // tm, TOP_K, tm), I32),
                   jax.ShapeDtypeStruct((t // tc, TOP_K, tc), I32)],
        compiler_params=_cparams(("arbitrary",), 40),
        name="slots",
    )(idx_t, rank_t, ps_col)


def _row_copy(src, dst, sem):
    return pltpu.make_async_copy(src, dst, sem)


def _dispatch_kernel(pend_ref, dest_ref, h_ref, xs_ref, zero_scr, sem, *, nch):
    i = pl.program_id(0)
    tm = h_ref.shape[0] // nch
    tile_rows = zero_scr.shape[0]

    @pl.when(i == 0)
    def _():
        zero_scr[...] = jnp.zeros(zero_scr.shape, zero_scr.dtype)

        def pad_copy(e):
            start_row = pl.multiple_of(pend_ref[e + 1] * nch - tile_rows, tile_rows)
            return _row_copy(zero_scr, xs_ref.at[pl.ds(start_row, tile_rows), :], sem.at[0])

        def start(e, carry):
            @pl.when(pend_ref[e + 1] > pend_ref[e])
            def _():
                pad_copy(e).start()
            return carry

        def wait(e, carry):
            @pl.when(pend_ref[e + 1] > pend_ref[e])
            def _():
                pad_copy(e).wait()
            return carry

        lax.fori_loop(0, N_EXPERTS, start, 0)
        lax.fori_loop(0, N_EXPERTS, wait, 0)

    def tok_copy(t, k):
        src = pl.multiple_of(t * nch, nch)
        dst = pl.multiple_of(dest_ref[0, k, t] * nch, nch)
        return _row_copy(h_ref.at[pl.ds(src, nch), :], xs_ref.at[pl.ds(dst, nch), :], sem.at[1])

    def start(t, carry):
        for k in range(TOP_K):
            tok_copy(t, k).start(priority=k % 2)
        return carry

    def wait(t, carry):
        for k in range(TOP_K):
            tok_copy(t, k).wait()
        return carry

    lax.fori_loop(0, tm, start, 0)
    lax.fori_loop(0, tm, wait, 0)


def _dispatch(pad_end_ext, dest3, h2, n_slots, nch):
    t = h2.shape[0] // nch
    tm = TOK_TILE
    grid_spec = pltpu.PrefetchScalarGridSpec(
        num_scalar_prefetch=1,
        grid=(t // tm,),
        in_specs=[pl.BlockSpec((1, TOP_K, tm), lambda i, pe: (i, 0, 0), memory_space=pltpu.SMEM),
                  pl.BlockSpec((tm * nch, LANES), lambda i, pe: (i, 0))],
        out_specs=pl.BlockSpec(memory_space=pl.ANY),
        scratch_shapes=[pltpu.VMEM((MOE_TILE * nch, LANES), U32), pltpu.SemaphoreType.DMA((2,))],
    )
    return pl.pallas_call(
        functools.partial(_dispatch_kernel, nch=nch),
        grid_spec=grid_spec,
        out_shape=jax.ShapeDtypeStruct((n_slots * nch, LANES), U32),
        compiler_params=_cparams(("arbitrary",), 40),
        name="dispatch",
    )(pad_end_ext, dest3, h2)


def _experts_kernel(first_ref, ntile_ref, xs_ref, wg_ref, wu_ref, wd_ref, ys_ref,
                    wg_scr, wu_scr, wd_scr, wg_buf, wu_buf, wd_buf, xbuf, ybuf, xsem, ysem, wsem):
    e = pl.program_id(0)
    n = ntile_ref[e]
    first = first_ref[e]
    rows = xbuf.shape[1]
    nch = _chunks(wg_scr.shape[0])
    te = rows // nch
    wslot = e % 2

    has_next = e + 1 < pl.num_programs(0)

    def w_copies(ex, s, c):
        out = []
        for i, (src, dst) in enumerate(((wg_ref, wg_buf), (wu_ref, wu_buf), (wd_ref, wd_buf))):
            dr = dst.shape[1] // WEIGHT_CHUNKS
            r0 = pl.multiple_of(c * dr, dr)
            out.append(_row_copy(src.at[ex, pl.ds(r0, dr), :], dst.at[s, pl.ds(r0, dr), :],
                                 wsem.at[s, i]))
        return out

    def start_next_chunk(c):
        @pl.when(has_next & (c < WEIGHT_CHUNKS))
        def _():
            for cp in w_copies(e + 1, 1 - wslot, c):
                cp.start()

    @pl.when(e == 0)
    def _():
        for c in range(WEIGHT_CHUNKS):
            for cp in w_copies(0, 0, c):
                cp.start()

    nx = xbuf.shape[0]
    ny = ybuf.shape[0]

    def x_copy(j):
        r0 = pl.multiple_of((first + j) * rows, rows)
        return _row_copy(xs_ref.at[pl.ds(r0, rows), :], xbuf.at[j % nx], xsem.at[j % nx])

    def y_copy(j):
        r0 = pl.multiple_of((first + j) * rows, rows)
        return _row_copy(ybuf.at[j % ny], ys_ref.at[pl.ds(r0, rows), :], ysem.at[j % ny])

    for j0 in range(nx - 1):
        @pl.when(j0 < n)
        def _():
            x_copy(j0).start(priority=TILE_DMA_PRIORITY)

    for c in range(WEIGHT_CHUNKS):
        for cp in w_copies(e, wslot, c):
            cp.wait()
    wg_scr[...] = wg_buf[wslot].astype(BF16)
    wu_scr[...] = wu_buf[wslot].astype(BF16)
    wd_scr[...] = wd_buf[wslot].astype(BF16)

    @pl.when(n > 0)
    def _():

        def tile(j, carry):
            @pl.when(j + nx - 1 < n)
            def _():
                x_copy(j + nx - 1).start(priority=TILE_DMA_PRIORITY)

            start_next_chunk(j)
            x_copy(j).wait()

            @pl.when(j >= ny)
            def _():
                y_copy(j - ny).wait()

            xb = _load_rows(xbuf.at[j % nx], te, nch).astype(BF16)
            act = _silu(_dot(xb, wg_scr[...])) * _dot(xb, wu_scr[...])
            _store_rows(ybuf.at[j % ny], _dot(act.astype(BF16), wd_scr[...]), nch)
            y_copy(j).start(priority=TILE_DMA_PRIORITY)
            return carry

        lax.fori_loop(0, n, tile, 0)

        for back in range(ny, 0, -1):
            @pl.when(n >= back)
            def _():
                y_copy(n - back).wait()

    def rest(c, carry):
        start_next_chunk(c)
        return carry

    lax.fori_loop(jnp.minimum(n, WEIGHT_CHUNKS), WEIGHT_CHUNKS, rest, 0)


def _experts(first_tile, n_tile, xs, wg, wu, wd):
    n_exp, d, f = wg.shape
    nch = _chunks(d)
    rows = MOE_TILE * nch
    grid_spec = pltpu.PrefetchScalarGridSpec(
        num_scalar_prefetch=2,
        grid=(n_exp,),
        in_specs=[pl.BlockSpec(memory_space=pl.ANY), pl.BlockSpec(memory_space=pl.ANY),
                  pl.BlockSpec(memory_space=pl.ANY), pl.BlockSpec(memory_space=pl.ANY)],
        out_specs=pl.BlockSpec(memory_space=pl.ANY),
        scratch_shapes=[pltpu.VMEM((d, f), BF16), pltpu.VMEM((d, f), BF16), pltpu.VMEM((f, d), BF16),
                        pltpu.VMEM((2, d, f), F32), pltpu.VMEM((2, d, f), F32),
                        pltpu.VMEM((2, f, d), F32),
                        pltpu.VMEM((3, rows, LANES), U32), pltpu.VMEM((2, rows, LANES), U32),
                        pltpu.SemaphoreType.DMA((3,)), pltpu.SemaphoreType.DMA((2,)),
                        pltpu.SemaphoreType.DMA((2, 3))],
    )
    return pl.pallas_call(
        _experts_kernel,
        grid_spec=grid_spec,
        out_shape=jax.ShapeDtypeStruct(xs.shape, U32),
        compiler_params=_cparams(("arbitrary",), 56),
        name="experts",
    )(first_tile, n_tile, xs, wg, wu, wd)


def _combine_kernel(dest_ref, dnext_ref, ys_ref, wt_ref, h_ref, x1_ref, sg_ref, su_ref, sd_ref,
                    g2_ref, fg_ref, eye_ref, o_ref, buf, sem):
    i = pl.program_id(0)
    n = pl.num_programs(0)
    tm = x1_ref.shape[0]
    nch = _chunks(x1_ref.shape[1])
    slot = i % 2

    def row_copy(d_ref, s, t, k):
        src = pl.multiple_of(d_ref[0, k, t] * nch, nch)
        dst = pl.multiple_of(t * nch, nch)
        return _row_copy(ys_ref.at[pl.ds(src, nch), :], buf.at[s, k, pl.ds(dst, nch), :], sem.at[s])

    def start_tile(d_ref, s):
        def body(t, carry):
            for k in range(TOP_K):
                row_copy(d_ref, s, t, k).start(priority=k % 2)
            return carry
        lax.fori_loop(0, tm, body, 0)

    @pl.when(i == 0)
    def _():
        start_tile(dest_ref, 0)

    @pl.when(i + 1 < n)
    def _():
        start_tile(dnext_ref, 1 - slot)

    hb = _load_rows(h_ref, tm, nch).astype(BF16)
    act = _silu(_dot(hb, sg_ref[...])) * _dot(hb, su_ref[...])
    ffn = _dot(act.astype(BF16), sd_ref[...])
    w_col = _dot_rhs_f32(eye_ref[...], wt_ref[...], _NT)

    def wait(t, carry):
        for k in range(TOP_K):
            row_copy(dest_ref, slot, t, k).wait()
        return carry

    lax.fori_loop(0, tm, wait, 0)
    for k in range(TOP_K):
        ffn = ffn + w_col[:, k:k + 1] * _load_rows(buf.at[slot, k], tm, nch)
    x2 = x1_ref[...] + g2_ref[0] * ffn
    o_ref[...] = _rms(x2) * fg_ref[...]


def _combine(dest3, ys, w_t, h2, x1, sg, su, sd, g2, fg, *, seq):
    t, d = x1.shape
    nch = _chunks(d)
    tm = COMB_TILE
    per = seq // tm
    f = sg.shape[1]
    eye = jnp.eye(tm, dtype=BF16)
    row = lambda i: (i, 0)
    c2 = lambda i: (0, 0)
    last = t // tm - 1
    return pl.pallas_call(
        _combine_kernel,
        grid=(t // tm,),
        in_specs=[pl.BlockSpec((1, TOP_K, tm), lambda i: (i, 0, 0), memory_space=pltpu.SMEM),
                  pl.BlockSpec((1, TOP_K, tm), lambda i: (jnp.minimum(i + 1, last), 0, 0),
                               memory_space=pltpu.SMEM),
                  pl.BlockSpec(memory_space=pl.ANY),
                  pl.BlockSpec((TOP_K, tm), lambda i: (0, i)),
                  pl.BlockSpec((tm * nch, LANES), row),
                  pl.BlockSpec((tm, d), row),
                  pl.BlockSpec((d, f), c2), pl.BlockSpec((d, f), c2), pl.BlockSpec((f, d), c2),
                  pl.BlockSpec((1, 1, d), lambda i: (i // per, 0, 0)),
                  pl.BlockSpec((1, d), c2),
                  pl.BlockSpec((tm, tm), c2)],
        out_specs=pl.BlockSpec((tm, d), row),
        out_shape=jax.ShapeDtypeStruct((t, d), F32),
        scratch_shapes=[pltpu.VMEM((2, TOP_K, tm * nch, LANES), U32),
                        pltpu.SemaphoreType.DMA((2,))],
        compiler_params=_cparams(("arbitrary",), 48),
        name="combine",
    )(dest3, dest3, ys, w_t, h2, x1, sg, su, sd, g2, fg, eye)


def _rope_tables(seq):
    n_rows = seq // GRID_W
    rows = np.repeat(np.arange(n_rows, dtype=np.float32), GRID_W)
    cols = np.tile(np.arange(GRID_W, dtype=np.float32), n_rows)
    expo = -np.arange(0, ROPE_AXIS_DIM, 2, dtype=np.float32) / np.float32(ROPE_AXIS_DIM)
    inv = np.power(np.float32(ROPE_THETA), expo).astype(np.float32)
    ar = (rows[:, None] * inv[None, :]).astype(np.float64)
    ac = (cols[:, None] * inv[None, :]).astype(np.float64)
    cos = np.concatenate([np.cos(ar), np.cos(ac), np.cos(ar), np.cos(ac)], axis=1)
    sin = np.concatenate([-np.sin(ar), -np.sin(ac), np.sin(ar), np.sin(ac)], axis=1)
    return jnp.asarray(cos, F32), jnp.asarray(sin, F32)


def _layer(x, c, ctx, c_ctx, w_ada, b_ada, norm1_g, norm2_g, w_in, attn_sink, conv_w, conv_b,
           dt_bias, a_log, d_skip, ssd_norm_g, w_out, router_w, router_bias, wg, wu, wd,
           sw_gate, sw_up, sw_down, final_norm_g):
    batch, seq, d = x.shape
    ctx_len = ctx.shape[1]
    t = batch * seq

    cvec = jnp.zeros((SUBLANES, d), F32).at[:batch].set(c).at[batch].set(c_ctx)
    mod = _ada(cvec, w_ada, b_ada[None, :])
    sh1, sc1, g1, sh2, sc2, g2 = [mod[:batch, i * d:(i + 1) * d][:, None, :] for i in range(6)]
    sh1c, sc1c = [jnp.broadcast_to(mod[batch, i * d:(i + 1) * d][None, None, :], (batch, 1, d))
                  for i in range(2)]

    w_main, w_dt = _wprep(w_in.T)
    cos, sin = _rope_tables(seq)
    n1 = norm1_g[None, :]
    n_main = MAIN_WIDTH // IN_TILE_N
    ctx0 = COL_XBC // IN_TILE_N
    main, dtp = _inproj(x.reshape(t, d), n1, sc1, sh1, w_main, w_dt, cos, sin,
                        seq=seq, tm=min(seq, 1024), rope=True, tile0=0, n_tiles=n_main)
    main_c, dtp_c = _inproj(ctx.reshape(batch * ctx_len, d), n1, sc1c, sh1c, w_main, w_dt, None, None,
                            seq=batch * ctx_len, tm=batch * ctx_len, rope=False, tile0=ctx0,
                            n_tiles=n_main - ctx0)

    attn = _attn(main, main_c, attn_sink, batch=batch, seq=seq, ctx_len=ctx_len)

    cw8 = jnp.pad(conv_w, ((0, SUBLANES - SSD_CONV), (0, 0)))
    u = _conv(main, COL_XBC, cw8, conv_b[None, :], seq=seq)
    u_c = _conv(main_c, 0, cw8, conv_b[None, :], seq=ctx_len)
    nd = N_DIRS * SSD_HEADS
    bias = jnp.pad(dt_bias.reshape(1, nd), ((0, 0), (0, LANES - nd)))
    a_neg = jnp.pad(-jnp.exp(a_log.reshape(1, nd)), ((0, 0), (0, LANES - nd)))
    lane_head = np.arange(SSD_WIDTH) // SSD_HEAD_DIM
    e_np = np.zeros((N_DIRS, LANES, SSD_WIDTH), np.float32)
    for dd in range(N_DIRS):
        e_np[dd, dd * SSD_HEADS + lane_head, np.arange(SSD_WIDTH)] = 1.0
    e_mat = jnp.asarray(e_np, BF16)

    def tri(q):
        lo = np.tril(np.ones((q, q), np.float32))
        return jnp.asarray(np.stack([lo, lo.T]), BF16)

    qc = min(SSD_CHUNK, ctx_len)
    h_ctx = _ssd(u_c, dtp_c, None, bias, a_neg, e_mat, tri(qc), batch=batch, seq=ctx_len, q=qc)
    ql = min(SSD_CHUNK, seq)
    yf, yb = _ssd(u, dtp, h_ctx, bias, a_neg, e_mat, tri(ql), batch=batch, seq=seq, q=ql)

    skip_e = jnp.repeat(d_skip, SSD_HEAD_DIM)[None, :]
    x1, h2, logits_t = _outproj(attn, yf, yb, u, main, x.reshape(t, d), w_out.astype(BF16),
                                skip_e, ssd_norm_g[None, :], g1, norm2_g[None, :], sc2, sh2,
                                jnp.pad(router_w, ((0, 0), (0, LANES - N_EXPERTS))), seq=seq)

    bias_col = jnp.broadcast_to(router_bias[:, None], (N_EXPERTS, LANES))
    idx_t, w_t, rank_t, cnt = _route(logits_t, bias_col)
    counts = cnt[:, 0]
    te = MOE_TILE
    padded = (counts + te - 1) // te * te
    pad_end = jnp.cumsum(padded)
    pad_start = pad_end - padded
    n_slots = -(-(t * TOP_K + N_EXPERTS * (te - 1)) // te) * te
    ps_col = jnp.broadcast_to(pad_start.astype(F32)[:, None], (N_EXPERTS, LANES))
    dest3, dest3c = _slots(idx_t, rank_t, ps_col)
    pad_end_ext = jnp.concatenate([jnp.zeros((1,), I32), pad_end.astype(I32)])

    xs = _dispatch(pad_end_ext, dest3, h2, n_slots, _chunks(d))
    ys = _experts((pad_start // te).astype(I32), (padded // te).astype(I32), xs, wg, wu, wd)
    out = _combine(dest3c, ys, w_t, h2, x1, sw_gate.astype(BF16), sw_up.astype(BF16),
                   sw_down.astype(BF16), g2, final_norm_g[None, :], seq=seq)
    return out.reshape(batch, seq, d)


def kernel(x, c, ctx, c_ctx, w_ada, b_ada, norm1_g, norm2_g, w_in, attn_sink, conv_w, conv_b, dt_bias, a_log, d_skip, ssd_norm_g, w_out, router_w, router_bias, expert_w_gate, expert_w_up, expert_w_down, shared_w_gate, shared_w_up, shared_w_down, final_norm_g):
    assert w_ada.shape[0] == 1, "single-layer stack only"
    return _layer(x, c, ctx, c_ctx, w_ada[0], b_ada[0], norm1_g[0], norm2_g[0], w_in[0],
                  attn_sink[0], conv_w[0], conv_b[0], dt_bias[0], a_log[0], d_skip[0],
                  ssd_norm_g[0], w_out[0], router_w[0], router_bias[0], expert_w_gate[0],
                  expert_w_up[0], expert_w_down[0], shared_w_gate[0], shared_w_up[0],
                  shared_w_down[0], final_norm_g)
```

```python
import functools

import jax
import jax.numpy as jnp
import numpy as np
from jax import lax
from jax.experimental import pallas as pl
from jax.experimental.pallas import tpu as pltpu

F32 = jnp.float32
BF16 = jnp.bfloat16
I32 = jnp.int32

EPS = 1e-6
GRID_W = 64
N_Q_HEADS = 8
N_KV_HEADS = 2
Q_PER_KV = N_Q_HEADS // N_KV_HEADS
HEAD_DIM = 128
ATTN_WIDTH = N_Q_HEADS * HEAD_DIM
KV_WIDTH = N_KV_HEADS * HEAD_DIM
WINDOW = 128
ATTN_BLOCK = 128
ROPE_THETA = 10000.0
ROPE_AXIS_DIM = HEAD_DIM // 2
SSD_HEADS = 16
SSD_HEAD_DIM = 64
SSD_WIDTH = SSD_HEADS * SSD_HEAD_DIM
SSD_GROUPS = 4
SSD_HEADS_PER_GROUP = SSD_HEADS // SSD_GROUPS
SSD_STATE = 128
SSD_CONV = 5
N_DIRS = 2
XBC_WIDTH = SSD_WIDTH + 2 * SSD_GROUPS * SSD_STATE
N_EXPERTS = 64
N_EXPERT_GROUPS = 8
EXPERTS_PER_GROUP = N_EXPERTS // N_EXPERT_GROUPS
TOPK_GROUPS = 4
TOP_K = 8
EXPERT_DIM = 512
ROUTED_SCALE = 2.5

LANES = 128
SUBLANES = 8
NEG_BIG = -1e30

COL_Q = 0
COL_Z = ATTN_WIDTH
COL_XBC = COL_Z + SSD_WIDTH
COL_K = COL_XBC + XBC_WIDTH
COL_V = COL_K + KV_WIDTH
MAIN_WIDTH = COL_V + KV_WIDTH

IN_TILE_N = 512
SSD_CHUNK = 256
MOE_TILE = 256
ROUTE_TILE = 512
TOK_TILE = 256
COMB_TILE = 128
TILE_DMA_PRIORITY = 1
WEIGHT_CHUNKS = 8

_NT = (((1,), (1,)), ((), ()))


def _cparams(sem, vmem_mb):
    return pltpu.CompilerParams(dimension_semantics=sem, vmem_limit_bytes=vmem_mb * 1024 * 1024)


def _dot(a, b, dims=None):
    if dims is None:
        return jnp.dot(a, b, preferred_element_type=F32)
    return lax.dot_general(a, b, dims, preferred_element_type=F32)


def _split(a):
    hi = a.astype(BF16)
    lo = (a - hi.astype(F32)).astype(BF16)
    return hi, lo


def _dot_lhs_f32(a, b_exact, dims=None):
    hi, lo = _split(a)
    return _dot(hi, b_exact, dims) + _dot(lo, b_exact, dims)


def _dot_rhs_f32(a_exact, b, dims=None):
    hi, lo = _split(b)
    return _dot(a_exact, hi, dims) + _dot(a_exact, lo, dims)


def _dot_f32(a, b, dims=None):
    ah, al = _split(a)
    bh, bl = _split(b)
    return _dot(ah, bh, dims) + (_dot(al, bh, dims) + _dot(ah, bl, dims))


def _sigmoid(x):
    return 1.0 / (1.0 + jnp.exp(-x))


def _silu(x):
    return x * _sigmoid(x)


def _rms(x):
    return x * lax.rsqrt(jnp.mean(x * x, axis=-1, keepdims=True) + EPS)


U32 = jnp.uint32
_HI_MASK = 0xFFFF0000


def _chunks(d):
    return d // (2 * LANES)


def _load_rows(ref, rows, nch):
    words = [ref[pl.ds(j, rows, stride=nch), :] for j in range(nch)]
    lo = [lax.bitcast_convert_type(w << 16, F32) for w in words]
    hi = [lax.bitcast_convert_type(w & jnp.uint32(_HI_MASK), F32) for w in words]
    return jnp.concatenate(lo + hi, axis=1)


def _store_rows(ref, val, nch):
    rows, d = val.shape
    bits = lax.bitcast_convert_type(val.astype(BF16).astype(F32), U32)
    for j in range(nch):
        lo = bits[:, j * LANES:(j + 1) * LANES] >> 16
        hi = bits[:, d // 2 + j * LANES:d // 2 + (j + 1) * LANES] & jnp.uint32(_HI_MASK)
        ref[pl.ds(j, rows, stride=nch), :] = lo | hi


def _ada_kernel(c_ref, w_ref, b_ref, o_ref):
    o_ref[...] = _dot_f32(_silu(c_ref[...]), w_ref[...]) + b_ref[...]


def _ada(cvec, w, b):
    m, d = cvec.shape
    n = w.shape[1]
    tn = 1024
    return pl.pallas_call(
        _ada_kernel,
        grid=(n // tn,),
        in_specs=[pl.BlockSpec((m, d), lambda j: (0, 0)),
                  pl.BlockSpec((d, tn), lambda j: (0, j)),
                  pl.BlockSpec((1, tn), lambda j: (0, j))],
        out_specs=pl.BlockSpec((m, tn), lambda j: (0, j)),
        out_shape=jax.ShapeDtypeStruct((m, n), F32),
        compiler_params=_cparams(("arbitrary",), 40),
        name="ada",
    )(cvec, w, b)


def _rope_mix(acc, cos, sin):
    heads = acc.shape[1] // HEAD_DIM
    partner = jnp.concatenate(
        [pltpu.roll(acc[:, h * HEAD_DIM:(h + 1) * HEAD_DIM], HEAD_DIM // 2, 1) for h in range(heads)],
        axis=1)
    return acc * cos + partner * sin


def _inproj_kernel(*refs, rope, n_rope_tiles, kv_tile):
    if rope:
        x_ref, g_ref, sc_ref, sh_ref, w_ref, wdt_ref, cos_ref, sin_ref, o_ref, dt_ref, h_scr = refs
    else:
        x_ref, g_ref, sc_ref, sh_ref, w_ref, wdt_ref, o_ref, dt_ref, h_scr = refs
    j = pl.program_id(1)

    @pl.when(j == 0)
    def _():
        gain = g_ref[...] * (1.0 + sc_ref[0])
        h = _rms(x_ref[...]) * gain + sh_ref[0]
        hb = h.astype(BF16)
        h_scr[...] = hb
        dt_ref[...] = _dot(hb, wdt_ref[...])

    acc = _dot(h_scr[...], w_ref[0])
    if not rope:
        o_ref[...] = acc.astype(o_ref.dtype)
        return
    reps = acc.shape[1] // HEAD_DIM

    @pl.when(j < n_rope_tiles)
    def _():
        cos = jnp.concatenate([cos_ref[...]] * reps, axis=1)
        sin = jnp.concatenate([sin_ref[...]] * reps, axis=1)
        o_ref[...] = _rope_mix(acc, cos, sin).astype(o_ref.dtype)

    @pl.when(j == kv_tile)
    def _():
        nk = KV_WIDTH // HEAD_DIM
        ones = jnp.ones_like(cos_ref[...])
        cos = jnp.concatenate([cos_ref[...]] * nk + [ones] * (reps - nk), axis=1)
        sin = jnp.concatenate([sin_ref[...]] * nk + [ones * 0.0] * (reps - nk), axis=1)
        o_ref[...] = _rope_mix(acc, cos, sin).astype(o_ref.dtype)

    @pl.when((j >= n_rope_tiles) & (j != kv_tile))
    def _():
        o_ref[...] = acc.astype(o_ref.dtype)


def _wprep_kernel(w_ref, wdt_ref, o_ref, odt_ref):
    j = pl.program_id(0)
    x = w_ref[...].T
    tn = x.shape[1]
    q = ROPE_AXIS_DIM // 2
    lane = lax.broadcasted_iota(I32, x.shape, 1)
    blk = (lane & (HEAD_DIM - 1)) // q
    swapped = jnp.where(blk == 1, pltpu.roll(x, tn - q, 1), jnp.where(blk == 2, pltpu.roll(x, q, 1), x))
    n_q = ATTN_WIDTH // tn
    kv = COL_K // tn

    @pl.when(j == 0)
    def _():
        rows = wdt_ref[...]
        pad = jnp.zeros((LANES - rows.shape[0], rows.shape[1]), rows.dtype)
        odt_ref[...] = jnp.concatenate([rows, pad], axis=0).T.astype(odt_ref.dtype)

    @pl.when(j < n_q)
    def _():
        o_ref[0] = swapped.astype(o_ref.dtype)

    @pl.when(j == kv)
    def _():
        o_ref[0] = jnp.where(lane < KV_WIDTH, swapped, x).astype(o_ref.dtype)

    @pl.when((j >= n_q) & (j != kv))
    def _():
        o_ref[0] = x.astype(o_ref.dtype)


def _wprep(w_in_t):
    d = w_in_t.shape[1]
    tn = IN_TILE_N
    n_q = ATTN_WIDTH // tn
    n_tiles = MAIN_WIDTH // tn
    src = lambda j: (jnp.where(j < n_q, j, jnp.where(j < n_tiles - 1, j + 1, n_q)), 0)
    n_dt = w_in_t.shape[0] - MAIN_WIDTH
    return pl.pallas_call(
        _wprep_kernel,
        grid=(n_tiles,),
        in_specs=[pl.BlockSpec((tn, d), src),
                  pl.BlockSpec((n_dt, d), lambda j: (MAIN_WIDTH // n_dt, 0))],
        out_specs=[pl.BlockSpec((1, d, tn), lambda j: (j, 0, 0)),
                   pl.BlockSpec((d, LANES), lambda j: (0, 0))],
        out_shape=[jax.ShapeDtypeStruct((n_tiles, d, tn), BF16),
                   jax.ShapeDtypeStruct((d, LANES), BF16)],
        compiler_params=_cparams(("arbitrary",), 40),
        name="wprep",
    )(w_in_t, w_in_t)


def _inproj(x, g, sc, sh, w, wdt, cos, sin, *, seq, tm, rope, tile0, n_tiles):
    t, d = x.shape
    tn = IN_TILE_N
    wn = n_tiles * tn
    per = seq // tm
    in_specs = [pl.BlockSpec((tm, d), lambda i, j: (i, 0)),
                pl.BlockSpec((1, d), lambda i, j: (0, 0)),
                pl.BlockSpec((1, 1, d), lambda i, j: (i // per, 0, 0)),
                pl.BlockSpec((1, 1, d), lambda i, j: (i // per, 0, 0)),
                pl.BlockSpec((1, d, tn), lambda i, j: (j + tile0, 0, 0)),
                pl.BlockSpec((d, LANES), lambda i, j: (0, 0))]
    args = [x, g, sc, sh, w, wdt]
    if rope:
        in_specs += [pl.BlockSpec((tm, HEAD_DIM), lambda i, j: (i % per, 0)),
                     pl.BlockSpec((tm, HEAD_DIM), lambda i, j: (i % per, 0))]
        args += [cos, sin]
    kern = functools.partial(_inproj_kernel, rope=rope, n_rope_tiles=ATTN_WIDTH // tn,
                             kv_tile=COL_K // tn)
    return pl.pallas_call(
        kern,
        grid=(t // tm, wn // tn),
        in_specs=in_specs,
        out_specs=[pl.BlockSpec((tm, tn), lambda i, j: (i, j)),
                   pl.BlockSpec((tm, LANES), lambda i, j: (i, 0))],
        out_shape=[jax.ShapeDtypeStruct((t, wn), BF16), jax.ShapeDtypeStruct((t, LANES), F32)],
        scratch_shapes=[pltpu.VMEM((tm, d), BF16)],
        compiler_params=_cparams(("arbitrary", "arbitrary"), 48),
        name="inproj_rope" if rope else "inproj",
    )(*args)


def _conv_kernel(x_ref, w_ref, b_ref, o_ref, pad_scr, *, seq, rows):
    halo = SUBLANES
    cw = x_ref.shape[1]
    pad_scr[0:halo, :] = jnp.zeros((halo, cw), F32)
    pad_scr[seq + halo:seq + 2 * halo, :] = jnp.zeros((halo, cw), F32)
    pad_scr[halo:seq + halo, :] = x_ref[...].astype(F32)
    w = w_ref[...]
    bias = b_ref[...]
    for r0 in range(0, seq, rows):
        acc = jnp.broadcast_to(bias, (rows, cw))
        for tap in range(SSD_CONV):
            start = r0 + halo - SSD_CONV // 2 + tap
            acc = acc + w[tap:tap + 1, :] * pad_scr[start:start + rows, :]
        o_ref[r0:r0 + rows, :] = _silu(acc).astype(o_ref.dtype)


def _conv(main, col0, w8, b, *, seq):
    t = main.shape[0]
    cw = 256
    rows = min(seq, 256)
    cb0 = col0 // cw
    kern = functools.partial(_conv_kernel, seq=seq, rows=rows)
    return pl.pallas_call(
        kern,
        grid=(t // seq, XBC_WIDTH // cw),
        in_specs=[pl.BlockSpec((seq, cw), lambda b_, c: (b_, cb0 + c)),
                  pl.BlockSpec((SUBLANES, cw), lambda b_, c: (0, c)),
                  pl.BlockSpec((1, cw), lambda b_, c: (0, c))],
        out_specs=pl.BlockSpec((seq, cw), lambda b_, c: (b_, c)),
        out_shape=jax.ShapeDtypeStruct((t, XBC_WIDTH), BF16),
        scratch_shapes=[pltpu.VMEM((seq + 2 * SUBLANES, cw), F32)],
        compiler_params=_cparams(("arbitrary", "arbitrary"), 40),
        name="conv",
    )(main, w8, b)


def _attn_kernel(sink_ref, bias_ref, q_ref, kp_ref, kc_ref, kn_ref, vp_ref, vc_ref, vn_ref,
                 kx_ref, vx_ref, o_ref):
    blk = ATTN_BLOCK
    g = Q_PER_KV
    hd = HEAD_DIM
    scale = hd ** -0.5
    bias = jnp.concatenate([bias_ref[0]] * g, axis=0)
    row = lax.broadcasted_iota(I32, (g * blk, 1), 0)
    for hk in range(N_KV_HEADS):
        ks = slice(hk * hd, (hk + 1) * hd)
        q = jnp.concatenate([q_ref[:, (hk * g + i) * hd:(hk * g + i + 1) * hd] for i in range(g)],
                            axis=0)
        k_loc = jnp.concatenate([kp_ref[:, ks], kc_ref[:, ks], kn_ref[:, ks]], axis=0)
        v_loc = jnp.concatenate([vp_ref[:, ks], vc_ref[:, ks], vn_ref[:, ks]], axis=0)
        s_loc = _dot(q, k_loc, _NT) * scale + bias
        s_ctx = _dot(q, kx_ref[:, ks], _NT) * scale
        sink = jnp.zeros((g * blk, 1), F32)
        for i in range(g):
            sink = jnp.where((row >= i * blk) & (row < (i + 1) * blk), sink_ref[hk * g + i], sink)
        m = jnp.maximum(jnp.maximum(jnp.max(s_loc, axis=1, keepdims=True),
                                    jnp.max(s_ctx, axis=1, keepdims=True)), sink)
        p_loc = jnp.exp(s_loc - m)
        p_ctx = jnp.exp(s_ctx - m)
        den = (jnp.sum(p_loc, axis=1, keepdims=True) + jnp.sum(p_ctx, axis=1, keepdims=True)
               + jnp.exp(sink - m))
        o = _dot(p_loc.astype(BF16), v_loc) + _dot(p_ctx.astype(BF16), vx_ref[:, ks])
        o = o / den
        for i in range(g):
            o_ref[:, (hk * g + i) * hd:(hk * g + i + 1) * hd] = (
                o[i * blk:(i + 1) * blk, :].astype(o_ref.dtype))


def _attn(main, main_c, sink, *, batch, seq, ctx_len):
    t = main.shape[0]
    blk = ATTN_BLOCK
    nb = seq // blk
    kcol = COL_K // KV_WIDTH
    vcol = COL_V // KV_WIDTH
    kcol_c = XBC_WIDTH // KV_WIDTH
    vcol_c = kcol_c + 1

    def nbr(col, off):
        return pl.BlockSpec(
            (blk, KV_WIDTH), lambda b, i: (b * nb + jnp.clip(i + off, 0, nb - 1), col))

    r = np.arange(blk)[:, None]
    c = np.arange(3 * blk)[None, :]
    variants = []
    for v in range(4):
        lo = np.maximum(r, blk if v & 1 else 0)
        hi = np.minimum(r + 2 * WINDOW, 2 * blk - 1 if v & 2 else 3 * blk - 1)
        variants.append(np.where((c >= lo) & (c <= hi), 0.0, NEG_BIG))
    bias = jnp.asarray(np.stack(variants), F32)
    variant = lambda b, i: ((i == 0).astype(I32) + 2 * (i == nb - 1).astype(I32), 0, 0)

    in_specs = [pl.BlockSpec(memory_space=pltpu.SMEM),
                pl.BlockSpec((1, blk, 3 * blk), variant),
                pl.BlockSpec((blk, ATTN_WIDTH), lambda b, i: (b * nb + i, 0)),
                nbr(kcol, -1), nbr(kcol, 0), nbr(kcol, 1),
                nbr(vcol, -1), nbr(vcol, 0), nbr(vcol, 1),
                pl.BlockSpec((ctx_len, KV_WIDTH), lambda b, i: (b, kcol_c)),
                pl.BlockSpec((ctx_len, KV_WIDTH), lambda b, i: (b, vcol_c))]
    return pl.pallas_call(
        _attn_kernel,
        grid=(batch, nb),
        in_specs=in_specs,
        out_specs=pl.BlockSpec((blk, ATTN_WIDTH), lambda b, i: (b * nb + i, 0)),
        out_shape=jax.ShapeDtypeStruct((t, ATTN_WIDTH), BF16),
        compiler_params=_cparams(("arbitrary", "arbitrary"), 40),
        name="attn",
    )(sink, bias, main, main, main, main, main, main, main, main_c, main_c)


def _softplus(x):
    return jnp.maximum(x, 0.0) + jnp.log1p(jnp.exp(-jnp.abs(x)))


def _ssd_chunk(u_ref, dtp_ref, bias_ref, a_ref, e_ref, tri_ref, h_scr, y_ref, d):
    q = u_ref.shape[0]
    rev = d == 1
    p = SSD_HEAD_DIM
    gw = SSD_HEADS_PER_GROUP * p
    dt = _softplus(dtp_ref[...] + bias_ref[...])
    dta = dt * a_ref[...]
    tri = tri_ref[d]
    a_cum = _dot_rhs_f32(tri, dta)
    a_row = _dot_lhs_f32(dta.T, tri_ref[1 - d])
    edge = a_cum[0:1, :] if rev else a_cum[q - 1:q, :]
    ea = jnp.exp(a_cum)
    w_state = dt * jnp.exp(edge - a_cum)
    e_mat = e_ref[d]
    dt_e = _dot(dt.astype(BF16), e_mat)
    ws_e = _dot(w_state.astype(BF16), e_mat)
    x = u_ref[:, 0:SSD_WIDTH].astype(F32)
    xdd = (x * ws_e).astype(BF16)
    li = lax.broadcasted_iota(I32, (q, q), 0)
    si = lax.broadcasted_iota(I32, (q, q), 1)
    keep = (si >= li) if rev else (si <= li)
    ea_e = _dot_lhs_f32(ea, e_mat)
    edge_e = ea_e[0:1, :] if rev else ea_e[q - 1:q, :]
    if y_ref is not None:
        xd = (x * dt_e).astype(BF16)
    for g in range(SSD_GROUPS):
        b_t = u_ref[:, SSD_WIDTH + g * SSD_STATE:SSD_WIDTH + (g + 1) * SSD_STATE].T
        c0 = SSD_WIDTH + SSD_GROUPS * SSD_STATE + g * SSD_STATE
        c_g = u_ref[:, c0:c0 + SSD_STATE]
        h_t = h_scr[d, g]
        if y_ref is not None:
            cb = _dot(c_g, b_t)
            ys = []
            for r in range(SSD_HEADS_PER_GROUP):
                hd = g * SSD_HEADS_PER_GROUP + r
                ln = d * SSD_HEADS + hd
                seg = a_cum[:, ln:ln + 1] - a_row[ln:ln + 1, :]
                decay = jnp.exp(jnp.where(keep, seg, NEG_BIG))
                ys.append(_dot((cb * decay).astype(BF16), xd[:, hd * p:(hd + 1) * p]))
            y_off = _dot(c_g, h_t.astype(BF16)) * ea_e[:, g * gw:(g + 1) * gw]
            y_ref[:, g * gw:(g + 1) * gw] = jnp.concatenate(ys, axis=1) + y_off
        st = _dot(b_t, xdd[:, g * gw:(g + 1) * gw])
        h_scr[d, g] = h_t * edge_e[:, g * gw:(g + 1) * gw] + st


def _ssd_kernel(*refs, emit_y):
    if emit_y:
        (uf_ref, ub_ref, df_ref, db_ref, h0_ref, bias_ref, a_ref, e_ref, tri_ref,
         yf_ref, yb_ref, h_scr) = refs
    else:
        (uf_ref, ub_ref, df_ref, db_ref, bias_ref, a_ref, e_ref, tri_ref, hout_ref, h_scr) = refs
        yf_ref = yb_ref = None
    c = pl.program_id(1)

    @pl.when(c == 0)
    def _():
        if emit_y:
            h_scr[...] = h0_ref[0]
        else:
            h_scr[...] = jnp.zeros(h_scr.shape, F32)

    _ssd_chunk(uf_ref, df_ref, bias_ref, a_ref, e_ref, tri_ref, h_scr, yf_ref, 0)
    _ssd_chunk(ub_ref, db_ref, bias_ref, a_ref, e_ref, tri_ref, h_scr, yb_ref, 1)
    if not emit_y:
        @pl.when(c == pl.num_programs(1) - 1)
        def _():
            hout_ref[0] = h_scr[...]


def _ssd(u, dtp, h0, bias, a_neg, e_mat, tri, *, batch, seq, q):
    t = u.shape[0]
    nc = seq // q
    emit_y = h0 is not None
    hshape = (N_DIRS, SSD_GROUPS, SSD_STATE, SSD_HEADS_PER_GROUP * SSD_HEAD_DIM)
    fwd = lambda b, c: (b * nc + c, 0)
    bwd = lambda b, c: (b * nc + nc - 1 - c, 0)
    const2 = lambda b, c: (0, 0)
    const3 = lambda b, c: (0, 0, 0)
    in_specs = [pl.BlockSpec((q, XBC_WIDTH), fwd), pl.BlockSpec((q, XBC_WIDTH), bwd),
                pl.BlockSpec((q, LANES), fwd), pl.BlockSpec((q, LANES), bwd)]
    args = [u, u, dtp, dtp]
    if emit_y:
        in_specs.append(pl.BlockSpec((1,) + hshape, lambda b, c: (b, 0, 0, 0, 0)))
        args.append(h0)
    in_specs += [pl.BlockSpec((1, LANES), const2), pl.BlockSpec((1, LANES), const2),
                 pl.BlockSpec((N_DIRS, LANES, SSD_WIDTH), const3),
                 pl.BlockSpec((N_DIRS, q, q), const3)]
    args += [bias, a_neg, e_mat, tri]
    if emit_y:
        out_specs = [pl.BlockSpec((q, SSD_WIDTH), fwd), pl.BlockSpec((q, SSD_WIDTH), bwd)]
        out_shape = [jax.ShapeDtypeStruct((t, SSD_WIDTH), F32)] * 2
    else:
        out_specs = pl.BlockSpec((1,) + hshape, lambda b, c: (b, 0, 0, 0, 0))
        out_shape = jax.ShapeDtypeStruct((batch,) + hshape, F32)
    return pl.pallas_call(
        functools.partial(_ssd_kernel, emit_y=emit_y),
        grid=(batch, nc),
        in_specs=in_specs,
        out_specs=out_specs,
        out_shape=out_shape,
        scratch_shapes=[pltpu.VMEM(hshape, F32)],
        compiler_params=_cparams(("arbitrary", "arbitrary"), 48),
        name="ssd_lat" if emit_y else "ssd_ctx",
    )(*args)


def _outproj_kernel(attn_ref, yf_ref, yb_ref, xs_ref, z_ref, x_ref, w_ref, skip_ref, ng_ref,
                    g1_ref, n2_ref, sc2_ref, sh2_ref, rw_ref, x1_ref, h2_ref, lg_ref):
    y = yf_ref[...] + yb_ref[...] + skip_ref[...] * xs_ref[...].astype(F32)
    y = y * _silu(z_ref[...].astype(F32))
    y = (_rms(y) * ng_ref[...]).astype(BF16)
    m = _dot(attn_ref[...], w_ref[0:ATTN_WIDTH, :]) + _dot(y, w_ref[ATTN_WIDTH:, :])
    x1 = x_ref[...] + g1_ref[0] * m
    x1_ref[...] = x1
    h2 = _rms(x1) * (n2_ref[...] * (1.0 + sc2_ref[0])) + sh2_ref[0]
    _store_rows(h2_ref, h2, _chunks(h2.shape[1]))
    logits = _dot_f32(h2, rw_ref[...])
    lg_ref[...] = logits.T[0:lg_ref.shape[0], :]


def _outproj(attn, yf, yb, u, main, x, w_out, skip_e, ssd_g, g1, n2, sc2, sh2, rw, *, seq):
    t, d = x.shape
    tm = TOK_TILE
    per = seq // tm
    row = lambda i: (i, 0)
    c2 = lambda i: (0, 0)
    bat = lambda i: (i // per, 0, 0)
    in_specs = [pl.BlockSpec((tm, ATTN_WIDTH), row),
                pl.BlockSpec((tm, SSD_WIDTH), row),
                pl.BlockSpec((tm, SSD_WIDTH), row),
                pl.BlockSpec((tm, SSD_WIDTH), row),
                pl.BlockSpec((tm, SSD_WIDTH), lambda i: (i, COL_Z // SSD_WIDTH)),
                pl.BlockSpec((tm, d), row),
                pl.BlockSpec((ATTN_WIDTH + SSD_WIDTH, d), c2),
                pl.BlockSpec((1, SSD_WIDTH), c2),
                pl.BlockSpec((1, SSD_WIDTH), c2),
                pl.BlockSpec((1, 1, d), bat),
                pl.BlockSpec((1, d), c2),
                pl.BlockSpec((1, 1, d), bat),
                pl.BlockSpec((1, 1, d), bat),
                pl.BlockSpec((d, LANES), c2)]
    return pl.pallas_call(
        _outproj_kernel,
        grid=(t // tm,),
        in_specs=in_specs,
        out_specs=[pl.BlockSpec((tm, d), row), pl.BlockSpec((tm * _chunks(d), LANES), row),
                   pl.BlockSpec((N_EXPERTS, tm), lambda i: (0, i))],
        out_shape=[jax.ShapeDtypeStruct((t, d), F32),
                   jax.ShapeDtypeStruct((t * _chunks(d), LANES), U32),
                   jax.ShapeDtypeStruct((N_EXPERTS, t), F32)],
        compiler_params=_cparams(("arbitrary",), 52),
        name="outproj",
    )(attn, yf, yb, u, main, x, w_out, skip_e, ssd_g, g1, n2, sc2, sh2, rw)


def _first_index(hit, iota, size):
    return jnp.min(jnp.where(hit, iota, size), axis=0, keepdims=True)


def _route_kernel(lg_ref, bias_ref, su_ref, idx_ref, w_ref, rank_ref, cnt_ref, carry_scr):
    i = pl.program_id(0)
    tm = lg_ref.shape[1]

    @pl.when(i == 0)
    def _():
        carry_scr[...] = jnp.zeros(carry_scr.shape, F32)

    scores = _sigmoid(lg_ref[...])
    sel = scores + bias_ref[...][:, 0:1]
    e8 = EXPERTS_PER_GROUP
    i8 = lax.broadcasted_iota(I32, (e8, tm), 0).astype(F32)
    gs = []
    for g in range(N_EXPERT_GROUPS):
        blk = sel[g * e8:(g + 1) * e8, :]
        m1 = jnp.max(blk, axis=0, keepdims=True)
        f1 = _first_index(blk == m1, i8, e8)
        m2 = jnp.max(jnp.where(i8 == f1, NEG_BIG, blk), axis=0, keepdims=True)
        gs.append(m1 + m2)
    gsc = jnp.concatenate(gs, axis=0)
    ig = lax.broadcasted_iota(I32, gsc.shape, 0).astype(F32)
    keep = jnp.zeros(gsc.shape, F32)
    for _ in range(TOPK_GROUPS):
        m = jnp.max(gsc, axis=0, keepdims=True)
        f = _first_index(gsc == m, ig, N_EXPERT_GROUPS)
        keep = jnp.where(ig == f, 1.0, keep)
        gsc = jnp.where(ig == f, NEG_BIG, gsc)
    keep_e = jnp.concatenate(
        [jnp.broadcast_to(keep[g:g + 1, :], (e8, tm)) for g in range(N_EXPERT_GROUPS)], axis=0)
    cand = jnp.where(keep_e > 0.5, sel, NEG_BIG)
    ie = lax.broadcasted_iota(I32, cand.shape, 0).astype(F32)
    chosen = jnp.zeros(cand.shape, F32)
    idxs, svals, hits = [], [], []
    for _ in range(TOP_K):
        m = jnp.max(cand, axis=0, keepdims=True)
        f = _first_index(cand == m, ie, N_EXPERTS)
        hit = ie == f
        idxs.append(f)
        svals.append(jnp.sum(jnp.where(hit, scores, 0.0), axis=0, keepdims=True))
        hits.append(hit)
        chosen = jnp.where(hit, 1.0, chosen)
        cand = jnp.where(hit, NEG_BIG, cand)
    s_all = jnp.concatenate(svals, axis=0)
    idx_ref[...] = jnp.concatenate(idxs, axis=0).astype(I32)
    w_ref[...] = ROUTED_SCALE * s_all / jnp.sum(s_all, axis=0, keepdims=True)
    before = _dot(chosen.astype(BF16), su_ref[...]) + carry_scr[...][:, 0:1]
    ranks = [jnp.sum(jnp.where(h, before, 0.0), axis=0, keepdims=True) for h in hits]
    rank_ref[...] = jnp.concatenate(ranks, axis=0).astype(I32)
    carry_scr[...] = carry_scr[...] + jnp.sum(chosen, axis=1, keepdims=True)
    cnt_ref[...] = carry_scr[...].astype(I32)


def _route(logits_t, bias_col):
    e, t = logits_t.shape
    tm = ROUTE_TILE
    su = jnp.triu(jnp.ones((tm, tm), F32), 1).astype(BF16)
    col = lambda i: (0, i)
    return pl.pallas_call(
        _route_kernel,
        grid=(t // tm,),
        in_specs=[pl.BlockSpec((e, tm), col),
                  pl.BlockSpec((e, LANES), lambda i: (0, 0)),
                  pl.BlockSpec((tm, tm), lambda i: (0, 0))],
        out_specs=[pl.BlockSpec((TOP_K, tm), col), pl.BlockSpec((TOP_K, tm), col),
                   pl.BlockSpec((TOP_K, tm), col), pl.BlockSpec((e, LANES), lambda i: (0, 0))],
        out_shape=[jax.ShapeDtypeStruct((TOP_K, t), I32), jax.ShapeDtypeStruct((TOP_K, t), F32),
                   jax.ShapeDtypeStruct((TOP_K, t), I32), jax.ShapeDtypeStruct((e, LANES), I32)],
        scratch_shapes=[pltpu.VMEM((e, LANES), F32)],
        compiler_params=_cparams(("arbitrary",), 40),
        name="route",
    )(logits_t, bias_col, su)


def _slots_kernel(idx_ref, rank_ref, ps_ref, d_ref, dc_ref):
    tm = idx_ref.shape[1]
    ie = lax.broadcasted_iota(I32, (N_EXPERTS, tm), 0)
    ps = ps_ref[...][:, 0:1]
    rows = []
    for k in range(TOP_K):
        hit = ie == idx_ref[k:k + 1, :]
        rows.append(jnp.sum(jnp.where(hit, ps, 0.0), axis=0, keepdims=True))
    dest = jnp.concatenate(rows, axis=0).astype(I32) + rank_ref[...]
    d_ref[0] = dest
    tc = dc_ref.shape[2]
    for s in range(tm // tc):
        dc_ref[s] = dest[:, s * tc:(s + 1) * tc]


def _slots(idx_t, rank_t, ps_col):
    t = idx_t.shape[1]
    tm = TOK_TILE
    tc = COMB_TILE
    col = lambda i: (0, i)
    return pl.pallas_call(
        _slots_kernel,
        grid=(t // tm,),
        in_specs=[pl.BlockSpec((TOP_K, tm), col), pl.BlockSpec((TOP_K, tm), col),
                  pl.BlockSpec((N_EXPERTS, LANES), lambda i: (0, 0))],
        out_specs=[pl.BlockSpec((1, TOP_K, tm), lambda i: (i, 0, 0)),
                   pl.BlockSpec((tm // tc, TOP_K, tc), lambda i: (i, 0, 0))],
        out_shape=[jax.ShapeDtypeStruct((t // tm, TOP_K, tm), I32),
                   jax.ShapeDtypeStruct((t // tc, TOP_K, tc), I32)],
        compiler_params=_cparams(("arbitrary",), 40),
        name="slots",
    )(idx_t, rank_t, ps_col)


def _row_copy(src, dst, sem):
    return pltpu.make_async_copy(src, dst, sem)


def _dispatch_kernel(pend_ref, dest_ref, h_ref, xs_ref, zero_scr, sem, *, nch):
    i = pl.program_id(0)
    tm = h_ref.shape[0] // nch
    tile_rows = zero_scr.shape[0]

    @pl.when(i == 0)
    def _():
        zero_scr[...] = jnp.zeros(zero_scr.shape, zero_scr.dtype)

        def pad_copy(e):
            start_row = pl.multiple_of(pend_ref[e + 1] * nch - tile_rows, tile_rows)
            return _row_copy(zero_scr, xs_ref.at[pl.ds(start_row, tile_rows), :], sem.at[0])

        def start(e, carry):
            @pl.when(pend_ref[e + 1] > pend_ref[e])
            def _():
                pad_copy(e).start()
            return carry

        def wait(e, carry):
            @pl.when(pend_ref[e + 1] > pend_ref[e])
            def _():
                pad_copy(e).wait()
            return carry

        lax.fori_loop(0, N_EXPERTS, start, 0)
        lax.fori_loop(0, N_EXPERTS, wait, 0)

    def tok_copy(t, k):
        src = pl.multiple_of(t * nch, nch)
        dst = pl.multiple_of(dest_ref[0, k, t] * nch, nch)
        return _row_copy(h_ref.at[pl.ds(src, nch), :], xs_ref.at[pl.ds(dst, nch), :], sem.at[1])

    def start(t, carry):
        for k in range(TOP_K):
            tok_copy(t, k).start(priority=k % 2)
        return carry

    def wait(t, carry):
        for k in range(TOP_K):
            tok_copy(t, k).wait()
        return carry

    lax.fori_loop(0, tm, start, 0)
    lax.fori_loop(0, tm, wait, 0)


def _dispatch(pad_end_ext, dest3, h2, n_slots, nch):
    t = h2.shape[0] // nch
    tm = TOK_TILE
    grid_spec = pltpu.PrefetchScalarGridSpec(
        num_scalar_prefetch=1,
        grid=(t // tm,),
        in_specs=[pl.BlockSpec((1, TOP_K, tm), lambda i, pe: (i, 0, 0), memory_space=pltpu.SMEM),
                  pl.BlockSpec((tm * nch, LANES), lambda i, pe: (i, 0))],
        out_specs=pl.BlockSpec(memory_space=pl.ANY),
        scratch_shapes=[pltpu.VMEM((MOE_TILE * nch, LANES), U32), pltpu.SemaphoreType.DMA((2,))],
    )
    return pl.pallas_call(
        functools.partial(_dispatch_kernel, nch=nch),
        grid_spec=grid_spec,
        out_shape=jax.ShapeDtypeStruct((n_slots * nch, LANES), U32),
        compiler_params=_cparams(("arbitrary",), 40),
        name="dispatch",
    )(pad_end_ext, dest3, h2)


def _experts_kernel(first_ref, ntile_ref, xs_ref, wg_ref, wu_ref, wd_ref, ys_ref,
                    wg_scr, wu_scr, wd_scr, wg_buf, wu_buf, wd_buf, xbuf, ybuf, xsem, ysem, wsem):
    e = pl.program_id(0)
    n = ntile_ref[e]
    first = first_ref[e]
    rows = xbuf.shape[1]
    nch = _chunks(wg_scr.shape[0])
    te = rows // nch
    wslot = e % 2

    has_next = e + 1 < pl.num_programs(0)

    def w_copies(ex, s, c):
        out = []
        for i, (src, dst) in enumerate(((wg_ref, wg_buf), (wu_ref, wu_buf), (wd_ref, wd_buf))):
            dr = dst.shape[1] // WEIGHT_CHUNKS
            r0 = pl.multiple_of(c * dr, dr)
            out.append(_row_copy(src.at[ex, pl.ds(r0, dr), :], dst.at[s, pl.ds(r0, dr), :],
                                 wsem.at[s, i]))
        return out

    def start_next_chunk(c):
        @pl.when(has_next & (c < WEIGHT_CHUNKS))
        def _():
            for cp in w_copies(e + 1, 1 - wslot, c):
                cp.start()

    @pl.when(e == 0)
    def _():
        for c in range(WEIGHT_CHUNKS):
            for cp in w_copies(0, 0, c):
                cp.start()

    nx = xbuf.shape[0]
    ny = ybuf.shape[0]

    def x_copy(j):
        r0 = pl.multiple_of((first + j) * rows, rows)
        return _row_copy(xs_ref.at[pl.ds(r0, rows), :], xbuf.at[j % nx], xsem.at[j % nx])

    def y_copy(j):
        r0 = pl.multiple_of((first + j) * rows, rows)
        return _row_copy(ybuf.at[j % ny], ys_ref.at[pl.ds(r0, rows), :], ysem.at[j % ny])

    for j0 in range(nx - 1):
        @pl.when(j0 < n)
        def _():
            x_copy(j0).start(priority=TILE_DMA_PRIORITY)

    for c in range(WEIGHT_CHUNKS):
        for cp in w_copies(e, wslot, c):
            cp.wait()
    wg_scr[...] = wg_buf[wslot].astype(BF16)
    wu_scr[...] = wu_buf[wslot].astype(BF16)
    wd_scr[...] = wd_buf[wslot].astype(BF16)

    @pl.when(n > 0)
    def _():

        def tile(j, carry):
            @pl.when(j + nx - 1 < n)
            def _():
                x_copy(j + nx - 1).start(priority=TILE_DMA_PRIORITY)

            start_next_chunk(j)
            x_copy(j).wait()

            @pl.when(j >= ny)
            def _():
                y_copy(j - ny).wait()

            xb = _load_rows(xbuf.at[j % nx], te, nch).astype(BF16)
            act = _silu(_dot(xb, wg_scr[...])) * _dot(xb, wu_scr[...])
            _store_rows(ybuf.at[j % ny], _dot(act.astype(BF16), wd_scr[...]), nch)
            y_copy(j).start(priority=TILE_DMA_PRIORITY)
            return carry

        lax.fori_loop(0, n, tile, 0)

        for back in range(ny, 0, -1):
            @pl.when(n >= back)
            def _():
                y_copy(n - back).wait()

    def rest(c, carry):
        start_next_chunk(c)
        return carry

    lax.fori_loop(jnp.minimum(n, WEIGHT_CHUNKS), WEIGHT_CHUNKS, rest, 0)


def _experts(first_tile, n_tile, xs, wg, wu, wd):
    n_exp, d, f = wg.shape
    nch = _chunks(d)
    rows = MOE_TILE * nch
    grid_spec = pltpu.PrefetchScalarGridSpec(
        num_scalar_prefetch=2,
        grid=(n_exp,),
        in_specs=[pl.BlockSpec(memory_space=pl.ANY), pl.BlockSpec(memory_space=pl.ANY),
                  pl.BlockSpec(memory_space=pl.ANY), pl.BlockSpec(memory_space=pl.ANY)],
        out_specs=pl.BlockSpec(memory_space=pl.ANY),
        scratch_shapes=[pltpu.VMEM((d, f), BF16), pltpu.VMEM((d, f), BF16), pltpu.VMEM((f, d), BF16),
                        pltpu.VMEM((2, d, f), F32), pltpu.VMEM((2, d, f), F32),
                        pltpu.VMEM((2, f, d), F32),
                        pltpu.VMEM((3, rows, LANES), U32), pltpu.VMEM((2, rows, LANES), U32),
                        pltpu.SemaphoreType.DMA((3,)), pltpu.SemaphoreType.DMA((2,)),
                        pltpu.SemaphoreType.DMA((2, 3))],
    )
    return pl.pallas_call(
        _experts_kernel,
        grid_spec=grid_spec,
        out_shape=jax.ShapeDtypeStruct(xs.shape, U32),
        compiler_params=_cparams(("arbitrary",), 56),
        name="experts",
    )(first_tile, n_tile, xs, wg, wu, wd)


def _combine_kernel(dest_ref, dnext_ref, ys_ref, wt_ref, h_ref, x1_ref, sg_ref, su_ref, sd_ref,
                    g2_ref, fg_ref, eye_ref, o_ref, buf, sem):
    i = pl.program_id(0)
    n = pl.num_programs(0)
    tm = x1_ref.shape[0]
    nch = _chunks(x1_ref.shape[1])
    slot = i % 2

    def row_copy(d_ref, s, t, k):
        src = pl.multiple_of(d_ref[0, k, t] * nch, nch)
        dst = pl.multiple_of(t * nch, nch)
        return _row_copy(ys_ref.at[pl.ds(src, nch), :], buf.at[s, k, pl.ds(dst, nch), :], sem.at[s])

    def start_tile(d_ref, s):
        def body(t, carry):
            for k in range(TOP_K):
                row_copy(d_ref, s, t, k).start(priority=k % 2)
            return carry
        lax.fori_loop(0, tm, body, 0)

    @pl.when(i == 0)
    def _():
        start_tile(dest_ref, 0)

    @pl.when(i + 1 < n)
    def _():
        start_tile(dnext_ref, 1 - slot)

    hb = _load_rows(h_ref, tm, nch).astype(BF16)
    act = _silu(_dot(hb, sg_ref[...])) * _dot(hb, su_ref[...])
    ffn = _dot(act.astype(BF16), sd_ref[...])
    w_col = _dot_rhs_f32(eye_ref[...], wt_ref[...], _NT)

    def wait(t, carry):
        for k in range(TOP_K):
            row_copy(dest_ref, slot, t, k).wait()
        return carry

    lax.fori_loop(0, tm, wait, 0)
    for k in range(TOP_K):
        ffn = ffn + w_col[:, k:k + 1] * _load_rows(buf.at[slot, k], tm, nch)
    x2 = x1_ref[...] + g2_ref[0] * ffn
    o_ref[...] = _rms(x2) * fg_ref[...]


def _combine(dest3, ys, w_t, h2, x1, sg, su, sd, g2, fg, *, seq):
    t, d = x1.shape
    nch = _chunks(d)
    tm = COMB_TILE
    per = seq // tm
    f = sg.shape[1]
    eye = jnp.eye(tm, dtype=BF16)
    row = lambda i: (i, 0)
    c2 = lambda i: (0, 0)
    last = t // tm - 1
    return pl.pallas_call(
        _combine_kernel,
        grid=(t // tm,),
        in_specs=[pl.BlockSpec((1, TOP_K, tm), lambda i: (i, 0, 0), memory_space=pltpu.SMEM),
                  pl.BlockSpec((1, TOP_K, tm), lambda i: (jnp.minimum(i + 1, last), 0, 0),
                               memory_space=pltpu.SMEM),
                  pl.BlockSpec(memory_space=pl.ANY),
                  pl.BlockSpec((TOP_K, tm), lambda i: (0, i)),
                  pl.BlockSpec((tm * nch, LANES), row),
                  pl.BlockSpec((tm, d), row),
                  pl.BlockSpec((d, f), c2), pl.BlockSpec((d, f), c2), pl.BlockSpec((f, d), c2),
                  pl.BlockSpec((1, 1, d), lambda i: (i // per, 0, 0)),
                  pl.BlockSpec((1, d), c2),
                  pl.BlockSpec((tm, tm), c2)],
        out_specs=pl.BlockSpec((tm, d), row),
        out_shape=jax.ShapeDtypeStruct((t, d), F32),
        scratch_shapes=[pltpu.VMEM((2, TOP_K, tm * nch, LANES), U32),
                        pltpu.SemaphoreType.DMA((2,))],
        compiler_params=_cparams(("arbitrary",), 48),
        name="combine",
    )(dest3, dest3, ys, w_t, h2, x1, sg, su, sd, g2, fg, eye)


def _rope_tables(seq):
    n_rows = seq // GRID_W
    rows = np.repeat(np.arange(n_rows, dtype=np.float32), GRID_W)
    cols = np.tile(np.arange(GRID_W, dtype=np.float32), n_rows)
    expo = -np.arange(0, ROPE_AXIS_DIM, 2, dtype=np.float32) / np.float32(ROPE_AXIS_DIM)
    inv = np.power(np.float32(ROPE_THETA), expo).astype(np.float32)
    ar = (rows[:, None] * inv[None, :]).astype(np.float64)
    ac = (cols[:, None] * inv[None, :]).astype(np.float64)
    cos = np.concatenate([np.cos(ar), np.cos(ac), np.cos(ar), np.cos(ac)], axis=1)
    sin = np.concatenate([-np.sin(ar), -np.sin(ac), np.sin(ar), np.sin(ac)], axis=1)
    return jnp.asarray(cos, F32), jnp.asarray(sin, F32)


def _layer(x, c, ctx, c_ctx, w_ada, b_ada, norm1_g, norm2_g, w_in, attn_sink, conv_w, conv_b,
           dt_bias, a_log, d_skip, ssd_norm_g, w_out, router_w, router_bias, wg, wu, wd,
           sw_gate, sw_up, sw_down, final_norm_g):
    batch, seq, d = x.shape
    ctx_len = ctx.shape[1]
    t = batch * seq

    cvec = jnp.zeros((SUBLANES, d), F32).at[:batch].set(c).at[batch].set(c_ctx)
    mod = _ada(cvec, w_ada, b_ada[None, :])
    sh1, sc1, g1, sh2, sc2, g2 = [mod[:batch, i * d:(i + 1) * d][:, None, :] for i in range(6)]
    sh1c, sc1c = [jnp.broadcast_to(mod[batch, i * d:(i + 1) * d][None, None, :], (batch, 1, d))
                  for i in range(2)]

    w_main, w_dt = _wprep(w_in.T)
    cos, sin = _rope_tables(seq)
    n1 = norm1_g[None, :]
    n_main = MAIN_WIDTH // IN_TILE_N
    ctx0 = COL_XBC // IN_TILE_N
    main, dtp = _inproj(x.reshape(t, d), n1, sc1, sh1, w_main, w_dt, cos, sin,
                        seq=seq, tm=min(seq, 1024), rope=True, tile0=0, n_tiles=n_main)
    main_c, dtp_c = _inproj(ctx.reshape(batch * ctx_len, d), n1, sc1c, sh1c, w_main, w_dt, None, None,
                            seq=batch * ctx_len, tm=batch * ctx_len, rope=False, tile0=ctx0,
                            n_tiles=n_main - ctx0)

    attn = _attn(main, main_c, attn_sink, batch=batch, seq=seq, ctx_len=ctx_len)

    cw8 = jnp.pad(conv_w, ((0, SUBLANES - SSD_CONV), (0, 0)))
    u = _conv(main, COL_XBC, cw8, conv_b[None, :], seq=seq)
    u_c = _conv(main_c, 0, cw8, conv_b[None, :], seq=ctx_len)
    nd = N_DIRS * SSD_HEADS
    bias = jnp.pad(dt_bias.reshape(1, nd), ((0, 0), (0, LANES - nd)))
    a_neg = jnp.pad(-jnp.exp(a_log.reshape(1, nd)), ((0, 0), (0, LANES - nd)))
    lane_head = np.arange(SSD_WIDTH) // SSD_HEAD_DIM
    e_np = np.zeros((N_DIRS, LANES, SSD_WIDTH), np.float32)
    for dd in range(N_DIRS):
        e_np[dd, dd * SSD_HEADS + lane_head, np.arange(SSD_WIDTH)] = 1.0
    e_mat = jnp.asarray(e_np, BF16)

    def tri(q):
        lo = np.tril(np.ones((q, q), np.float32))
        return jnp.asarray(np.stack([lo, lo.T]), BF16)

    qc = min(SSD_CHUNK, ctx_len)
    h_ctx = _ssd(u_c, dtp_c, None, bias, a_neg, e_mat, tri(qc), batch=batch, seq=ctx_len, q=qc)
    ql = min(SSD_CHUNK, seq)
    yf, yb = _ssd(u, dtp, h_ctx, bias, a_neg, e_mat, tri(ql), batch=batch, seq=seq, q=ql)

    skip_e = jnp.repeat(d_skip, SSD_HEAD_DIM)[None, :]
    x1, h2, logits_t = _outproj(attn, yf, yb, u, main, x.reshape(t, d), w_out.astype(BF16),
                                skip_e, ssd_norm_g[None, :], g1, norm2_g[None, :], sc2, sh2,
                                jnp.pad(router_w, ((0, 0), (0, LANES - N_EXPERTS))), seq=seq)

    bias_col = jnp.broadcast_to(router_bias[:, None], (N_EXPERTS, LANES))
    idx_t, w_t, rank_t, cnt = _route(logits_t, bias_col)
    counts = cnt[:, 0]
    te = MOE_TILE
    padded = (counts + te - 1) // te * te
    pad_end = jnp.cumsum(padded)
    pad_start = pad_end - padded
    n_slots = -(-(t * TOP_K + N_EXPERTS * (te - 1)) // te) * te
    ps_col = jnp.broadcast_to(pad_start.astype(F32)[:, None], (N_EXPERTS, LANES))
    dest3, dest3c = _slots(idx_t, rank_t, ps_col)
    pad_end_ext = jnp.concatenate([jnp.zeros((1,), I32), pad_end.astype(I32)])

    xs = _dispatch(pad_end_ext, dest3, h2, n_slots, _chunks(d))
    ys = _experts((pad_start // te).astype(I32), (padded // te).astype(I32), xs, wg, wu, wd)
    out = _combine(dest3c, ys, w_t, h2, x1, sw_gate.astype(BF16), sw_up.astype(BF16),
                   sw_down.astype(BF16), g2, final_norm_g[None, :], seq=seq)
    return out.reshape(batch, seq, d)


def kernel(x, c, ctx, c_ctx, w_ada, b_ada, norm1_g, norm2_g, w_in, attn_sink, conv_w, conv_b, dt_bias, a_log, d_skip, ssd_norm_g, w_out, router_w, router_bias, expert_w_gate, expert_w_up, expert_w_down, shared_w_gate, shared_w_up, shared_w_down, final_norm_g):
    assert w_ada.shape[0] == 1, "single-layer stack only"
    return _layer(x, c, ctx, c_ctx, w_ada[0], b_ada[0], norm1_g[0], norm2_g[0], w_in[0],
                  attn_sink[0], conv_w[0], conv_b[0], dt_bias[0], a_log[0], d_skip[0],
                  ssd_norm_g[0], w_out[0], router_w[0], router_bias[0], expert_w_gate[0],
                  expert_w_up[0], expert_w_down[0], shared_w_gate[0], shared_w_up[0],
                  shared_w_down[0], final_norm_g)
```

```python
import functools

import jax
import jax.numpy as jnp
import numpy as np
from jax import lax
from jax.experimental import pallas as pl
from jax.experimental.pallas import tpu as pltpu

F32 = jnp.float32
BF16 = jnp.bfloat16
I32 = jnp.int32

EPS = 1e-6
GRID_W = 64
N_Q_HEADS = 8
N_KV_HEADS = 2
Q_PER_KV = N_Q_HEADS // N_KV_HEADS
HEAD_DIM = 128
ATTN_WIDTH = N_Q_HEADS * HEAD_DIM
KV_WIDTH = N_KV_HEADS * HEAD_DIM
WINDOW = 128
ATTN_BLOCK = 128
ROPE_THETA = 10000.0
ROPE_AXIS_DIM = HEAD_DIM // 2
SSD_HEADS = 16
SSD_HEAD_DIM = 64
SSD_WIDTH = SSD_HEADS * SSD_HEAD_DIM
SSD_GROUPS = 4
SSD_HEADS_PER_GROUP = SSD_HEADS // SSD_GROUPS
SSD_STATE = 128
SSD_CONV = 5
N_DIRS = 2
XBC_WIDTH = SSD_WIDTH + 2 * SSD_GROUPS * SSD_STATE
N_EXPERTS = 64
N_EXPERT_GROUPS = 8
EXPERTS_PER_GROUP = N_EXPERTS // N_EXPERT_GROUPS
TOPK_GROUPS = 4
TOP_K = 8
EXPERT_DIM = 512
ROUTED_SCALE = 2.5

LANES = 128
SUBLANES = 8
NEG_BIG = -1e30

COL_Q = 0
COL_Z = ATTN_WIDTH
COL_XBC = COL_Z + SSD_WIDTH
COL_K = COL_XBC + XBC_WIDTH
COL_V = COL_K + KV_WIDTH
MAIN_WIDTH = COL_V + KV_WIDTH

IN_TILE_N = 512
SSD_CHUNK = 256
MOE_TILE = 256
ROUTE_TILE = 512
TOK_TILE = 256
COMB_TILE = 256
TILE_DMA_PRIORITY = 1
WEIGHT_CHUNKS = 8

_NT = (((1,), (1,)), ((), ()))


def _cparams(sem, vmem_mb):
    return pltpu.CompilerParams(dimension_semantics=sem, vmem_limit_bytes=vmem_mb * 1024 * 1024)


def _dot(a, b, dims=None):
    if dims is None:
        return jnp.dot(a, b, preferred_element_type=F32)
    return lax.dot_general(a, b, dims, preferred_element_type=F32)


def _split(a):
    hi = a.astype(BF16)
    lo = (a - hi.astype(F32)).astype(BF16)
    return hi, lo


def _dot_lhs_f32(a, b_exact, dims=None):
    hi, lo = _split(a)
    return _dot(hi, b_exact, dims) + _dot(lo, b_exact, dims)


def _dot_rhs_f32(a_exact, b, dims=None):
    hi, lo = _split(b)
    return _dot(a_exact, hi, dims) + _dot(a_exact, lo, dims)


def _dot_f32(a, b, dims=None):
    ah, al = _split(a)
    bh, bl = _split(b)
    return _dot(ah, bh, dims) + (_dot(al, bh, dims) + _dot(ah, bl, dims))


def _sigmoid(x):
    return 1.0 / (1.0 + jnp.exp(-x))


def _silu(x):
    return x * _sigmoid(x)


def _rms(x):
    return x * lax.rsqrt(jnp.mean(x * x, axis=-1, keepdims=True) + EPS)


U32 = jnp.uint32
_HI_MASK = 0xFFFF0000


def _chunks(d):
    return d // (2 * LANES)


def _load_rows(ref, rows, nch):
    words = [ref[pl.ds(j, rows, stride=nch), :] for j in range(nch)]
    lo = [lax.bitcast_convert_type(w << 16, F32) for w in words]
    hi = [lax.bitcast_convert_type(w & jnp.uint32(_HI_MASK), F32) for w in words]
    return jnp.concatenate(lo + hi, axis=1)


def _store_rows(ref, val, nch):
    rows, d = val.shape
    bits = lax.bitcast_convert_type(val.astype(BF16).astype(F32), U32)
    for j in range(nch):
        lo = bits[:, j * LANES:(j + 1) * LANES] >> 16
        hi = bits[:, d // 2 + j * LANES:d // 2 + (j + 1) * LANES] & jnp.uint32(_HI_MASK)
        ref[pl.ds(j, rows, stride=nch), :] = lo | hi


def _ada_kernel(c_ref, w_ref, b_ref, o_ref):
    o_ref[...] = _dot_f32(_silu(c_ref[...]), w_ref[...]) + b_ref[...]


def _ada(cvec, w, b):
    m, d = cvec.shape
    n = w.shape[1]
    tn = 1024
    return pl.pallas_call(
        _ada_kernel,
        grid=(n // tn,),
        in_specs=[pl.BlockSpec((m, d), lambda j: (0, 0)),
                  pl.BlockSpec((d, tn), lambda j: (0, j)),
                  pl.BlockSpec((1, tn), lambda j: (0, j))],
        out_specs=pl.BlockSpec((m, tn), lambda j: (0, j)),
        out_shape=jax.ShapeDtypeStruct((m, n), F32),
        compiler_params=_cparams(("arbitrary",), 40),
        name="ada",
    )(cvec, w, b)


def _rope_mix(acc, cos, sin):
    heads = acc.shape[1] // HEAD_DIM
    partner = jnp.concatenate(
        [pltpu.roll(acc[:, h * HEAD_DIM:(h + 1) * HEAD_DIM], HEAD_DIM // 2, 1) for h in range(heads)],
        axis=1)
    return acc * cos + partner * sin


def _inproj_kernel(*refs, rope, n_rope_tiles, kv_tile):
    if rope:
        x_ref, g_ref, sc_ref, sh_ref, w_ref, wdt_ref, cos_ref, sin_ref, o_ref, dt_ref, h_scr = refs
    else:
        x_ref, g_ref, sc_ref, sh_ref, w_ref, wdt_ref, o_ref, dt_ref, h_scr = refs
    j = pl.program_id(1)

    @pl.when(j == 0)
    def _():
        gain = g_ref[...] * (1.0 + sc_ref[0])
        h = _rms(x_ref[...]) * gain + sh_ref[0]
        hb = h.astype(BF16)
        h_scr[...] = hb
        dt_ref[...] = _dot(hb, wdt_ref[...])

    acc = _dot(h_scr[...], w_ref[0])
    if not rope:
        o_ref[...] = acc.astype(o_ref.dtype)
        return
    reps = acc.shape[1] // HEAD_DIM

    @pl.when(j < n_rope_tiles)
    def _():
        cos = jnp.concatenate([cos_ref[...]] * reps, axis=1)
        sin = jnp.concatenate([sin_ref[...]] * reps, axis=1)
        o_ref[...] = _rope_mix(acc, cos, sin).astype(o_ref.dtype)

    @pl.when(j == kv_tile)
    def _():
        nk = KV_WIDTH // HEAD_DIM
        ones = jnp.ones_like(cos_ref[...])
        cos = jnp.concatenate([cos_ref[...]] * nk + [ones] * (reps - nk), axis=1)
        sin = jnp.concatenate([sin_ref[...]] * nk + [ones * 0.0] * (reps - nk), axis=1)
        o_ref[...] = _rope_mix(acc, cos, sin).astype(o_ref.dtype)

    @pl.when((j >= n_rope_tiles) & (j != kv_tile))
    def _():
        o_ref[...] = acc.astype(o_ref.dtype)


def _wprep_kernel(w_ref, wdt_ref, o_ref, odt_ref):
    j = pl.program_id(0)
    x = w_ref[...].T
    tn = x.shape[1]
    q = ROPE_AXIS_DIM // 2
    lane = lax.broadcasted_iota(I32, x.shape, 1)
    blk = (lane & (HEAD_DIM - 1)) // q
    swapped = jnp.where(blk == 1, pltpu.roll(x, tn - q, 1), jnp.where(blk == 2, pltpu.roll(x, q, 1), x))
    n_q = ATTN_WIDTH // tn
    kv = COL_K // tn

    @pl.when(j == 0)
    def _():
        rows = wdt_ref[...]
        pad = jnp.zeros((LANES - rows.shape[0], rows.shape[1]), rows.dtype)
        odt_ref[...] = jnp.concatenate([rows, pad], axis=0).T.astype(odt_ref.dtype)

    @pl.when(j < n_q)
    def _():
        o_ref[0] = swapped.astype(o_ref.dtype)

    @pl.when(j == kv)
    def _():
        o_ref[0] = jnp.where(lane < KV_WIDTH, swapped, x).astype(o_ref.dtype)

    @pl.when((j >= n_q) & (j != kv))
    def _():
        o_ref[0] = x.astype(o_ref.dtype)


def _wprep(w_in_t):
    d = w_in_t.shape[1]
    tn = IN_TILE_N
    n_q = ATTN_WIDTH // tn
    n_tiles = MAIN_WIDTH // tn
    src = lambda j: (jnp.where(j < n_q, j, jnp.where(j < n_tiles - 1, j + 1, n_q)), 0)
    n_dt = w_in_t.shape[0] - MAIN_WIDTH
    return pl.pallas_call(
        _wprep_kernel,
        grid=(n_tiles,),
        in_specs=[pl.BlockSpec((tn, d), src),
                  pl.BlockSpec((n_dt, d), lambda j: (MAIN_WIDTH // n_dt, 0))],
        out_specs=[pl.BlockSpec((1, d, tn), lambda j: (j, 0, 0)),
                   pl.BlockSpec((d, LANES), lambda j: (0, 0))],
        out_shape=[jax.ShapeDtypeStruct((n_tiles, d, tn), BF16),
                   jax.ShapeDtypeStruct((d, LANES), BF16)],
        compiler_params=_cparams(("arbitrary",), 40),
        name="wprep",
    )(w_in_t, w_in_t)


def _inproj(x, g, sc, sh, w, wdt, cos, sin, *, seq, tm, rope, tile0, n_tiles):
    t, d = x.shape
    tn = IN_TILE_N
    wn = n_tiles * tn
    per = seq // tm
    in_specs = [pl.BlockSpec((tm, d), lambda i, j: (i, 0)),
                pl.BlockSpec((1, d), lambda i, j: (0, 0)),
                pl.BlockSpec((1, 1, d), lambda i, j: (i // per, 0, 0)),
                pl.BlockSpec((1, 1, d), lambda i, j: (i // per, 0, 0)),
                pl.BlockSpec((1, d, tn), lambda i, j: (j + tile0, 0, 0)),
                pl.BlockSpec((d, LANES), lambda i, j: (0, 0))]
    args = [x, g, sc, sh, w, wdt]
    if rope:
        in_specs += [pl.BlockSpec((tm, HEAD_DIM), lambda i, j: (i % per, 0)),
                     pl.BlockSpec((tm, HEAD_DIM), lambda i, j: (i % per, 0))]
        args += [cos, sin]
    kern = functools.partial(_inproj_kernel, rope=rope, n_rope_tiles=ATTN_WIDTH // tn,
                             kv_tile=COL_K // tn)
    return pl.pallas_call(
        kern,
        grid=(t // tm, wn // tn),
        in_specs=in_specs,
        out_specs=[pl.BlockSpec((tm, tn), lambda i, j: (i, j)),
                   pl.BlockSpec((tm, LANES), lambda i, j: (i, 0))],
        out_shape=[jax.ShapeDtypeStruct((t, wn), BF16), jax.ShapeDtypeStruct((t, LANES), F32)],
        scratch_shapes=[pltpu.VMEM((tm, d), BF16)],
        compiler_params=_cparams(("arbitrary", "arbitrary"), 48),
        name="inproj_rope" if rope else "inproj",
    )(*args)


def _conv_kernel(x_ref, w_ref, b_ref, o_ref, pad_scr, *, seq, rows):
    halo = SUBLANES
    cw = x_ref.shape[1]
    pad_scr[0:halo, :] = jnp.zeros((halo, cw), F32)
    pad_scr[seq + halo:seq + 2 * halo, :] = jnp.zeros((halo, cw), F32)
    pad_scr[halo:seq + halo, :] = x_ref[...].astype(F32)
    w = w_ref[...]
    bias = b_ref[...]
    for r0 in range(0, seq, rows):
        acc = jnp.broadcast_to(bias, (rows, cw))
        for tap in range(SSD_CONV):
            start = r0 + halo - SSD_CONV // 2 + tap
            acc = acc + w[tap:tap + 1, :] * pad_scr[start:start + rows, :]
        o_ref[r0:r0 + rows, :] = _silu(acc).astype(o_ref.dtype)


def _conv(main, col0, w8, b, *, seq):
    t = main.shape[0]
    cw = 256
    rows = min(seq, 256)
    cb0 = col0 // cw
    kern = functools.partial(_conv_kernel, seq=seq, rows=rows)
    return pl.pallas_call(
        kern,
        grid=(t // seq, XBC_WIDTH // cw),
        in_specs=[pl.BlockSpec((seq, cw), lambda b_, c: (b_, cb0 + c)),
                  pl.BlockSpec((SUBLANES, cw), lambda b_, c: (0, c)),
                  pl.BlockSpec((1, cw), lambda b_, c: (0, c))],
        out_specs=pl.BlockSpec((seq, cw), lambda b_, c: (b_, c)),
        out_shape=jax.ShapeDtypeStruct((t, XBC_WIDTH), BF16),
        scratch_shapes=[pltpu.VMEM((seq + 2 * SUBLANES, cw), F32)],
        compiler_params=_cparams(("arbitrary", "arbitrary"), 40),
        name="conv",
    )(main, w8, b)


def _attn_kernel(sink_ref, bias_ref, q_ref, kp_ref, kc_ref, kn_ref, vp_ref, vc_ref, vn_ref,
                 kx_ref, vx_ref, o_ref):
    blk = ATTN_BLOCK
    g = Q_PER_KV
    hd = HEAD_DIM
    scale = hd ** -0.5
    bias = jnp.concatenate([bias_ref[0]] * g, axis=0)
    row = lax.broadcasted_iota(I32, (g * blk, 1), 0)
    for hk in range(N_KV_HEADS):
        ks = slice(hk * hd, (hk + 1) * hd)
        q = jnp.concatenate([q_ref[:, (hk * g + i) * hd:(hk * g + i + 1) * hd] for i in range(g)],
                            axis=0)
        k_loc = jnp.concatenate([kp_ref[:, ks], kc_ref[:, ks], kn_ref[:, ks]], axis=0)
        v_loc = jnp.concatenate([vp_ref[:, ks], vc_ref[:, ks], vn_ref[:, ks]], axis=0)
        s_loc = _dot(q, k_loc, _NT) * scale + bias
        s_ctx = _dot(q, kx_ref[:, ks], _NT) * scale
        sink = jnp.zeros((g * blk, 1), F32)
        for i in range(g):
            sink = jnp.where((row >= i * blk) & (row < (i + 1) * blk), sink_ref[hk * g + i], sink)
        m = jnp.maximum(jnp.maximum(jnp.max(s_loc, axis=1, keepdims=True),
                                    jnp.max(s_ctx, axis=1, keepdims=True)), sink)
        p_loc = jnp.exp(s_loc - m)
        p_ctx = jnp.exp(s_ctx - m)
        den = (jnp.sum(p_loc, axis=1, keepdims=True) + jnp.sum(p_ctx, axis=1, keepdims=True)
               + jnp.exp(sink - m))
        o = _dot(p_loc.astype(BF16), v_loc) + _dot(p_ctx.astype(BF16), vx_ref[:, ks])
        o = o / den
        for i in range(g):
            o_ref[:, (hk * g + i) * hd:(hk * g + i + 1) * hd] = (
                o[i * blk:(i + 1) * blk, :].astype(o_ref.dtype))


def _attn(main, main_c, sink, *, batch, seq, ctx_len):
    t = main.shape[0]
    blk = ATTN_BLOCK
    nb = seq // blk
    kcol = COL_K // KV_WIDTH
    vcol = COL_V // KV_WIDTH
    kcol_c = XBC_WIDTH // KV_WIDTH
    vcol_c = kcol_c + 1

    def nbr(col, off):
        return pl.BlockSpec(
            (blk, KV_WIDTH), lambda b, i: (b * nb + jnp.clip(i + off, 0, nb - 1), col))

    r = np.arange(blk)[:, None]
    c = np.arange(3 * blk)[None, :]
    variants = []
    for v in range(4):
        lo = np.maximum(r, blk if v & 1 else 0)
        hi = np.minimum(r + 2 * WINDOW, 2 * blk - 1 if v & 2 else 3 * blk - 1)
        variants.append(np.where((c >= lo) & (c <= hi), 0.0, NEG_BIG))
    bias = jnp.asarray(np.stack(variants), F32)
    variant = lambda b, i: ((i == 0).astype(I32) + 2 * (i == nb - 1).astype(I32), 0, 0)

    in_specs = [pl.BlockSpec(memory_space=pltpu.SMEM),
                pl.BlockSpec((1, blk, 3 * blk), variant),
                pl.BlockSpec((blk, ATTN_WIDTH), lambda b, i: (b * nb + i, 0)),
                nbr(kcol, -1), nbr(kcol, 0), nbr(kcol, 1),
                nbr(vcol, -1), nbr(vcol, 0), nbr(vcol, 1),
                pl.BlockSpec((ctx_len, KV_WIDTH), lambda b, i: (b, kcol_c)),
                pl.BlockSpec((ctx_len, KV_WIDTH), lambda b, i: (b, vcol_c))]
    return pl.pallas_call(
        _attn_kernel,
        grid=(batch, nb),
        in_specs=in_specs,
        out_specs=pl.BlockSpec((blk, ATTN_WIDTH), lambda b, i: (b * nb + i, 0)),
        out_shape=jax.ShapeDtypeStruct((t, ATTN_WIDTH), BF16),
        compiler_params=_cparams(("arbitrary", "arbitrary"), 40),
        name="attn",
    )(sink, bias, main, main, main, main, main, main, main, main_c, main_c)


def _softplus(x):
    return jnp.maximum(x, 0.0) + jnp.log1p(jnp.exp(-jnp.abs(x)))


def _ssd_chunk(u_ref, dtp_ref, bias_ref, a_ref, e_ref, tri_ref, h_scr, y_ref, d):
    q = u_ref.shape[0]
    rev = d == 1
    p = SSD_HEAD_DIM
    gw = SSD_HEADS_PER_GROUP * p
    dt = _softplus(dtp_ref[...] + bias_ref[...])
    dta = dt * a_ref[...]
    tri = tri_ref[d]
    a_cum = _dot_rhs_f32(tri, dta)
    a_row = _dot_lhs_f32(dta.T, tri_ref[1 - d])
    edge = a_cum[0:1, :] if rev else a_cum[q - 1:q, :]
    ea = jnp.exp(a_cum)
    w_state = dt * jnp.exp(edge - a_cum)
    e_mat = e_ref[d]
    dt_e = _dot(dt.astype(BF16), e_mat)
    ws_e = _dot(w_state.astype(BF16), e_mat)
    x = u_ref[:, 0:SSD_WIDTH].astype(F32)
    xdd = (x * ws_e).astype(BF16)
    li = lax.broadcasted_iota(I32, (q, q), 0)
    si = lax.broadcasted_iota(I32, (q, q), 1)
    keep = (si >= li) if rev else (si <= li)
    ea_e = _dot_lhs_f32(ea, e_mat)
    edge_e = ea_e[0:1, :] if rev else ea_e[q - 1:q, :]
    if y_ref is not None:
        xd = (x * dt_e).astype(BF16)
    for g in range(SSD_GROUPS):
        b_t = u_ref[:, SSD_WIDTH + g * SSD_STATE:SSD_WIDTH + (g + 1) * SSD_STATE].T
        c0 = SSD_WIDTH + SSD_GROUPS * SSD_STATE + g * SSD_STATE
        c_g = u_ref[:, c0:c0 + SSD_STATE]
        h_t = h_scr[d, g]
        if y_ref is not None:
            cb = _dot(c_g, b_t)
            ys = []
            for r in range(SSD_HEADS_PER_GROUP):
                hd = g * SSD_HEADS_PER_GROUP + r
                ln = d * SSD_HEADS + hd
                seg = a_cum[:, ln:ln + 1] - a_row[ln:ln + 1, :]
                decay = jnp.exp(jnp.where(keep, seg, NEG_BIG))
                ys.append(_dot((cb * decay).astype(BF16), xd[:, hd * p:(hd + 1) * p]))
            y_off = _dot(c_g, h_t.astype(BF16)) * ea_e[:, g * gw:(g + 1) * gw]
            y_ref[:, g * gw:(g + 1) * gw] = jnp.concatenate(ys, axis=1) + y_off
        st = _dot(b_t, xdd[:, g * gw:(g + 1) * gw])
        h_scr[d, g] = h_t * edge_e[:, g * gw:(g + 1) * gw] + st


def _ssd_kernel(*refs, emit_y):
    if emit_y:
        (uf_ref, ub_ref, df_ref, db_ref, h0_ref, bias_ref, a_ref, e_ref, tri_ref,
         yf_ref, yb_ref, h_scr) = refs
    else:
        (uf_ref, ub_ref, df_ref, db_ref, bias_ref, a_ref, e_ref, tri_ref, hout_ref, h_scr) = refs
        yf_ref = yb_ref = None
    c = pl.program_id(1)

    @pl.when(c == 0)
    def _():
        if emit_y:
            h_scr[...] = h0_ref[0]
        else:
            h_scr[...] = jnp.zeros(h_scr.shape, F32)

    _ssd_chunk(uf_ref, df_ref, bias_ref, a_ref, e_ref, tri_ref, h_scr, yf_ref, 0)
    _ssd_chunk(ub_ref, db_ref, bias_ref, a_ref, e_ref, tri_ref, h_scr, yb_ref, 1)
    if not emit_y:
        @pl.when(c == pl.num_programs(1) - 1)
        def _():
            hout_ref[0] = h_scr[...]


def _ssd(u, dtp, h0, bias, a_neg, e_mat, tri, *, batch, seq, q):
    t = u.shape[0]
    nc = seq // q
    emit_y = h0 is not None
    hshape = (N_DIRS, SSD_GROUPS, SSD_STATE, SSD_HEADS_PER_GROUP * SSD_HEAD_DIM)
    fwd = lambda b, c: (b * nc + c, 0)
    bwd = lambda b, c: (b * nc + nc - 1 - c, 0)
    const2 = lambda b, c: (0, 0)
    const3 = lambda b, c: (0, 0, 0)
    in_specs = [pl.BlockSpec((q, XBC_WIDTH), fwd), pl.BlockSpec((q, XBC_WIDTH), bwd),
                pl.BlockSpec((q, LANES), fwd), pl.BlockSpec((q, LANES), bwd)]
    args = [u, u, dtp, dtp]
    if emit_y:
        in_specs.append(pl.BlockSpec((1,) + hshape, lambda b, c: (b, 0, 0, 0, 0)))
        args.append(h0)
    in_specs += [pl.BlockSpec((1, LANES), const2), pl.BlockSpec((1, LANES), const2),
                 pl.BlockSpec((N_DIRS, LANES, SSD_WIDTH), const3),
                 pl.BlockSpec((N_DIRS, q, q), const3)]
    args += [bias, a_neg, e_mat, tri]
    if emit_y:
        out_specs = [pl.BlockSpec((q, SSD_WIDTH), fwd), pl.BlockSpec((q, SSD_WIDTH), bwd)]
        out_shape = [jax.ShapeDtypeStruct((t, SSD_WIDTH), F32)] * 2
    else:
        out_specs = pl.BlockSpec((1,) + hshape, lambda b, c: (b, 0, 0, 0, 0))
        out_shape = jax.ShapeDtypeStruct((batch,) + hshape, F32)
    return pl.pallas_call(
        functools.partial(_ssd_kernel, emit_y=emit_y),
        grid=(batch, nc),
        in_specs=in_specs,
        out_specs=out_specs,
        out_shape=out_shape,
        scratch_shapes=[pltpu.VMEM(hshape, F32)],
        compiler_params=_cparams(("arbitrary", "arbitrary"), 48),
        name="ssd_lat" if emit_y else "ssd_ctx",
    )(*args)


def _outproj_kernel(attn_ref, yf_ref, yb_ref, xs_ref, z_ref, x_ref, w_ref, skip_ref, ng_ref,
                    g1_ref, n2_ref, sc2_ref, sh2_ref, rw_ref, x1_ref, h2_ref, lg_ref):
    y = yf_ref[...] + yb_ref[...] + skip_ref[...] * xs_ref[...].astype(F32)
    y = y * _silu(z_ref[...].astype(F32))
    y = (_rms(y) * ng_ref[...]).astype(BF16)
    m = _dot(attn_ref[...], w_ref[0:ATTN_WIDTH, :]) + _dot(y, w_ref[ATTN_WIDTH:, :])
    x1 = x_ref[...] + g1_ref[0] * m
    x1_ref[...] = x1
    h2 = _rms(x1) * (n2_ref[...] * (1.0 + sc2_ref[0])) + sh2_ref[0]
    _store_rows(h2_ref, h2, _chunks(h2.shape[1]))
    logits = _dot_f32(h2, rw_ref[...])
    lg_ref[...] = logits.T[0:lg_ref.shape[0], :]


def _outproj(attn, yf, yb, u, main, x, w_out, skip_e, ssd_g, g1, n2, sc2, sh2, rw, *, seq):
    t, d = x.shape
    tm = TOK_TILE
    per = seq // tm
    row = lambda i: (i, 0)
    c2 = lambda i: (0, 0)
    bat = lambda i: (i // per, 0, 0)
    in_specs = [pl.BlockSpec((tm, ATTN_WIDTH), row),
                pl.BlockSpec((tm, SSD_WIDTH), row),
                pl.BlockSpec((tm, SSD_WIDTH), row),
                pl.BlockSpec((tm, SSD_WIDTH), row),
                pl.BlockSpec((tm, SSD_WIDTH), lambda i: (i, COL_Z // SSD_WIDTH)),
                pl.BlockSpec((tm, d), row),
                pl.BlockSpec((ATTN_WIDTH + SSD_WIDTH, d), c2),
                pl.BlockSpec((1, SSD_WIDTH), c2),
                pl.BlockSpec((1, SSD_WIDTH), c2),
                pl.BlockSpec((1, 1, d), bat),
                pl.BlockSpec((1, d), c2),
                pl.BlockSpec((1, 1, d), bat),
                pl.BlockSpec((1, 1, d), bat),
                pl.BlockSpec((d, LANES), c2)]
    return pl.pallas_call(
        _outproj_kernel,
        grid=(t // tm,),
        in_specs=in_specs,
        out_specs=[pl.BlockSpec((tm, d), row), pl.BlockSpec((tm * _chunks(d), LANES), row),
                   pl.BlockSpec((N_EXPERTS, tm), lambda i: (0, i))],
        out_shape=[jax.ShapeDtypeStruct((t, d), F32),
                   jax.ShapeDtypeStruct((t * _chunks(d), LANES), U32),
                   jax.ShapeDtypeStruct((N_EXPERTS, t), F32)],
        compiler_params=_cparams(("arbitrary",), 52),
        name="outproj",
    )(attn, yf, yb, u, main, x, w_out, skip_e, ssd_g, g1, n2, sc2, sh2, rw)


def _first_index(hit, iota, size):
    return jnp.min(jnp.where(hit, iota, size), axis=0, keepdims=True)


def _route_kernel(lg_ref, bias_ref, su_ref, idx_ref, w_ref, rank_ref, cnt_ref, carry_scr):
    i = pl.program_id(0)
    tm = lg_ref.shape[1]

    @pl.when(i == 0)
    def _():
        carry_scr[...] = jnp.zeros(carry_scr.shape, F32)

    scores = _sigmoid(lg_ref[...])
    sel = scores + bias_ref[...][:, 0:1]
    e8 = EXPERTS_PER_GROUP
    i8 = lax.broadcasted_iota(I32, (e8, tm), 0).astype(F32)
    gs = []
    for g in range(N_EXPERT_GROUPS):
        blk = sel[g * e8:(g + 1) * e8, :]
        m1 = jnp.max(blk, axis=0, keepdims=True)
        f1 = _first_index(blk == m1, i8, e8)
        m2 = jnp.max(jnp.where(i8 == f1, NEG_BIG, blk), axis=0, keepdims=True)
        gs.append(m1 + m2)
    gsc = jnp.concatenate(gs, axis=0)
    ig = lax.broadcasted_iota(I32, gsc.shape, 0).astype(F32)
    keep = jnp.zeros(gsc.shape, F32)
    for _ in range(TOPK_GROUPS):
        m = jnp.max(gsc, axis=0, keepdims=True)
        f = _first_index(gsc == m, ig, N_EXPERT_GROUPS)
        keep = jnp.where(ig == f, 1.0, keep)
        gsc = jnp.where(ig == f, NEG_BIG, gsc)
    keep_e = jnp.concatenate(
        [jnp.broadcast_to(keep[g:g + 1, :], (e8, tm)) for g in range(N_EXPERT_GROUPS)], axis=0)
    cand = jnp.where(keep_e > 0.5, sel, NEG_BIG)
    ie = lax.broadcasted_iota(I32, cand.shape, 0).astype(F32)
    chosen = jnp.zeros(cand.shape, F32)
    idxs, svals, hits = [], [], []
    for _ in range(TOP_K):
        m = jnp.max(cand, axis=0, keepdims=True)
        f = _first_index(cand == m, ie, N_EXPERTS)
        hit = ie == f
        idxs.append(f)
        svals.append(jnp.sum(jnp.where(hit, scores, 0.0), axis=0, keepdims=True))
        hits.append(hit)
        chosen = jnp.where(hit, 1.0, chosen)
        cand = jnp.where(hit, NEG_BIG, cand)
    s_all = jnp.concatenate(svals, axis=0)
    idx_ref[...] = jnp.concatenate(idxs, axis=0).astype(I32)
    w_ref[...] = ROUTED_SCALE * s_all / jnp.sum(s_all, axis=0, keepdims=True)
    before = _dot(chosen.astype(BF16), su_ref[...]) + carry_scr[...][:, 0:1]
    ranks = [jnp.sum(jnp.where(h, before, 0.0), axis=0, keepdims=True) for h in hits]
    rank_ref[...] = jnp.concatenate(ranks, axis=0).astype(I32)
    carry_scr[...] = carry_scr[...] + jnp.sum(chosen, axis=1, keepdims=True)
    cnt_ref[...] = carry_scr[...].astype(I32)


def _route(logits_t, bias_col):
    e, t = logits_t.shape
    tm = ROUTE_TILE
    su = jnp.triu(jnp.ones((tm, tm), F32), 1).astype(BF16)
    col = lambda i: (0, i)
    return pl.pallas_call(
        _route_kernel,
        grid=(t // tm,),
        in_specs=[pl.BlockSpec((e, tm), col),
                  pl.BlockSpec((e, LANES), lambda i: (0, 0)),
                  pl.BlockSpec((tm, tm), lambda i: (0, 0))],
        out_specs=[pl.BlockSpec((TOP_K, tm), col), pl.BlockSpec((TOP_K, tm), col),
                   pl.BlockSpec((TOP_K, tm), col), pl.BlockSpec((e, LANES), lambda i: (0, 0))],
        out_shape=[jax.ShapeDtypeStruct((TOP_K, t), I32), jax.ShapeDtypeStruct((TOP_K, t), F32),
                   jax.ShapeDtypeStruct((TOP_K, t), I32), jax.ShapeDtypeStruct((e, LANES), I32)],
        scratch_shapes=[pltpu.VMEM((e, LANES), F32)],
        compiler_params=_cparams(("arbitrary",), 40),
        name="route",
    )(logits_t, bias_col, su)


def _slots_kernel(idx_ref, rank_ref, ps_ref, d_ref, dc_ref):
    tm = idx_ref.shape[1]
    ie = lax.broadcasted_iota(I32, (N_EXPERTS, tm), 0)
    ps = ps_ref[...][:, 0:1]
    rows = []
    for k in range(TOP_K):
        hit = ie == idx_ref[k:k + 1, :]
        rows.append(jnp.sum(jnp.where(hit, ps, 0.0), axis=0, keepdims=True))
    dest = jnp.concatenate(rows, axis=0).astype(I32) + rank_ref[...]
    d_ref[0] = dest
    tc = dc_ref.shape[2]
    for s in range(tm // tc):
        dc_ref[s] = dest[:, s * tc:(s + 1) * tc]


def _slots(idx_t, rank_t, ps_col):
    t = idx_t.shape[1]
    tm = TOK_TILE
    tc = COMB_TILE
    col = lambda i: (0, i)
    return pl.pallas_call(
        _slots_kernel,
        grid=(t // tm,),
        in_specs=[pl.BlockSpec((TOP_K, tm), col), pl.BlockSpec((TOP_K, tm), col),
                  pl.BlockSpec((N_EXPERTS, LANES), lambda i: (0, 0))],
        out_specs=[pl.BlockSpec((1, TOP_K, tm), lambda i: (i, 0, 0)),
                   pl.BlockSpec((tm // tc, TOP_K, tc), lambda i: (i, 0, 0))],
        out_shape=[jax.ShapeDtypeStruct((t // tm, TOP_K, tm), I32),
                   jax.ShapeDtypeStruct((t // tc, TOP_K, tc), I32)],
        compiler_params=_cparams(("arbitrary",), 40),
        name="slots",
    )(idx_t, rank_t, ps_col)


def _row_copy(src, dst, sem):
    return pltpu.make_async_copy(src, dst, sem)


def _dispatch_kernel(pend_ref, dest_ref, h_ref, xs_ref, zero_scr, sem, *, nch):
    i = pl.program_id(0)
    tm = h_ref.shape[0] // nch
    tile_rows = zero_scr.shape[0]

    @pl.when(i == 0)
    def _():
        zero_scr[...] = jnp.zeros(zero_scr.shape, zero_scr.dtype)

        def pad_copy(e):
            start_row = pl.multiple_of(pend_ref[e + 1] * nch - tile_rows, tile_rows)
            return _row_copy(zero_scr, xs_ref.at[pl.ds(start_row, tile_rows), :], sem.at[0])

        def start(e, carry):
            @pl.when(pend_ref[e + 1] > pend_ref[e])
            def _():
                pad_copy(e).start()
            return carry

        def wait(e, carry):
            @pl.when(pend_ref[e + 1] > pend_ref[e])
            def _():
                pad_copy(e).wait()
            return carry

        lax.fori_loop(0, N_EXPERTS, start, 0)
        lax.fori_loop(0, N_EXPERTS, wait, 0)

    def tok_copy(t, k):
        src = pl.multiple_of(t * nch, nch)
        dst = pl.multiple_of(dest_ref[0, k, t] * nch, nch)
        return _row_copy(h_ref.at[pl.ds(src, nch), :], xs_ref.at[pl.ds(dst, nch), :], sem.at[1])

    def start(t, carry):
        for k in range(TOP_K):
            tok_copy(t, k).start(priority=k % 2)
        return carry

    def wait(t, carry):
        for k in range(TOP_K):
            tok_copy(t, k).wait()
        return carry

    lax.fori_loop(0, tm, start, 0)
    lax.fori_loop(0, tm, wait, 0)


def _dispatch(pad_end_ext, dest3, h2, n_slots, nch):
    t = h2.shape[0] // nch
    tm = TOK_TILE
    grid_spec = pltpu.PrefetchScalarGridSpec(
        num_scalar_prefetch=1,
        grid=(t // tm,),
        in_specs=[pl.BlockSpec((1, TOP_K, tm), lambda i, pe: (i, 0, 0), memory_space=pltpu.SMEM),
                  pl.BlockSpec((tm * nch, LANES), lambda i, pe: (i, 0))],
        out_specs=pl.BlockSpec(memory_space=pl.ANY),
        scratch_shapes=[pltpu.VMEM((MOE_TILE * nch, LANES), U32), pltpu.SemaphoreType.DMA((2,))],
    )
    return pl.pallas_call(
        functools.partial(_dispatch_kernel, nch=nch),
        grid_spec=grid_spec,
        out_shape=jax.ShapeDtypeStruct((n_slots * nch, LANES), U32),
        compiler_params=_cparams(("arbitrary",), 40),
        name="dispatch",
    )(pad_end_ext, dest3, h2)


def _experts_kernel(first_ref, ntile_ref, xs_ref, wg_ref, wu_ref, wd_ref, ys_ref,
                    wg_scr, wu_scr, wd_scr, wg_buf, wu_buf, wd_buf, xbuf, ybuf, xsem, ysem, wsem):
    e = pl.program_id(0)
    n = ntile_ref[e]
    first = first_ref[e]
    rows = xbuf.shape[1]
    nch = _chunks(wg_scr.shape[0])
    te = rows // nch
    wslot = e % 2

    has_next = e + 1 < pl.num_programs(0)

    def w_copies(ex, s, c):
        out = []
        for i, (src, dst) in enumerate(((wg_ref, wg_buf), (wu_ref, wu_buf), (wd_ref, wd_buf))):
            dr = dst.shape[1] // WEIGHT_CHUNKS
            r0 = pl.multiple_of(c * dr, dr)
            out.append(_row_copy(src.at[ex, pl.ds(r0, dr), :], dst.at[s, pl.ds(r0, dr), :],
                                 wsem.at[s, i]))
        return out

    def start_next_chunk(c):
        @pl.when(has_next & (c < WEIGHT_CHUNKS))
        def _():
            for cp in w_copies(e + 1, 1 - wslot, c):
                cp.start()

    @pl.when(e == 0)
    def _():
        for c in range(WEIGHT_CHUNKS):
            for cp in w_copies(0, 0, c):
                cp.start()

    nx = xbuf.shape[0]
    ny = ybuf.shape[0]

    def x_copy(j):
        r0 = pl.multiple_of((first + j) * rows, rows)
        return _row_copy(xs_ref.at[pl.ds(r0, rows), :], xbuf.at[j % nx], xsem.at[j % nx])

    def y_copy(j):
        r0 = pl.multiple_of((first + j) * rows, rows)
        return _row_copy(ybuf.at[j % ny], ys_ref.at[pl.ds(r0, rows), :], ysem.at[j % ny])

    for j0 in range(nx - 1):
        @pl.when(j0 < n)
        def _():
            x_copy(j0).start(priority=TILE_DMA_PRIORITY)

    for c in range(WEIGHT_CHUNKS):
        for cp in w_copies(e, wslot, c):
            cp.wait()
    wg_scr[...] = wg_buf[wslot].astype(BF16)
    wu_scr[...] = wu_buf[wslot].astype(BF16)
    wd_scr[...] = wd_buf[wslot].astype(BF16)

    @pl.when(n > 0)
    def _():

        def tile(j, carry):
            @pl.when(j + nx - 1 < n)
            def _():
                x_copy(j + nx - 1).start(priority=TILE_DMA_PRIORITY)

            start_next_chunk(j)
            x_copy(j).wait()

            @pl.when(j >= ny)
            def _():
                y_copy(j - ny).wait()

            xb = _load_rows(xbuf.at[j % nx], te, nch).astype(BF16)
            act = _silu(_dot(xb, wg_scr[...])) * _dot(xb, wu_scr[...])
            _store_rows(ybuf.at[j % ny], _dot(act.astype(BF16), wd_scr[...]), nch)
            y_copy(j).start(priority=TILE_DMA_PRIORITY)
            return carry

        lax.fori_loop(0, n, tile, 0)

        for back in range(ny, 0, -1):
            @pl.when(n >= back)
            def _():
                y_copy(n - back).wait()

    def rest(c, carry):
        start_next_chunk(c)
        return carry

    lax.fori_loop(jnp.minimum(n, WEIGHT_CHUNKS), WEIGHT_CHUNKS, rest, 0)


def _experts(first_tile, n_tile, xs, wg, wu, wd):
    n_exp, d, f = wg.shape
    nch = _chunks(d)
    rows = MOE_TILE * nch
    grid_spec = pltpu.PrefetchScalarGridSpec(
        num_scalar_prefetch=2,
        grid=(n_exp,),
        in_specs=[pl.BlockSpec(memory_space=pl.ANY), pl.BlockSpec(memory_space=pl.ANY),
                  pl.BlockSpec(memory_space=pl.ANY), pl.BlockSpec(memory_space=pl.ANY)],
        out_specs=pl.BlockSpec(memory_space=pl.ANY),
        scratch_shapes=[pltpu.VMEM((d, f), BF16), pltpu.VMEM((d, f), BF16), pltpu.VMEM((f, d), BF16),
                        pltpu.VMEM((2, d, f), F32), pltpu.VMEM((2, d, f), F32),
                        pltpu.VMEM((2, f, d), F32),
                        pltpu.VMEM((3, rows, LANES), U32), pltpu.VMEM((2, rows, LANES), U32),
                        pltpu.SemaphoreType.DMA((3,)), pltpu.SemaphoreType.DMA((2,)),
                        pltpu.SemaphoreType.DMA((2, 3))],
    )
    return pl.pallas_call(
        _experts_kernel,
        grid_spec=grid_spec,
        out_shape=jax.ShapeDtypeStruct(xs.shape, U32),
        compiler_params=_cparams(("arbitrary",), 56),
        name="experts",
    )(first_tile, n_tile, xs, wg, wu, wd)


def _combine_kernel(dest_ref, dnext_ref, ys_ref, wt_ref, h_ref, x1_ref, sg_ref, su_ref, sd_ref,
                    g2_ref, fg_ref, eye_ref, o_ref, buf, sem):
    i = pl.program_id(0)
    n = pl.num_programs(0)
    tm = x1_ref.shape[0]
    nch = _chunks(x1_ref.shape[1])
    slot = i % 2

    def row_copy(d_ref, s, t, k):
        src = pl.multiple_of(d_ref[0, k, t] * nch, nch)
        dst = pl.multiple_of(t * nch, nch)
        return _row_copy(ys_ref.at[pl.ds(src, nch), :], buf.at[s, k, pl.ds(dst, nch), :], sem.at[s])

    def start_tile(d_ref, s):
        def body(t, carry):
            for k in range(TOP_K):
                row_copy(d_ref, s, t, k).start(priority=k % 2)
            return carry
        lax.fori_loop(0, tm, body, 0)

    @pl.when(i == 0)
    def _():
        start_tile(dest_ref, 0)

    @pl.when(i + 1 < n)
    def _():
        start_tile(dnext_ref, 1 - slot)

    hb = _load_rows(h_ref, tm, nch).astype(BF16)
    act = _silu(_dot(hb, sg_ref[...])) * _dot(hb, su_ref[...])
    ffn = _dot(act.astype(BF16), sd_ref[...])
    w_col = _dot_rhs_f32(eye_ref[...], wt_ref[...], _NT)

    def wait(t, carry):
        for k in range(TOP_K):
            row_copy(dest_ref, slot, t, k).wait()
        return carry

    lax.fori_loop(0, tm, wait, 0)
    for k in range(TOP_K):
        ffn = ffn + w_col[:, k:k + 1] * _load_rows(buf.at[slot, k], tm, nch)
    x2 = x1_ref[...] + g2_ref[0] * ffn
    o_ref[...] = _rms(x2) * fg_ref[...]


def _combine(dest3, ys, w_t, h2, x1, sg, su, sd, g2, fg, *, seq):
    t, d = x1.shape
    nch = _chunks(d)
    tm = COMB_TILE
    per = seq // tm
    f = sg.shape[1]
    eye = jnp.eye(tm, dtype=BF16)
    row = lambda i: (i, 0)
    c2 = lambda i: (0, 0)
    last = t // tm - 1
    return pl.pallas_call(
        _combine_kernel,
        grid=(t // tm,),
        in_specs=[pl.BlockSpec((1, TOP_K, tm), lambda i: (i, 0, 0), memory_space=pltpu.SMEM),
                  pl.BlockSpec((1, TOP_K, tm), lambda i: (jnp.minimum(i + 1, last), 0, 0),
                               memory_space=pltpu.SMEM),
                  pl.BlockSpec(memory_space=pl.ANY),
                  pl.BlockSpec((TOP_K, tm), lambda i: (0, i)),
                  pl.BlockSpec((tm * nch, LANES), row),
                  pl.BlockSpec((tm, d), row),
                  pl.BlockSpec((d, f), c2), pl.BlockSpec((d, f), c2), pl.BlockSpec((f, d), c2),
                  pl.BlockSpec((1, 1, d), lambda i: (i // per, 0, 0)),
                  pl.BlockSpec((1, d), c2),
                  pl.BlockSpec((tm, tm), c2)],
        out_specs=pl.BlockSpec((tm, d), row),
        out_shape=jax.ShapeDtypeStruct((t, d), F32),
        scratch_shapes=[pltpu.VMEM((2, TOP_K, tm * nch, LANES), U32),
                        pltpu.SemaphoreType.DMA((2,))],
        compiler_params=_cparams(("arbitrary",), 56),
        name="combine",
    )(dest3, dest3, ys, w_t, h2, x1, sg, su, sd, g2, fg, eye)


def _rope_tables(seq):
    n_rows = seq // GRID_W
    rows = np.repeat(np.arange(n_rows, dtype=np.float32), GRID_W)
    cols = np.tile(np.arange(GRID_W, dtype=np.float32), n_rows)
    expo = -np.arange(0, ROPE_AXIS_DIM, 2, dtype=np.float32) / np.float32(ROPE_AXIS_DIM)
    inv = np.power(np.float32(ROPE_THETA), expo).astype(np.float32)
    ar = (rows[:, None] * inv[None, :]).astype(np.float64)
    ac = (cols[:, None] * inv[None, :]).astype(np.float64)
    cos = np.concatenate([np.cos(ar), np.cos(ac), np.cos(ar), np.cos(ac)], axis=1)
    sin = np.concatenate([-np.sin(ar), -np.sin(ac), np.sin(ar), np.sin(ac)], axis=1)
    return jnp.asarray(cos, F32), jnp.asarray(sin, F32)


def _layer(x, c, ctx, c_ctx, w_ada, b_ada, norm1_g, norm2_g, w_in, attn_sink, conv_w, conv_b,
           dt_bias, a_log, d_skip, ssd_norm_g, w_out, router_w, router_bias, wg, wu, wd,
           sw_gate, sw_up, sw_down, final_norm_g):
    batch, seq, d = x.shape
    ctx_len = ctx.shape[1]
    t = batch * seq

    cvec = jnp.zeros((SUBLANES, d), F32).at[:batch].set(c).at[batch].set(c_ctx)
    mod = _ada(cvec, w_ada, b_ada[None, :])
    sh1, sc1, g1, sh2, sc2, g2 = [mod[:batch, i * d:(i + 1) * d][:, None, :] for i in range(6)]
    sh1c, sc1c = [jnp.broadcast_to(mod[batch, i * d:(i + 1) * d][None, None, :], (batch, 1, d))
                  for i in range(2)]

    w_main, w_dt = _wprep(w_in.T)
    cos, sin = _rope_tables(seq)
    n1 = norm1_g[None, :]
    n_main = MAIN_WIDTH // IN_TILE_N
    ctx0 = COL_XBC // IN_TILE_N
    main, dtp = _inproj(x.reshape(t, d), n1, sc1, sh1, w_main, w_dt, cos, sin,
                        seq=seq, tm=min(seq, 1024), rope=True, tile0=0, n_tiles=n_main)
    main_c, dtp_c = _inproj(ctx.reshape(batch * ctx_len, d), n1, sc1c, sh1c, w_main, w_dt, None, None,
                            seq=batch * ctx_len, tm=batch * ctx_len, rope=False, tile0=ctx0,
                            n_tiles=n_main - ctx0)

    attn = _attn(main, main_c, attn_sink, batch=batch, seq=seq, ctx_len=ctx_len)

    cw8 = jnp.pad(conv_w, ((0, SUBLANES - SSD_CONV), (0, 0)))
    u = _conv(main, COL_XBC, cw8, conv_b[None, :], seq=seq)
    u_c = _conv(main_c, 0, cw8, conv_b[None, :], seq=ctx_len)
    nd = N_DIRS * SSD_HEADS
    bias = jnp.pad(dt_bias.reshape(1, nd), ((0, 0), (0, LANES - nd)))
    a_neg = jnp.pad(-jnp.exp(a_log.reshape(1, nd)), ((0, 0), (0, LANES - nd)))
    lane_head = np.arange(SSD_WIDTH) // SSD_HEAD_DIM
    e_np = np.zeros((N_DIRS, LANES, SSD_WIDTH), np.float32)
    for dd in range(N_DIRS):
        e_np[dd, dd * SSD_HEADS + lane_head, np.arange(SSD_WIDTH)] = 1.0
    e_mat = jnp.asarray(e_np, BF16)

    def tri(q):
        lo = np.tril(np.ones((q, q), np.float32))
        return jnp.asarray(np.stack([lo, lo.T]), BF16)

    qc = min(SSD_CHUNK, ctx_len)
    h_ctx = _ssd(u_c, dtp_c, None, bias, a_neg, e_mat, tri(qc), batch=batch, seq=ctx_len, q=qc)
    ql = min(SSD_CHUNK, seq)
    yf, yb = _ssd(u, dtp, h_ctx, bias, a_neg, e_mat, tri(ql), batch=batch, seq=seq, q=ql)

    skip_e = jnp.repeat(d_skip, SSD_HEAD_DIM)[None, :]
    x1, h2, logits_t = _outproj(attn, yf, yb, u, main, x.reshape(t, d), w_out.astype(BF16),
                                skip_e, ssd_norm_g[None, :], g1, norm2_g[None, :], sc2, sh2,
                                jnp.pad(router_w, ((0, 0), (0, LANES - N_EXPERTS))), seq=seq)

    bias_col = jnp.broadcast_to(router_bias[:, None], (N_EXPERTS, LANES))
    idx_t, w_t, rank_t, cnt = _route(logits_t, bias_col)
    counts = cnt[:, 0]
    te = MOE_TILE
    padded = (counts + te - 1) // te * te
    pad_end = jnp.cumsum(padded)
    pad_start = pad_end - padded
    n_slots = -(-(t * TOP_K + N_EXPERTS * (te - 1)) // te) * te
    ps_col = jnp.broadcast_to(pad_start.astype(F32)[:, None], (N_EXPERTS, LANES))
    dest3, dest3c = _slots(idx_t, rank_t, ps_col)
    pad_end_ext = jnp.concatenate([jnp.zeros((1,), I32), pad_end.astype(I32)])

    xs = _dispatch(pad_end_ext, dest3, h2, n_slots, _chunks(d))
    ys = _experts((pad_start // te).astype(I32), (padded // te).astype(I32), xs, wg, wu, wd)
    out = _combine(dest3c, ys, w_t, h2, x1, sw_gate.astype(BF16), sw_up.astype(BF16),
                   sw_down.astype(BF16), g2, final_norm_g[None, :], seq=seq)
    return out.reshape(batch, seq, d)


def kernel(x, c, ctx, c_ctx, w_ada, b_ada, norm1_g, norm2_g, w_in, attn_sink, conv_w, conv_b, dt_bias, a_log, d_skip, ssd_norm_g, w_out, router_w, router_bias, expert_w_gate, expert_w_up, expert_w_down, shared_w_gate, shared_w_up, shared_w_down, final_norm_g):
    assert w_ada.shape[0] == 1, "single-layer stack only"
    return _layer(x, c, ctx, c_ctx, w_ada[0], b_ada[0], norm1_g[0], norm2_g[0], w_in[0],
                  attn_sink[0], conv_w[0], conv_b[0], dt_bias[0], a_log[0], d_skip[0],
                  ssd_norm_g[0], w_out[0], router_w[0], router_bias[0], expert_w_gate[0],
                  expert_w_up[0], expert_w_down[0], shared_w_gate[0], shared_w_up[0],
                  shared_w_down[0], final_norm_g)
```
